```python
import math
import jax, jax.numpy as jnp
from jax import lax
import numpy as np

D_MODEL = 1024
BATCH = 1
SEQ = 16384
DEPTH = 4

HEAD_DIM = 64
A_HEADS = 8
MOBA_BLOCK = 256
MOBA_TOPK = 3
MOBA_QCHUNK = 64
B_HEADS = 8
B_KV_HEADS = 2
B_WINDOW = 128
C_GROUPS = ((128, 1), (512, 4), (2048, 16))
N_C_GROUPS = 3
C_HEADS_PER_GROUP = 4
C_HEADS = N_C_GROUPS * C_HEADS_PER_GROUP
C_MAX_DILATION = 16
BAND = 128
D_FF = ((8 * D_MODEL + 3 * 256 - 1) // (3 * 256)) * 256
A_QKV = 3 * A_HEADS * HEAD_DIM
B_Q = B_HEADS * HEAD_DIM
B_KV = 2 * B_KV_HEADS * HEAD_DIM
C_QKV = 3 * C_HEADS * HEAD_DIM
N_BRANCH = 3
GATE_W = N_BRANCH * D_MODEL
IN_WIDTH = A_QKV + B_Q + B_KV + C_QKV + GATE_W
N_ALIBI_HEADS = A_HEADS + B_HEADS + C_HEADS
SEQ_MULTIPLE = math.lcm(BAND * C_MAX_DILATION, MOBA_BLOCK, MOBA_QCHUNK)
RMS_EPS = 1e-6

kernel_name = "hybrid_moba_swa_dilated_gated_trunk"


def alibi_slopes():
    n = N_ALIBI_HEADS
    s = 2.0 ** (-8.0 * (np.arange(n) + 1) / n)
    return jnp.asarray(s, dtype=jnp.float32)


def rmsnorm(x, g):
    xf = x.astype(jnp.float32)
    y = xf * lax.rsqrt(jnp.mean(xf * xf, axis=-1, keepdims=True) + RMS_EPS)
    return (y * g.astype(jnp.float32)).astype(x.dtype)


def banded_attention(q, k, v, slopes, max_steps, dilation, sinks=None):
    Bn, S, Hq, Dh = q.shape
    Hkv = k.shape[2]
    G = Hq // Hkv
    L = S // dilation
    nb = L // BAND

    def split(a):
        h = a.shape[2]
        a = a.reshape(Bn, L, dilation, h, Dh).transpose(0, 2, 1, 3, 4)
        return a.reshape(Bn, dilation, nb, BAND, h, Dh)

    def with_prev(a):
        prev = jnp.concatenate([jnp.zeros_like(a[:, :, :1]), a[:, :, :-1]], axis=2)
        return jnp.concatenate([prev, a], axis=3)

    qb = split(q).reshape(Bn, dilation, nb, BAND, Hkv, G, Dh)
    kk = with_prev(split(k))
    vv = with_prev(split(v))
    s = jnp.einsum('brnihgd,brnjhd->brnhgij', qb, kk).astype(jnp.float32) * (Dh ** -0.5)
    i = jnp.arange(BAND)[:, None]
    j = jnp.arange(2 * BAND)[None, :]
    steps = i + BAND - j
    blk = jnp.arange(nb)[:, None, None]
    mask = (steps >= 0) & (steps <= max_steps) & ((blk > 0) | (j >= BAND))
    bias = -slopes.reshape(Hkv, G)[:, :, None, None] * (steps * dilation).astype(jnp.float32)
    s = jnp.where(mask[:, None, None], s + bias, -jnp.inf)
    m = jnp.max(s, axis=-1)
    if sinks is not None:
        sink = sinks.reshape(Hkv, G, 1).astype(jnp.float32)
        m = jnp.maximum(m, sink)
    p = jnp.exp(s - m[..., None])
    denom = jnp.sum(p, axis=-1)
    if sinks is not None:
        denom = denom + jnp.exp(sink - m)
    o = jnp.einsum('brnhgij,brnjhd->brnihgd', p.astype(v.dtype), vv).astype(jnp.float32)
    denom_t = denom.transpose(0, 1, 2, 5, 3, 4)
    o = o / denom_t[..., None]
    lse = m.transpose(0, 1, 2, 5, 3, 4) + jnp.log(denom_t)
    o = o.reshape(Bn, dilation, L, Hq, Dh).transpose(0, 2, 1, 3, 4).reshape(Bn, S, Hq, Dh)
    lse = lse.reshape(Bn, dilation, L, Hq).transpose(0, 2, 1, 3).reshape(Bn, S, Hq)
    return o, lse


def moba_attention(q, k, v, slopes):
    Bn, S, H, Dh = q.shape
    nblk = S // MOBA_BLOCK
    ksel = min(MOBA_TOPK, nblk)
    nc = S // MOBA_QCHUNK
    scale = Dh ** -0.5
    kbh = k.transpose(0, 2, 1, 3).reshape(Bn, H, nblk, MOBA_BLOCK, Dh)
    vbh = v.transpose(0, 2, 1, 3).reshape(Bn, H, nblk, MOBA_BLOCK, Dh)
    kmean = jnp.mean(kbh.astype(jnp.float32), axis=3)
    gate = jnp.einsum('bshd,bhnd->bhsn', q.astype(jnp.float32), kmean)
    own = jnp.arange(S) // MOBA_BLOCK
    past = jnp.arange(nblk)[None, :] < own[:, None]
    gate = jnp.where(past, gate, -jnp.inf)
    _, sel = lax.top_k(gate, ksel)
    q_c = q.reshape(Bn, nc, MOBA_QCHUNK, H, Dh).transpose(1, 0, 3, 2, 4)
    sel_c = sel.reshape(Bn, H, nc, MOBA_QCHUNK, ksel).transpose(2, 0, 1, 3, 4)
    bi = jnp.arange(Bn)[:, None, None, None]
    hi = jnp.arange(H)[None, :, None, None]
    slope = slopes.astype(jnp.float32)[None, :, None, None]

    def chunk(args):
        ci, qc, sc = args
        t = ci * MOBA_QCHUNK + jnp.arange(MOBA_QCHUNK)
        own_blk = (ci * MOBA_QCHUNK) // MOBA_BLOCK
        k_sel = kbh[bi, hi, sc]
        v_sel = vbh[bi, hi, sc]
        k_own = lax.dynamic_index_in_dim(kbh, own_blk, axis=2, keepdims=False)
        v_own = lax.dynamic_index_in_dim(vbh, own_blk, axis=2, keepdims=False)
        pos_sel = sc[..., None] * MOBA_BLOCK + jnp.arange(MOBA_BLOCK)
        valid = (jnp.arange(ksel)[None, :] < (t // MOBA_BLOCK)[:, None])[None, None, :, :, None]
        s_sel = jnp.einsum('bhqd,bhqjkd->bhqjk', qc, k_sel).astype(jnp.float32) * scale
        s_sel = jnp.where(valid, s_sel - slope[..., None] * (t[:, None, None] - pos_sel).astype(jnp.float32), -jnp.inf)
        pos_own = own_blk * MOBA_BLOCK + jnp.arange(MOBA_BLOCK)
        d_own = t[:, None] - pos_own[None, :]
        s_own = jnp.einsum('bhqd,bhkd->bhqk', qc, k_own).astype(jnp.float32) * scale
        s_own = jnp.where(d_own >= 0, s_own - slope * d_own.astype(jnp.float32), -jnp.inf)
        s_all = jnp.concatenate([s_sel.reshape(Bn, H, MOBA_QCHUNK, ksel * MOBA_BLOCK), s_own], axis=-1)
        p = jax.nn.softmax(s_all, axis=-1).astype(v.dtype)
        p_sel = p[..., :ksel * MOBA_BLOCK].reshape(Bn, H, MOBA_QCHUNK, ksel, MOBA_BLOCK)
        p_own = p[..., ksel * MOBA_BLOCK:]
        return (jnp.einsum('bhqjk,bhqjkd->bhqd', p_sel, v_sel)
                + jnp.einsum('bhqk,bhkd->bhqd', p_own, v_own))

    o = lax.map(chunk, (jnp.arange(nc), q_c, sel_c))
    return o.transpose(1, 0, 3, 2, 4).reshape(Bn, S, H, Dh)


def dilated_attention(q, k, v, slopes):
    outs, lses = [], []
    for g, (w, d) in enumerate(C_GROUPS):
        o, lse = banded_attention(q[:, :, g], k[:, :, g], v[:, :, g],
                                  slopes[g * C_HEADS_PER_GROUP:(g + 1) * C_HEADS_PER_GROUP], w // d, d)
        outs.append(o)
        lses.append(lse)
    wts = jax.nn.softmax(jnp.stack(lses, axis=0), axis=0)
    return jnp.sum(wts[..., None] * jnp.stack(outs, axis=0), axis=0)


def token_mixers(u, w_in, sinks, w_br_a, w_br_b, w_br_c, w_out, slopes):
    Bn, S, _ = u.shape
    z = jnp.dot(u, w_in)
    o1 = A_QKV
    o2 = o1 + B_Q
    o3 = o2 + B_KV
    o4 = o3 + C_QKV
    za, zbq, zbkv, zc, zg = jnp.split(z, [o1, o2, o3, o4], axis=-1)
    slopes_b = slopes[:B_HEADS]
    slopes_c = slopes[B_HEADS:B_HEADS + C_HEADS]
    slopes_a = slopes[B_HEADS + C_HEADS:]
    za = za.reshape(Bn, S, 3, A_HEADS, HEAD_DIM)
    oa = moba_attention(za[:, :, 0], za[:, :, 1], za[:, :, 2], slopes_a).astype(u.dtype)
    ya = jnp.dot(oa.reshape(Bn, S, A_HEADS * HEAD_DIM), w_br_a)
    qb = zbq.reshape(Bn, S, B_HEADS, HEAD_DIM)
    zbkv = zbkv.reshape(Bn, S, 2, B_KV_HEADS, HEAD_DIM)
    ob, _ = banded_attention(qb, zbkv[:, :, 0], zbkv[:, :, 1], slopes_b, B_WINDOW - 1, 1, sinks)
    yb = jnp.dot(ob.astype(u.dtype).reshape(Bn, S, B_HEADS * HEAD_DIM), w_br_b)
    zc = zc.reshape(Bn, S, N_C_GROUPS, 3, C_HEADS_PER_GROUP, HEAD_DIM)
    oc = dilated_attention(zc[:, :, :, 0], zc[:, :, :, 1], zc[:, :, :, 2], slopes_c).astype(u.dtype)
    yc = jnp.dot(oc.reshape(Bn, S, C_HEADS_PER_GROUP * HEAD_DIM), w_br_c)
    gates = jax.nn.sigmoid(zg.astype(jnp.float32)).astype(u.dtype).reshape(Bn, S, N_BRANCH, D_MODEL)
    merged = gates[:, :, 0] * ya + gates[:, :, 1] * yb + gates[:, :, 2] * yc
    return jnp.dot(merged, w_out)


def swiglu(u, w_gate, w_up, w_down):
    return jnp.dot(jax.nn.silu(jnp.dot(u, w_gate)) * jnp.dot(u, w_up), w_down)


def setup_inputs(seed: int = 0) -> dict:
    key = jax.random.key(seed)
    ks = jax.random.split(key, 20)
    f32 = jnp.float32
    D = D_MODEL

    def nrm(k, shape, scale):
        return jax.random.normal(k, shape, f32) * scale

    return {
        "x": nrm(ks[0], (BATCH, SEQ, D), 1.0),
        "c": nrm(ks[1], (BATCH, D), 1.0),
        "w_ada": nrm(ks[2], (DEPTH, D, 6 * D), 0.5 * D ** -0.5),
        "b_ada": nrm(ks[3], (DEPTH, 6 * D), 0.01),
        "g_pre_mix": 1.0 + nrm(ks[4], (DEPTH, D), 0.05),
        "g_post_mix": 1.0 + nrm(ks[5], (DEPTH, D), 0.05),
        "w_in": nrm(ks[6], (DEPTH, D, IN_WIDTH), D ** -0.5),
        "sinks": nrm(ks[7], (DEPTH, B_HEADS), 1.0),
        "w_br_a": nrm(ks[8], (DEPTH, A_HEADS * HEAD_DIM, D), (A_HEADS * HEAD_DIM) ** -0.5),
        "w_br_b": nrm(ks[9], (DEPTH, B_HEADS * HEAD_DIM, D), (B_HEADS * HEAD_DIM) ** -0.5),
        "w_br_c": nrm(ks[10], (DEPTH, C_HEADS_PER_GROUP * HEAD_DIM, D), (C_HEADS_PER_GROUP * HEAD_DIM) ** -0.5),
        "w_out": nrm(ks[11], (DEPTH, D, D), D ** -0.5),
        "g_pre_ffn": 1.0 + nrm(ks[12], (DEPTH, D), 0.05),
        "g_post_ffn": 1.0 + nrm(ks[13], (DEPTH, D), 0.05),
        "w_gate": nrm(ks[14], (DEPTH, D, D_FF), D ** -0.5),
        "w_up": nrm(ks[15], (DEPTH, D, D_FF), D ** -0.5),
        "w_down": nrm(ks[16], (DEPTH, D_FF, D), D_FF ** -0.5),
    }


def reference(x, c, w_ada, b_ada, g_pre_mix, g_post_mix, w_in, sinks, w_br_a, w_br_b, w_br_c,
              w_out, g_pre_ffn, g_post_ffn, w_gate, w_up, w_down):
    Bn, S, _ = x.shape
    S_pad = -(-S // SEQ_MULTIPLE) * SEQ_MULTIPLE
    h = jnp.pad(x, ((0, 0), (0, S_pad - S), (0, 0)))
    slopes = alibi_slopes()
    for l in range(DEPTH):
        mod = jnp.dot(jax.nn.silu(c), w_ada[l]) + b_ada[l]
        sh1, sc1, gt1, sh2, sc2, gt2 = [m[:, None, :] for m in jnp.split(mod, 6, axis=-1)]
        u = rmsnorm(h, g_pre_mix[l]) * (1.0 + sc1) + sh1
        y = token_mixers(u, w_in[l], sinks[l], w_br_a[l], w_br_b[l], w_br_c[l], w_out[l], slopes)
        h = h + gt1 * rmsnorm(y, g_post_mix[l])
        u = rmsnorm(h, g_pre_ffn[l]) * (1.0 + sc2) + sh2
        y = swiglu(u, w_gate[l], w_up[l], w_down[l])
        h = h + gt2 * rmsnorm(y, g_post_ffn[l])
    return h[:, :S]
```

```python
import functools

import numpy as np
import jax
import jax.numpy as jnp
from jax import lax
from jax.experimental import pallas as pl
from jax.experimental.pallas import tpu as pltpu

D_MODEL = 1024
DEPTH = 4
HEAD_DIM = 64
A_HEADS = 8
MOBA_BLOCK = 256
MOBA_TOPK = 3
B_HEADS = 8
B_KV_HEADS = 2
B_WINDOW = 128
C_GROUPS = ((128, 1), (512, 4), (2048, 16))
C_HEADS_PER_GROUP = 4
C_HEADS = len(C_GROUPS) * C_HEADS_PER_GROUP
BAND = 128
D_FF = 2816
N_ALIBI_HEADS = A_HEADS + B_HEADS + C_HEADS
SEQ_MULTIPLE = 2048
RMS_EPS = 1e-6

LANES_V7X = 128
VMEM_LIMIT_BYTES_V7X = 56 * 1024 * 1024

Z_GATES = 0
Z_A = 3 * D_MODEL
Z_BQ = Z_A + 3 * A_HEADS * HEAD_DIM
Z_BK = Z_BQ + B_HEADS * HEAD_DIM
Z_BV = Z_BK + B_KV_HEADS * HEAD_DIM
Z_C = Z_BV + B_KV_HEADS * HEAD_DIM
Z_WIDTH = Z_C + 3 * C_HEADS * HEAD_DIM
B_HEAD_ORDER = (0, 4, 1, 5, 2, 6, 3, 7)

NEG_BIG = -1e30
QK_SCALE = HEAD_DIM ** -0.5


def _alibi_slopes():
    n = N_ALIBI_HEADS
    return [float(2.0 ** (-8.0 * (i + 1) / n)) for i in range(n)]


_SLOPES = _alibi_slopes()
SLOPES_B = _SLOPES[:B_HEADS]
SLOPES_C = _SLOPES[B_HEADS:B_HEADS + C_HEADS]
SLOPES_A = _SLOPES[B_HEADS + C_HEADS:]


def _dot(a, b):
    return jnp.dot(a, b, preferred_element_type=jnp.float32)


def _dot_nt(a, b):
    return lax.dot_general(a, b, (((1,), (1,)), ((), ())), preferred_element_type=jnp.float32)


def _params(semantics):
    return pltpu.CompilerParams(dimension_semantics=semantics, vmem_limit_bytes=VMEM_LIMIT_BYTES_V7X)


def _rmsnorm(x, g):
    return x * lax.rsqrt(jnp.mean(x * x, axis=-1, keepdims=True) + RMS_EPS) * g


def _ada_kernel(c_ref, w_ref, b_ref, o_ref):
    c = c_ref[...]
    sc = c * jax.nn.sigmoid(c)
    o_ref[...] = jnp.sum(w_ref[...] * sc, axis=0, keepdims=True) + b_ref[...]


def _ada_call(c, w_ada, b_ada):
    depth, d, n = w_ada.shape
    tn = 1536
    return pl.pallas_call(
        _ada_kernel,
        grid=(depth, n // tn),
        in_specs=[
            pl.BlockSpec((d, 1), lambda l, j: (0, 0)),
            pl.BlockSpec((None, d, tn), lambda l, j: (l, 0, j)),
            pl.BlockSpec((None, 1, tn), lambda l, j: (l, 0, j)),
        ],
        out_specs=pl.BlockSpec((None, 1, tn), lambda l, j: (l, 0, j)),
        out_shape=jax.ShapeDtypeStruct((depth, 1, n), jnp.float32),
        compiler_params=_params(("arbitrary", "arbitrary")),
        name="adaln_mod",
    )(c.reshape(d, 1), w_ada, b_ada.reshape(depth, 1, n))


IN_PROJ_CHUNK = 512


def _in_proj_kernel(h_ref, g_ref, mod_ref, w_ref, z_ref):
    x = h_ref[...]
    u = _rmsnorm(x, g_ref[...]) * (1.0 + mod_ref[1:2, :]) + mod_ref[0:1, :]
    u = u.astype(jnp.bfloat16)
    for c0 in range(0, Z_WIDTH, IN_PROJ_CHUNK):
        z_ref[:, c0:c0 + IN_PROJ_CHUNK] = _dot(u, w_ref[:, c0:c0 + IN_PROJ_CHUNK]).astype(z_ref.dtype)


def _in_proj_call(h, g, mod, w):
    s, d = h.shape
    tm = 256
    return pl.pallas_call(
        _in_proj_kernel,
        grid=(s // tm,),
        in_specs=[
            pl.BlockSpec((tm, d), lambda i: (i, 0)),
            pl.BlockSpec((1, d), lambda i: (0, 0)),
            pl.BlockSpec((6, d), lambda i: (0, 0)),
            pl.BlockSpec((d, Z_WIDTH), lambda i: (0, 0), pipeline_mode=pl.Buffered(1)),
        ],
        out_specs=pl.BlockSpec((tm, Z_WIDTH), lambda i: (i, 0)),
        out_shape=jax.ShapeDtypeStruct((s, Z_WIDTH), jnp.bfloat16),
        compiler_params=_params(("arbitrary",)),
        name="in_proj",
    )(h, g, mod, w)


def _moba_kernel(slopes_ref, q_ref, k_ref, v_ref, o_ref, vt_ref, km_ref, base_ref, sel_ref):
    p = pl.program_id(0)
    i = pl.program_id(1)
    nblk = vt_ref.shape[0]
    blk = MOBA_BLOCK
    half = HEAD_DIM

    @pl.when(i == 0)
    def _prepare_pair():
        def body(j, carry):
            rows = pl.ds(pl.multiple_of(j * blk, blk), blk)
            vt_ref[j] = v_ref[rows, :].astype(jnp.float32).T.astype(jnp.bfloat16)
            km = jnp.mean(k_ref[rows, :].astype(jnp.float32), axis=0, keepdims=True)
            km_ref[pl.ds(j, 1), :] = km
            return carry
        lax.fori_loop(0, nblk, body, 0)
        key_pos = lax.broadcasted_iota(jnp.int32, (blk, blk), 0)
        qry_pos = lax.broadcasted_iota(jnp.int32, (blk, blk), 1)
        dist = (qry_pos - key_pos).astype(jnp.float32)
        for h in range(2):
            base_ref[h] = -slopes_ref[2 * p + h] * dist

    q2 = q_ref[...]
    lane = lax.broadcasted_iota(jnp.int32, q2.shape, 1)
    qh = [jnp.where(lane < half, q2, jnp.zeros_like(q2)), jnp.where(lane >= half, q2, jnp.zeros_like(q2))]

    blk_id = lax.broadcasted_iota(jnp.int32, (nblk, blk), 0)
    km = km_ref[...]
    km_hi = km.astype(jnp.bfloat16)
    km_lo = (km - km_hi.astype(jnp.float32)).astype(jnp.bfloat16)
    for h in range(2):
        gate = _dot_nt(km_hi, qh[h]) + _dot_nt(km_lo, qh[h])
        gate = jnp.where(blk_id < i, gate, -jnp.inf)
        sel = jnp.full((nblk, blk), NEG_BIG, jnp.float32)
        for _ in range(MOBA_TOPK):
            mx = jnp.max(gate, axis=0, keepdims=True)
            cand = (gate == mx) & (mx > -jnp.inf)
            idx = jnp.min(jnp.where(cand, blk_id, nblk), axis=0, keepdims=True)
            chosen = blk_id == idx
            sel = jnp.where(chosen, 0.0, sel)
            gate = jnp.where(chosen, -jnp.inf, gate)
        sel_ref[h] = sel

    rows_i = pl.ds(pl.multiple_of(i * blk, blk), blk)
    k_own = k_ref[rows_i, :]
    vt_own = vt_ref[i]
    key_pos = lax.broadcasted_iota(jnp.int32, (blk, blk), 0)
    qry_pos = lax.broadcasted_iota(jnp.int32, (blk, blk), 1)
    causal = qry_pos >= key_pos
    init = []
    for h in range(2):
        s = jnp.where(causal, _dot_nt(k_own, qh[h]) + base_ref[h], NEG_BIG)
        m = jnp.max(s, axis=0, keepdims=True)
        e = jnp.exp(s - m)
        l = jnp.sum(e, axis=0, keepdims=True)
        acc = _dot(vt_own[h * half:(h + 1) * half, :], e.astype(jnp.bfloat16))
        init += [m, l, acc]

    def past_block(j, carry):
        rows = pl.ds(pl.multiple_of(j * blk, blk), blk)
        k_j = k_ref[rows, :]
        vt_j = vt_ref[j]
        gap = ((i - j) * blk).astype(jnp.float32)
        out = []
        for h in range(2):
            m, l, acc = carry[3 * h:3 * h + 3]
            row = sel_ref[h, pl.ds(j, 1), :] - slopes_ref[2 * p + h] * gap
            s = _dot_nt(k_j, qh[h]) + base_ref[h] + row
            m_new = jnp.maximum(m, jnp.max(s, axis=0, keepdims=True))
            alpha = jnp.exp(m - m_new)
            e = jnp.exp(s - m_new)
            l = alpha * l + jnp.sum(e, axis=0, keepdims=True)
            acc = alpha * acc + _dot(vt_j[h * half:(h + 1) * half, :], e.astype(jnp.bfloat16))
            out += [m_new, l, acc]
        return tuple(out)

    res = lax.fori_loop(0, i, past_block, tuple(init))
    o_t = jnp.concatenate([res[2] / res[1], res[5] / res[4]], axis=0)
    o_ref[...] = o_t.T.astype(o_ref.dtype)


def _moba_call(z):
    s = z.shape[0]
    nblk = s // MOBA_BLOCK
    pairs = A_HEADS // 2
    ln = LANES_V7X
    qb, kb, vb = Z_A // ln, Z_A // ln + pairs, Z_A // ln + 2 * pairs
    return pl.pallas_call(
        _moba_kernel,
        grid=(pairs, nblk),
        in_specs=[
            pl.BlockSpec(memory_space=pltpu.SMEM),
            pl.BlockSpec((MOBA_BLOCK, ln), lambda p, i: (i, qb + p)),
            pl.BlockSpec((s, ln), lambda p, i: (0, kb + p)),
            pl.BlockSpec((s, ln), lambda p, i: (0, vb + p)),
        ],
        out_specs=pl.BlockSpec((MOBA_BLOCK, ln), lambda p, i: (i, p)),
        out_shape=jax.ShapeDtypeStruct((s, A_HEADS * HEAD_DIM), jnp.bfloat16),
        scratch_shapes=[
            pltpu.VMEM((nblk, ln, MOBA_BLOCK), jnp.bfloat16),
            pltpu.VMEM((nblk, ln), jnp.float32),
            pltpu.VMEM((2, MOBA_BLOCK, MOBA_BLOCK), jnp.float32),
            pltpu.VMEM((2, nblk, MOBA_BLOCK), jnp.float32),
        ],
        compiler_params=_params(("arbitrary", "arbitrary")),
        name="moba_attn",
    )(jnp.asarray(SLOPES_A, jnp.float32), z, z, z)


def _band_scores_mask(first_block, max_steps):
    qi = lax.broadcasted_iota(jnp.int32, (BAND, 2 * BAND), 0)
    kj = lax.broadcasted_iota(jnp.int32, (BAND, 2 * BAND), 1)
    steps = qi + BAND - kj
    mask = (steps >= 0) & (steps <= max_steps) & (jnp.logical_not(first_block) | (kj >= BAND))
    return steps.astype(jnp.float32), mask


def _band_head(qm, kk, vv, stepsf, mask, slope_dil, sink):
    s = jnp.where(mask, _dot_nt(qm, kk) - slope_dil * stepsf, NEG_BIG)
    m = jnp.max(s, axis=1, keepdims=True)
    if sink is not None:
        m = jnp.maximum(m, sink)
    e = jnp.exp(s - m)
    denom = jnp.sum(e, axis=1, keepdims=True)
    if sink is not None:
        denom = denom + jnp.exp(sink - m)
    o = _dot(e.astype(jnp.bfloat16), vv) / denom
    return o, m + jnp.log(denom)


def _swa_kernel(sinks_ref, q_ref, kp_ref, ko_ref, vp_ref, vo_ref, o_ref):
    n = pl.program_id(0)
    stepsf, mask = _band_scores_mask(n == 0, B_WINDOW - 1)
    kk = jnp.concatenate([kp_ref[...], ko_ref[...]], axis=0)
    vv = jnp.concatenate([vp_ref[...], vo_ref[...]], axis=0)
    ln = LANES_V7X
    lane = lax.broadcasted_iota(jnp.int32, (BAND, ln), 1)
    low = lane < HEAD_DIM
    for b in range(B_HEADS // 2):
        q2 = q_ref[:, b * ln:(b + 1) * ln]
        outs = []
        for c in range(2):
            head = B_HEAD_ORDER[2 * b + c]
            qm = jnp.where(low if c == 0 else jnp.logical_not(low), q2, jnp.zeros_like(q2))
            o, _ = _band_head(qm, kk, vv, stepsf, mask, SLOPES_B[head], sinks_ref[head])
            outs.append(o)
        o_ref[:, b * ln:(b + 1) * ln] = jnp.where(low, outs[0], outs[1]).astype(o_ref.dtype)


def _swa_call(z, sinks):
    s = z.shape[0]
    nb = s // BAND
    ln = LANES_V7X
    qw = B_HEADS * HEAD_DIM
    prev = lambda n: jnp.maximum(n - 1, 0)
    return pl.pallas_call(
        _swa_kernel,
        grid=(nb,),
        in_specs=[
            pl.BlockSpec(memory_space=pltpu.SMEM),
            pl.BlockSpec((BAND, qw), lambda n: (n, Z_BQ // qw)),
            pl.BlockSpec((BAND, ln), lambda n: (prev(n), Z_BK // ln)),
            pl.BlockSpec((BAND, ln), lambda n: (n, Z_BK // ln)),
            pl.BlockSpec((BAND, ln), lambda n: (prev(n), Z_BV // ln)),
            pl.BlockSpec((BAND, ln), lambda n: (n, Z_BV // ln)),
        ],
        out_specs=pl.BlockSpec((BAND, qw), lambda n: (n, 0)),
        out_shape=jax.ShapeDtypeStruct((s, qw), jnp.bfloat16),
        compiler_params=_params(("arbitrary",)),
        name="swa_attn",
    )(sinks, z, z, z, z, z)


def _dilated_kernel(q_ref, kp_ref, ko_ref, vp_ref, vo_ref, o_ref, lse_ref, *, group, dilation, max_steps):
    n = pl.program_id(1)
    stepsf, mask = _band_scores_mask(n == 0, max_steps)
    q4 = q_ref[...]
    kk = jnp.concatenate([kp_ref[...], ko_ref[...]], axis=0)
    vv = jnp.concatenate([vp_ref[...], vo_ref[...]], axis=0)
    lane = lax.broadcasted_iota(jnp.int32, q4.shape, 1)
    o_all = jnp.zeros(q4.shape, jnp.float32)
    lse_all = jnp.zeros(q4.shape, jnp.float32)
    for h in range(C_HEADS_PER_GROUP):
        mine = (lane >= h * HEAD_DIM) & (lane < (h + 1) * HEAD_DIM)
        qm = jnp.where(mine, q4, jnp.zeros_like(q4))
        slope = SLOPES_C[group * C_HEADS_PER_GROUP + h]
        o, lse = _band_head(qm, kk, vv, stepsf, mask, slope * dilation, None)
        o_all = jnp.where(mine, o, o_all)
        lse_all = jnp.where(mine, lse, lse_all)
    o_ref[...] = o_all
    lse_ref[...] = lse_all


def _dilated_call(z, group):
    window, dil = C_GROUPS[group]
    s = z.shape[0]
    length = s // dil
    nb = length // BAND
    cw = C_HEADS_PER_GROUP * HEAD_DIM
    zv = z.reshape(length, dil * Z_WIDTH)
    per_res = Z_WIDTH // cw
    base = Z_C // cw + 3 * group
    prev = lambda n: jnp.maximum(n - 1, 0)
    kern = functools.partial(_dilated_kernel, group=group, dilation=dil, max_steps=window // dil)
    o, lse = pl.pallas_call(
        kern,
        grid=(dil, nb),
        in_specs=[
            pl.BlockSpec((BAND, cw), lambda r, n: (n, r * per_res + base)),
            pl.BlockSpec((BAND, cw), lambda r, n: (prev(n), r * per_res + base + 1)),
            pl.BlockSpec((BAND, cw), lambda r, n: (n, r * per_res + base + 1)),
            pl.BlockSpec((BAND, cw), lambda r, n: (prev(n), r * per_res + base + 2)),
            pl.BlockSpec((BAND, cw), lambda r, n: (n, r * per_res + base + 2)),
        ],
        out_specs=[
            pl.BlockSpec((BAND, cw), lambda r, n: (n, r)),
            pl.BlockSpec((BAND, cw), lambda r, n: (n, r)),
        ],
        out_shape=[
            jax.ShapeDtypeStruct((length, dil * cw), jnp.float32),
            jax.ShapeDtypeStruct((length, dil * cw), jnp.float32),
        ],
        compiler_params=_params(("arbitrary", "arbitrary")),
        name=f"dilated_attn_g{group}",
    )(zv, zv, zv, zv, zv)
    return o.reshape(s, cw), lse.reshape(s, cw)


def _merge_kernel(h_ref, zg_ref, oa_ref, ob_ref, o0_ref, o1_ref, o2_ref, l0_ref, l1_ref, l2_ref,
                  wa_ref, wb_ref, wc_ref, wo_ref, g_ref, mod_ref, out_ref):
    l0, l1, l2 = l0_ref[...], l1_ref[...], l2_ref[...]
    mx = jnp.maximum(jnp.maximum(l0, l1), l2)
    e0, e1, e2 = jnp.exp(l0 - mx), jnp.exp(l1 - mx), jnp.exp(l2 - mx)
    oc = (e0 * o0_ref[...] + e1 * o1_ref[...] + e2 * o2_ref[...]) / (e0 + e1 + e2)
    ya = _dot(oa_ref[...], wa_ref[...])
    yb = _dot(ob_ref[...], wb_ref[...])
    yc = _dot(oc.astype(jnp.bfloat16), wc_ref[...])
    d = D_MODEL
    gates = jax.nn.sigmoid(zg_ref[...].astype(jnp.float32))
    merged = gates[:, 0:d] * ya + gates[:, d:2 * d] * yb + gates[:, 2 * d:3 * d] * yc
    y = _dot(merged.astype(jnp.bfloat16), wo_ref[...])
    out_ref[...] = h_ref[...] + mod_ref[2:3, :] * _rmsnorm(y, g_ref[...])


def _merge_call(h, z, oa, ob, oc, lse, wa, wb, wc, wo, g, mod):
    s, d = h.shape
    tm = 256
    cw = C_HEADS_PER_GROUP * HEAD_DIM
    row = lambda w: pl.BlockSpec((tm, w), lambda i: (i, 0))
    full = lambda a: pl.BlockSpec(a.shape, lambda i: (0, 0), pipeline_mode=pl.Buffered(1))
    return pl.pallas_call(
        _merge_kernel,
        grid=(s // tm,),
        in_specs=[row(d), row(3 * d), row(oa.shape[1]), row(ob.shape[1])] + [row(cw)] * 6
        + [full(wa), full(wb), full(wc), full(wo), full(g), full(mod)],
        out_specs=row(d),
        out_shape=jax.ShapeDtypeStruct((s, d), jnp.float32),
        input_output_aliases={0: 0},
        compiler_params=_params(("arbitrary",)),
        name="merge_out_proj",
    )(h, z, oa, ob, *oc, *lse, wa, wb, wc, wo, g, mod)


FFN_CHUNK = 1408


def _ffn_kernel(h_ref, g_pre_ref, g_post_ref, mod_ref, wg_ref, wu_ref, wd_ref, out_ref):
    x = h_ref[...]
    u = _rmsnorm(x, g_pre_ref[...]) * (1.0 + mod_ref[4:5, :]) + mod_ref[3:4, :]
    u = u.astype(jnp.bfloat16)
    y = jnp.zeros(x.shape, jnp.float32)
    for c0 in range(0, D_FF, FFN_CHUNK):
        gate = _dot(u, wg_ref[:, c0:c0 + FFN_CHUNK])
        up = _dot(u, wu_ref[:, c0:c0 + FFN_CHUNK])
        act = (gate * jax.nn.sigmoid(gate) * up).astype(jnp.bfloat16)
        y = y + _dot(act, wd_ref[c0:c0 + FFN_CHUNK, :])
    out_ref[...] = x + mod_ref[5:6, :] * _rmsnorm(y, g_post_ref[...])


def _ffn_call(h, g_pre, g_post, mod, wg, wu, wd):
    s, d = h.shape
    tm = 256
    row = pl.BlockSpec((tm, d), lambda i: (i, 0))
    full = lambda a: pl.BlockSpec(a.shape, lambda i: (0, 0), pipeline_mode=pl.Buffered(1))
    return pl.pallas_call(
        _ffn_kernel,
        grid=(s // tm,),
        in_specs=[row, full(g_pre), full(g_post), full(mod), full(wg), full(wu), full(wd)],
        out_specs=row,
        out_shape=jax.ShapeDtypeStruct((s, d), jnp.float32),
        input_output_aliases={0: 0},
        compiler_params=_params(("arbitrary",)),
        name="swiglu_ffn",
    )(h, g_pre, g_post, mod, wg, wu, wd)


def _in_proj_columns():
    hd = HEAD_DIM
    a_w = 3 * A_HEADS * hd
    bq_w = B_HEADS * hd
    bkv_w = 2 * B_KV_HEADS * hd
    c_w = 3 * C_HEADS * hd
    o_a, o_bq = 0, a_w
    o_bkv = o_bq + bq_w
    o_c = o_bkv + bkv_w
    o_g = o_c + c_w
    cols = list(range(o_g, o_g + 3 * D_MODEL))
    scale = [1.0] * (3 * D_MODEL)
    cols += list(range(o_a, o_a + a_w))
    scale += [QK_SCALE] * (A_HEADS * hd) + [1.0] * (2 * A_HEADS * hd)
    for head in B_HEAD_ORDER:
        cols += list(range(o_bq + head * hd, o_bq + (head + 1) * hd))
    scale += [QK_SCALE] * bq_w
    cols += list(range(o_bkv, o_bkv + bkv_w))
    scale += [1.0] * bkv_w
    cols += list(range(o_c, o_c + c_w))
    for _ in range(len(C_GROUPS)):
        scale += [QK_SCALE] * (C_HEADS_PER_GROUP * hd) + [1.0] * (2 * C_HEADS_PER_GROUP * hd)
    assert len(cols) == Z_WIDTH and len(scale) == Z_WIDTH
    return np.asarray(cols, np.int32), np.asarray(scale, np.float32)


def _b_out_rows():
    rows = []
    for head in B_HEAD_ORDER:
        rows += list(range(head * HEAD_DIM, (head + 1) * HEAD_DIM))
    return np.asarray(rows, np.int32)


def kernel(x, c, w_ada, b_ada, g_pre_mix, g_post_mix, w_in, sinks, w_br_a, w_br_b, w_br_c,
           w_out, g_pre_ffn, g_post_ffn, w_gate, w_up, w_down):
    bn, s, d = x.shape
    assert bn == 1 and d == D_MODEL and s % SEQ_MULTIPLE == 0
    bf = jnp.bfloat16
    cols, scale = _in_proj_columns()
    w_in_p = (jnp.take(w_in, cols, axis=2) * scale).astype(bf)
    w_br_b_p = jnp.take(w_br_b, _b_out_rows(), axis=1).astype(bf)
    w_br_a_b, w_br_c_b, w_out_b = w_br_a.astype(bf), w_br_c.astype(bf), w_out.astype(bf)
    w_gate_b, w_up_b, w_down_b = w_gate.astype(bf), w_up.astype(bf), w_down.astype(bf)

    mod_all = _ada_call(c, w_ada, b_ada).reshape(DEPTH, 6, d)
    h = x.reshape(s, d)
    for l in range(DEPTH):
        mod = mod_all[l]
        z = _in_proj_call(h, g_pre_mix[l].reshape(1, d), mod, w_in_p[l])
        oa = _moba_call(z)
        ob = _swa_call(z, sinks[l])
        oc, lse = zip(*[_dilated_call(z, g) for g in range(len(C_GROUPS))])
        h = _merge_call(h, z, oa, ob, oc, lse, w_br_a_b[l], w_br_b_p[l], w_br_c_b[l], w_out_b[l],
                        g_post_mix[l].reshape(1, d), mod)
        h = _ffn_call(h, g_pre_ffn[l].reshape(1, d), g_post_ffn[l].reshape(1, d), mod,
                      w_gate_b[l], w_up_b[l], w_down_b[l])
    return h.reshape(bn, s, d)
```

```python
import functools

import numpy as np
import jax
import jax.numpy as jnp
from jax import lax
from jax.experimental import pallas as pl
from jax.experimental.pallas import tpu as pltpu

D_MODEL = 1024
DEPTH = 4
HEAD_DIM = 64
A_HEADS = 8
MOBA_BLOCK = 256
MOBA_TOPK = 3
B_HEADS = 8
B_KV_HEADS = 2
B_WINDOW = 128
C_GROUPS = ((128, 1), (512, 4), (2048, 16))
C_HEADS_PER_GROUP = 4
C_HEADS = len(C_GROUPS) * C_HEADS_PER_GROUP
BAND = 128
D_FF = 2816
N_ALIBI_HEADS = A_HEADS + B_HEADS + C_HEADS
SEQ_MULTIPLE = 2048
RMS_EPS = 1e-6

LANES_V7X = 128
BF16_SUBLANES_V7X = 16
VMEM_LIMIT_BYTES_V7X = 56 * 1024 * 1024

ROW_TILE = 256
C_WIDTH = 3 * C_HEADS_PER_GROUP * HEAD_DIM

Z_GATES = 0
Z_A = 3 * D_MODEL
Z_BQ = Z_A + 3 * A_HEADS * HEAD_DIM
Z_BK = Z_BQ + B_HEADS * HEAD_DIM
Z_BV = Z_BK + B_KV_HEADS * HEAD_DIM
Z_C0 = Z_BV + B_KV_HEADS * HEAD_DIM
Z_WIDTH = Z_C0 + C_WIDTH
IN_WIDTH = Z_WIDTH + (len(C_GROUPS) - 1) * C_WIDTH
B_HEAD_ORDER = (0, 4, 1, 5, 2, 6, 3, 7)

NEG_BIG = -1e30
QK_SCALE = HEAD_DIM ** -0.5


def _alibi_slopes():
    n = N_ALIBI_HEADS
    return [float(2.0 ** (-8.0 * (i + 1) / n)) for i in range(n)]


_SLOPES = _alibi_slopes()
SLOPES_B = _SLOPES[:B_HEADS]
SLOPES_C = _SLOPES[B_HEADS:B_HEADS + C_HEADS]
SLOPES_A = _SLOPES[B_HEADS + C_HEADS:]


def _dot(a, b):
    return jnp.dot(a, b, preferred_element_type=jnp.float32)


def _dot_nt(a, b):
    return lax.dot_general(a, b, (((1,), (1,)), ((), ())), preferred_element_type=jnp.float32)


def _params(semantics):
    return pltpu.CompilerParams(dimension_semantics=semantics, vmem_limit_bytes=VMEM_LIMIT_BYTES_V7X)


def _rmsnorm(x, g):
    return x * lax.rsqrt(jnp.mean(x * x, axis=-1, keepdims=True) + RMS_EPS) * g


def _ada_kernel(c_ref, w_ref, b_ref, o_ref):
    c = c_ref[...]
    sc = c * jax.nn.sigmoid(c)
    o_ref[...] = jnp.sum(w_ref[...] * sc, axis=0, keepdims=True) + b_ref[...]


def _ada_call(c, w_ada, b_ada):
    depth, d, n = w_ada.shape
    tn = 1536
    return pl.pallas_call(
        _ada_kernel,
        grid=(depth, n // tn),
        in_specs=[
            pl.BlockSpec((d, 1), lambda l, j: (0, 0)),
            pl.BlockSpec((None, d, tn), lambda l, j: (l, 0, j)),
            pl.BlockSpec((None, 1, tn), lambda l, j: (l, 0, j)),
        ],
        out_specs=pl.BlockSpec((None, 1, tn), lambda l, j: (l, 0, j)),
        out_shape=jax.ShapeDtypeStruct((depth, 1, n), jnp.float32),
        compiler_params=_params(("arbitrary", "arbitrary")),
        name="adaln_mod",
    )(c.reshape(d, 1), w_ada, b_ada.reshape(depth, 1, n))


IN_PROJ_CHUNK = 512


def _in_proj_kernel(h_ref, g_ref, mod_ref, w_ref, z_ref, zc1_ref, zc2_ref, tmp_ref):
    x = h_ref[...]
    u = _rmsnorm(x, g_ref[...]) * (1.0 + mod_ref[1:2, :]) + mod_ref[0:1, :]
    u = u.astype(jnp.bfloat16)
    for c0 in range(0, Z_WIDTH, IN_PROJ_CHUNK):
        z_ref[:, c0:c0 + IN_PROJ_CHUNK] = _dot(u, w_ref[:, c0:c0 + IN_PROJ_CHUNK]).astype(z_ref.dtype)
    for g, out_ref in ((1, zc1_ref), (2, zc2_ref)):
        c0 = Z_WIDTH + (g - 1) * C_WIDTH
        res = _dot(u, w_ref[:, c0:c0 + C_WIDTH])
        dil = C_GROUPS[g][1]
        rows = x.shape[0] // dil
        ln = LANES_V7X
        for c in range(C_WIDTH // ln):
            tmp_ref[c] = res[:, c * ln:(c + 1) * ln]
        for r in range(dil):
            for c in range(C_WIDTH // ln):
                out_ref[r, :, c * ln:(c + 1) * ln] = tmp_ref[c, pl.ds(r, rows, stride=dil), :].astype(out_ref.dtype)


def _in_proj_call(h, g, mod, w):
    s, d = h.shape
    tm = ROW_TILE
    d1, d2 = C_GROUPS[1][1], C_GROUPS[2][1]
    assert tm % (d2 * BF16_SUBLANES_V7X) == 0
    return pl.pallas_call(
        _in_proj_kernel,
        grid=(s // tm,),
        in_specs=[
            pl.BlockSpec((tm, d), lambda i: (i, 0)),
            pl.BlockSpec((1, d), lambda i: (0, 0)),
            pl.BlockSpec((6, d), lambda i: (0, 0)),
            pl.BlockSpec((d, IN_WIDTH), lambda i: (0, 0), pipeline_mode=pl.Buffered(1)),
        ],
        out_specs=[
            pl.BlockSpec((tm, Z_WIDTH), lambda i: (i, 0)),
            pl.BlockSpec((d1, tm // d1, C_WIDTH), lambda i: (0, i, 0)),
            pl.BlockSpec((d2, tm // d2, C_WIDTH), lambda i: (0, i, 0)),
        ],
        out_shape=[
            jax.ShapeDtypeStruct((s, Z_WIDTH), jnp.bfloat16),
            jax.ShapeDtypeStruct((d1, s // d1, C_WIDTH), jnp.bfloat16),
            jax.ShapeDtypeStruct((d2, s // d2, C_WIDTH), jnp.bfloat16),
        ],
        scratch_shapes=[pltpu.VMEM((C_WIDTH // LANES_V7X, tm, LANES_V7X), jnp.float32)],
        compiler_params=_params(("arbitrary",)),
        name="in_proj",
    )(h, g, mod, w)


SUBLANES_V7X = 8
MOBA_V_ROWS = HEAD_DIM + BF16_SUBLANES_V7X
MOBA_GROUPS = MOBA_BLOCK // SUBLANES_V7X


def _all_sublanes_max(x):
    for shift in (4, 2, 1):
        x = jnp.maximum(x, pltpu.roll(x, shift, axis=0))
    return x


def _moba_kernel(slopes_ref, q_ref, k_ref, v_ref, o_ref, vt_ref, km_ref, base_ref, sel_ref, s_ref):
    p = pl.program_id(0)
    i = pl.program_id(1)
    nblk = vt_ref.shape[0]
    blk = MOBA_BLOCK
    half = HEAD_DIM
    heads = (0, 1)
    sub = SUBLANES_V7X
    tile3 = (MOBA_GROUPS, sub, blk)
    key_pos = lax.broadcasted_iota(jnp.int32, tile3, 0) * sub + lax.broadcasted_iota(jnp.int32, tile3, 1)
    qry_pos = lax.broadcasted_iota(jnp.int32, tile3, 2)

    @pl.when(i == 0)
    def _prepare_pair():
        ones = jnp.ones((BF16_SUBLANES_V7X, blk), jnp.bfloat16)

        def body(j, carry):
            rows = pl.ds(pl.multiple_of(j * blk, blk), blk)
            vt = v_ref[rows, :].astype(jnp.float32).T.astype(jnp.bfloat16)
            for h in heads:
                vt_ref[j, h, 0:half, :] = vt[h * half:(h + 1) * half, :]
                vt_ref[j, h, half:, :] = ones
            km_ref[pl.ds(j, 1), :] = jnp.mean(k_ref[rows, :].astype(jnp.float32), axis=0, keepdims=True)
            return carry
        lax.fori_loop(0, nblk, body, 0)
        dist = (qry_pos - key_pos).astype(jnp.float32)
        for h in heads:
            base_ref[h] = -slopes_ref[2 * p + h] * dist

    q2 = q_ref[...]
    lane = lax.broadcasted_iota(jnp.int32, q2.shape, 1)
    qh = [jnp.where(lane < half, q2, jnp.zeros_like(q2)), jnp.where(lane >= half, q2, jnp.zeros_like(q2))]

    blk_id = lax.broadcasted_iota(jnp.int32, (nblk, blk), 0)
    km = km_ref[...]
    km_hi = km.astype(jnp.bfloat16)
    km_lo = (km - km_hi.astype(jnp.float32)).astype(jnp.bfloat16)
    gates = [_dot_nt(km_hi, qh[h]) + _dot_nt(km_lo, qh[h]) for h in heads]
    for h in heads:
        gate = jnp.where(blk_id < i, gates[h], -jnp.inf)
        sel = jnp.full((nblk, blk), NEG_BIG, jnp.float32)
        for _ in range(MOBA_TOPK):
            mx = jnp.max(gate, axis=0, keepdims=True)
            cand = (gate == mx) & (mx > -jnp.inf)
            idx = jnp.min(jnp.where(cand, blk_id, nblk), axis=0, keepdims=True)
            chosen = blk_id == idx
            sel = jnp.where(chosen, 0.0, sel)
            gate = jnp.where(chosen, -jnp.inf, gate)
        sel_ref[h] = sel

    def scores(k_blk):
        return [_dot_nt(k_blk, qh[h]).reshape(tile3) for h in heads]

    def finish(slot, vblock, state):
        es, alphas, ms = [], [], []
        for h in heads:
            m, _, tmax, row = state[h]
            m_new = jnp.maximum(m, tmax)
            alphas.append(jnp.exp(m - m_new))
            e = jnp.exp(s_ref[slot, h] - (m_new - row)[None])
            es.append(e.reshape(blk, blk).astype(jnp.bfloat16))
            ms.append(m_new)
        pvs = [_dot(vt_ref[vblock, h], es[h]).reshape(MOBA_V_ROWS // sub, sub, blk) for h in heads]
        return [(ms[h], alphas[h][None] * state[h][1] + pvs[h]) for h in heads]

    rows_i = pl.ds(pl.multiple_of(i * blk, blk), blk)
    raws = scores(k_ref[rows_i, :])
    state = []
    for h in heads:
        s = jnp.where(qry_pos >= key_pos, raws[h] + base_ref[h], NEG_BIG)
        s_ref[0, h] = s
        state.append((jnp.full((sub, blk), -jnp.inf, jnp.float32),
                      jnp.zeros((MOBA_V_ROWS // sub, sub, blk), jnp.float32),
                      _all_sublanes_max(jnp.max(s, axis=0)),
                      jnp.zeros((sub, blk), jnp.float32)))

    def trip(t, carry):
        state = [carry[4 * h:4 * h + 4] for h in heads]
        rows = pl.ds(pl.multiple_of(t * blk, blk), blk)
        raws = scores(k_ref[rows, :])
        slot = jnp.bitwise_and(t, 1)
        done = finish(slot, jnp.where(t == 0, i, t - 1), state)
        gap = ((i - t) * blk).astype(jnp.float32)
        out = []
        for h in heads:
            s = raws[h] + base_ref[h]
            s_ref[1 - slot, h] = s
            row = jnp.broadcast_to(sel_ref[h, pl.ds(t, 1), :] - slopes_ref[2 * p + h] * gap, (sub, blk))
            out += [done[h][0], done[h][1], _all_sublanes_max(jnp.max(s, axis=0)) + row, row]
        return tuple(out)

    carry = lax.fori_loop(0, i, trip, tuple(x for st in state for x in st))
    state = [carry[4 * h:4 * h + 4] for h in heads]
    last = finish(jnp.bitwise_and(i, 1), jnp.maximum(i - 1, 0), state)
    o_t = jnp.concatenate([(last[h][1][0:half // sub] / last[h][1][half // sub][None]).reshape(half, blk)
                           for h in heads], axis=0)
    o_ref[...] = o_t.T.astype(o_ref.dtype)


def _moba_call(z):
    s = z.shape[0]
    nblk = s // MOBA_BLOCK
    pairs = A_HEADS // 2
    ln = LANES_V7X
    qb, kb, vb = Z_A // ln, Z_A // ln + pairs, Z_A // ln + 2 * pairs
    return pl.pallas_call(
        _moba_kernel,
        grid=(pairs, nblk),
        in_specs=[
            pl.BlockSpec(memory_space=pltpu.SMEM),
            pl.BlockSpec((MOBA_BLOCK, ln), lambda p, i: (i, qb + p)),
            pl.BlockSpec((s, ln), lambda p, i: (0, kb + p)),
            pl.BlockSpec((s, ln), lambda p, i: (0, vb + p)),
        ],
        out_specs=pl.BlockSpec((MOBA_BLOCK, ln), lambda p, i: (i, p)),
        out_shape=jax.ShapeDtypeStruct((s, A_HEADS * HEAD_DIM), jnp.bfloat16),
        scratch_shapes=[
            pltpu.VMEM((nblk, 2, MOBA_V_ROWS, MOBA_BLOCK), jnp.bfloat16),
            pltpu.VMEM((nblk, ln), jnp.float32),
            pltpu.VMEM((2, MOBA_GROUPS, SUBLANES_V7X, MOBA_BLOCK), jnp.float32),
            pltpu.VMEM((2, nblk, MOBA_BLOCK), jnp.float32),
            pltpu.VMEM((2, 2, MOBA_GROUPS, SUBLANES_V7X, MOBA_BLOCK), jnp.float32),
        ],
        compiler_params=_params(("arbitrary", "arbitrary")),
        name="moba_attn",
    )(jnp.asarray(SLOPES_A, jnp.float32), z, z, z)


def _band_scores_mask(first_block, max_steps):
    qi = lax.broadcasted_iota(jnp.int32, (BAND, 2 * BAND), 0)
    kj = lax.broadcasted_iota(jnp.int32, (BAND, 2 * BAND), 1)
    steps = qi + BAND - kj
    mask = (steps >= 0) & (steps <= max_steps) & (jnp.logical_not(first_block) | (kj >= BAND))
    return steps.astype(jnp.float32), mask


def _band_heads(qms, kk, vv, stepsf, mask, slope_dils, sinks):
    raws = [_dot_nt(qm, kk) for qm in qms]
    es, denoms, ms = [], [], []
    for raw, sd, sink in zip(raws, slope_dils, sinks):
        s = jnp.where(mask, raw - sd * stepsf, NEG_BIG)
        m = jnp.max(s, axis=1, keepdims=True)
        if sink is not None:
            m = jnp.maximum(m, sink)
        e = jnp.exp(s - m)
        denom = jnp.sum(e, axis=1, keepdims=True)
        if sink is not None:
            denom = denom + jnp.exp(sink - m)
        es.append(e.astype(jnp.bfloat16))
        denoms.append(denom)
        ms.append(m)
    pvs = [_dot(e, vv) for e in es]
    return [pv / dn for pv, dn in zip(pvs, denoms)], [m + jnp.log(dn) for m, dn in zip(ms, denoms)]


def _swa_kernel(sinks_ref, q_ref, kp_ref, ko_ref, vp_ref, vo_ref, o_ref):
    n = pl.program_id(0)
    stepsf, mask = _band_scores_mask(n == 0, B_WINDOW - 1)
    kk = jnp.concatenate([kp_ref[...], ko_ref[...]], axis=0)
    vv = jnp.concatenate([vp_ref[...], vo_ref[...]], axis=0)
    ln = LANES_V7X
    lane = lax.broadcasted_iota(jnp.int32, (BAND, ln), 1)
    low = lane < HEAD_DIM
    qms, slopes, sinks = [], [], []
    for b in range(B_HEADS // 2):
        q2 = q_ref[:, b * ln:(b + 1) * ln]
        for c in range(2):
            head = B_HEAD_ORDER[2 * b + c]
            qms.append(jnp.where(low if c == 0 else jnp.logical_not(low), q2, jnp.zeros_like(q2)))
            slopes.append(SLOPES_B[head])
            sinks.append(sinks_ref[head])
    outs, _ = _band_heads(qms, kk, vv, stepsf, mask, slopes, sinks)
    for b in range(B_HEADS // 2):
        o_ref[:, b * ln:(b + 1) * ln] = jnp.where(low, outs[2 * b], outs[2 * b + 1]).astype(o_ref.dtype)


def _swa_call(z, sinks):
    s = z.shape[0]
    nb = s // BAND
    ln = LANES_V7X
    qw = B_HEADS * HEAD_DIM
    prev = lambda n: jnp.maximum(n - 1, 0)
    return pl.pallas_call(
        _swa_kernel,
        grid=(nb,),
        in_specs=[
            pl.BlockSpec(memory_space=pltpu.SMEM),
            pl.BlockSpec((BAND, qw), lambda n: (n, Z_BQ // qw)),
            pl.BlockSpec((BAND, ln), lambda n: (prev(n), Z_BK // ln)),
            pl.BlockSpec((BAND, ln), lambda n: (n, Z_BK // ln)),
            pl.BlockSpec((BAND, ln), lambda n: (prev(n), Z_BV // ln)),
            pl.BlockSpec((BAND, ln), lambda n: (n, Z_BV // ln)),
        ],
        out_specs=pl.BlockSpec((BAND, qw), lambda n: (n, 0)),
        out_shape=jax.ShapeDtypeStruct((s, qw), jnp.bfloat16),
        compiler_params=_params(("arbitrary",)),
        name="swa_attn",
    )(sinks, z, z, z, z, z)


def _dilated_kernel(q_ref, kp_ref, ko_ref, vp_ref, vo_ref, o_ref, lse_ref, *, group, dilation, max_steps):
    n = pl.program_id(1)
    stepsf, mask = _band_scores_mask(n == 0, max_steps)
    q4 = q_ref[...]
    kk = jnp.concatenate([kp_ref[...], ko_ref[...]], axis=0)
    vv = jnp.concatenate([vp_ref[...], vo_ref[...]], axis=0)
    lane = lax.broadcasted_iota(jnp.int32, q4.shape, 1)
    heads = range(C_HEADS_PER_GROUP)
    mine = [(lane >= h * HEAD_DIM) & (lane < (h + 1) * HEAD_DIM) for h in heads]
    qms = [jnp.where(mine[h], q4, jnp.zeros_like(q4)) for h in heads]
    slopes = [SLOPES_C[group * C_HEADS_PER_GROUP + h] * dilation for h in heads]
    outs, lses = _band_heads(qms, kk, vv, stepsf, mask, slopes, [None] * C_HEADS_PER_GROUP)
    o_all = jnp.zeros(q4.shape, jnp.float32)
    lse_all = jnp.zeros(q4.shape, jnp.float32)
    for h in heads:
        o_all = jnp.where(mine[h], outs[h], o_all)
        lse_all = jnp.where(mine[h], lses[h], lse_all)
    o_ref[...] = o_all
    lse_ref[...] = lse_all


def _dilated_call(zc, group, col0):
    window, dil = C_GROUPS[group]
    assert zc.shape[0] == dil
    length = zc.shape[1]
    nb = length // BAND
    cw = C_HEADS_PER_GROUP * HEAD_DIM
    base = col0 // cw
    prev = lambda n: jnp.maximum(n - 1, 0)
    kern = functools.partial(_dilated_kernel, group=group, dilation=dil, max_steps=window // dil)
    blk = lambda rows, col: pl.BlockSpec((None, BAND, cw), lambda r, n: (r, rows(n), base + col))
    same = lambda n: n
    out_blk = pl.BlockSpec((None, BAND, cw), lambda r, n: (r, n, 0))
    return pl.pallas_call(
        kern,
        grid=(dil, nb),
        in_specs=[blk(same, 0), blk(prev, 1), blk(same, 1), blk(prev, 2), blk(same, 2)],
        out_specs=[out_blk, out_blk],
        out_shape=[jax.ShapeDtypeStruct((dil, length, cw), jnp.float32)] * 2,
        compiler_params=_params(("arbitrary", "arbitrary")),
        name=f"dilated_attn_g{group}",
    )(zc, zc, zc, zc, zc)


def _merge_kernel(h_ref, zg_ref, oa_ref, ob_ref, o0_ref, l0_ref, o1_ref, l1_ref, o2_ref, l2_ref,
                  wa_ref, wb_ref, wc_ref, wo_ref, g_ref, mod_ref, out_ref, nat_ref):
    tm = h_ref.shape[0]

    def natural(ref, slot):
        dil = ref.shape[0]
        if dil == 1:
            return ref[0]
        ln = LANES_V7X
        chunks = ref.shape[2] // ln
        for r in range(dil):
            for c in range(chunks):
                nat_ref[slot * chunks + c, pl.ds(r, tm // dil, stride=dil), :] = ref[r, :, c * ln:(c + 1) * ln]
        return jnp.concatenate([nat_ref[slot * chunks + c] for c in range(chunks)], axis=1)

    l0, l1, l2 = natural(l0_ref, 0), natural(l1_ref, 0), natural(l2_ref, 1)
    o0, o1, o2 = natural(o0_ref, 0), natural(o1_ref, 2), natural(o2_ref, 3)
    mx = jnp.maximum(jnp.maximum(l0, l1), l2)
    e0, e1, e2 = jnp.exp(l0 - mx), jnp.exp(l1 - mx), jnp.exp(l2 - mx)
    oc = (e0 * o0 + e1 * o1 + e2 * o2) / (e0 + e1 + e2)
    ya = _dot(oa_ref[...], wa_ref[...])
    yb = _dot(ob_ref[...], wb_ref[...])
    yc = _dot(oc.astype(jnp.bfloat16), wc_ref[...])
    d = D_MODEL
    gates = jax.nn.sigmoid(zg_ref[...].astype(jnp.float32))
    merged = gates[:, 0:d] * ya + gates[:, d:2 * d] * yb + gates[:, 2 * d:3 * d] * yc
    y = _dot(merged.astype(jnp.bfloat16), wo_ref[...])
    out_ref[...] = h_ref[...] + mod_ref[2:3, :] * _rmsnorm(y, g_ref[...])


def _merge_call(h, z, oa, ob, oc_lse, wa, wb, wc, wo, g, mod):
    s, d = h.shape
    tm = ROW_TILE
    cw = C_HEADS_PER_GROUP * HEAD_DIM
    row = lambda w: pl.BlockSpec((tm, w), lambda i: (i, 0))
    res = lambda a: pl.BlockSpec((a.shape[0], tm // a.shape[0], cw), lambda i: (0, i, 0))
    full = lambda a: pl.BlockSpec(a.shape, lambda i: (0, 0), pipeline_mode=pl.Buffered(1))
    return pl.pallas_call(
        _merge_kernel,
        grid=(s // tm,),
        in_specs=[row(d), row(3 * d), row(oa.shape[1]), row(ob.shape[1])] + [res(a) for a in oc_lse]
        + [full(wa), full(wb), full(wc), full(wo), full(g), full(mod)],
        out_specs=row(d),
        out_shape=jax.ShapeDtypeStruct((s, d), jnp.float32),
        scratch_shapes=[pltpu.VMEM((4 * cw // LANES_V7X, tm, LANES_V7X), jnp.float32)],
        input_output_aliases={0: 0},
        compiler_params=_params(("arbitrary",)),
        name="merge_out_proj",
    )(h, z, oa, ob, *oc_lse, wa, wb, wc, wo, g, mod)


FFN_CHUNK = 1408


def _ffn_kernel(h_ref, g_pre_ref, g_post_ref, mod_ref, wg_ref, wu_ref, wd_ref, out_ref):
    x = h_ref[...]
    u = _rmsnorm(x, g_pre_ref[...]) * (1.0 + mod_ref[4:5, :]) + mod_ref[3:4, :]
    u = u.astype(jnp.bfloat16)
    y = jnp.zeros(x.shape, jnp.float32)
    for c0 in range(0, D_FF, FFN_CHUNK):
        gate = _dot(u, wg_ref[:, c0:c0 + FFN_CHUNK])
        up = _dot(u, wu_ref[:, c0:c0 + FFN_CHUNK])
        act = (gate * jax.nn.sigmoid(gate) * up).astype(jnp.bfloat16)
        y = y + _dot(act, wd_ref[c0:c0 + FFN_CHUNK, :])
    out_ref[...] = x + mod_ref[5:6, :] * _rmsnorm(y, g_post_ref[...])


def _ffn_call(h, g_pre, g_post, mod, wg, wu, wd):
    s, d = h.shape
    tm = ROW_TILE
    row = pl.BlockSpec((tm, d), lambda i: (i, 0))
    full = lambda a: pl.BlockSpec(a.shape, lambda i: (0, 0), pipeline_mode=pl.Buffered(1))
    return pl.pallas_call(
        _ffn_kernel,
        grid=(s // tm,),
        in_specs=[row, full(g_pre), full(g_post), full(mod), full(wg), full(wu), full(wd)],
        out_specs=row,
        out_shape=jax.ShapeDtypeStruct((s, d), jnp.float32),
        input_output_aliases={0: 0},
        compiler_params=_params(("arbitrary",)),
        name="swiglu_ffn",
    )(h, g_pre, g_post, mod, wg, wu, wd)


def _permute_in_proj(w_in):
    hd = HEAD_DIM
    o_bq = 3 * A_HEADS * hd
    o_bkv = o_bq + B_HEADS * hd
    o_c = o_bkv + 2 * B_KV_HEADS * hd
    o_g = o_c + 3 * C_HEADS * hd
    cw = C_HEADS_PER_GROUP * hd
    sl = lambda a, b: w_in[:, :, a:b]
    parts = [sl(o_g, o_g + 3 * D_MODEL), sl(0, A_HEADS * hd) * QK_SCALE, sl(A_HEADS * hd, o_bq)]
    parts += [sl(o_bq + hh * hd, o_bq + (hh + 1) * hd) * QK_SCALE for hh in B_HEAD_ORDER]
    parts += [sl(o_bkv, o_c)]
    for g in range(len(C_GROUPS)):
        parts += [sl(o_c + g * 3 * cw, o_c + g * 3 * cw + cw) * QK_SCALE,
                  sl(o_c + g * 3 * cw + cw, o_c + (g + 1) * 3 * cw)]
    out = jnp.concatenate(parts, axis=2).astype(jnp.bfloat16)
    assert out.shape[2] == IN_WIDTH
    return out


def kernel(x, c, w_ada, b_ada, g_pre_mix, g_post_mix, w_in, sinks, w_br_a, w_br_b, w_br_c,
           w_out, g_pre_ffn, g_post_ffn, w_gate, w_up, w_down):
    bn, s, d = x.shape
    assert bn == 1 and d == D_MODEL and s % SEQ_MULTIPLE == 0
    bf = jnp.bfloat16
    hd = HEAD_DIM
    w_in_p = _permute_in_proj(w_in)
    w_br_b_p = jnp.concatenate([w_br_b[:, hh * hd:(hh + 1) * hd] for hh in B_HEAD_ORDER], axis=1).astype(bf)
    w_br_a_b, w_br_c_b, w_out_b = w_br_a.astype(bf), w_br_c.astype(bf), w_out.astype(bf)
    w_gate_b, w_up_b, w_down_b = w_gate.astype(bf), w_up.astype(bf), w_down.astype(bf)

    mod_all = _ada_call(c, w_ada, b_ada).reshape(DEPTH, 6, d)
    h = x.reshape(s, d)
    for l in range(DEPTH):
        mod = mod_all[l]
        z, zc1, zc2 = _in_proj_call(h, g_pre_mix[l].reshape(1, d), mod, w_in_p[l])
        oa = _moba_call(z)
        ob = _swa_call(z, sinks[l])
        oc_lse = (_dilated_call(z.reshape(1, s, Z_WIDTH), 0, Z_C0)
                  + _dilated_call(zc1, 1, 0) + _dilated_call(zc2, 2, 0))
        h = _merge_call(h, z, oa, ob, oc_lse, w_br_a_b[l], w_br_b_p[l], w_br_c_b[l], w_out_b[l],
                        g_post_mix[l].reshape(1, d), mod)
        h = _ffn_call(h, g_pre_ffn[l].reshape(1, d), g_post_ffn[l].reshape(1, d), mod,
                      w_gate_b[l], w_up_b[l], w_down_b[l])
    return h.reshape(bn, s, d)
```

```python
import functools

import numpy as np
import jax
import jax.numpy as jnp
from jax import lax
from jax.experimental import pallas as pl
from jax.experimental.pallas import tpu as pltpu

D_MODEL = 1024
DEPTH = 4
HEAD_DIM = 64
A_HEADS = 8
MOBA_BLOCK = 256
MOBA_TOPK = 3
B_HEADS = 8
B_KV_HEADS = 2
B_WINDOW = 128
C_GROUPS = ((128, 1), (512, 4), (2048, 16))
C_HEADS_PER_GROUP = 4
C_HEADS = len(C_GROUPS) * C_HEADS_PER_GROUP
BAND = 128
D_FF = 2816
N_ALIBI_HEADS = A_HEADS + B_HEADS + C_HEADS
SEQ_MULTIPLE = 2048
RMS_EPS = 1e-6

LANES_V7X = 128
BF16_SUBLANES_V7X = 16
VMEM_LIMIT_BYTES_V7X = 56 * 1024 * 1024

ROW_TILE = 256
C_WIDTH = 3 * C_HEADS_PER_GROUP * HEAD_DIM

Z_GATES = 0
Z_A = 3 * D_MODEL
Z_BQ = Z_A + 3 * A_HEADS * HEAD_DIM
Z_BK = Z_BQ + B_HEADS * HEAD_DIM
Z_BV = Z_BK + B_KV_HEADS * HEAD_DIM
Z_C0 = Z_BV + B_KV_HEADS * HEAD_DIM
Z_WIDTH = Z_C0 + C_WIDTH
IN_WIDTH = Z_WIDTH + (len(C_GROUPS) - 1) * C_WIDTH
B_HEAD_ORDER = (0, 4, 1, 5, 2, 6, 3, 7)

NEG_BIG = -1e30
QK_SCALE = HEAD_DIM ** -0.5
LOG2E = 1.4426950408889634


def _alibi_slopes():
    n = N_ALIBI_HEADS
    return [float(2.0 ** (-8.0 * (i + 1) / n)) for i in range(n)]


_SLOPES = _alibi_slopes()
SLOPES_B = _SLOPES[:B_HEADS]
SLOPES_C = _SLOPES[B_HEADS:B_HEADS + C_HEADS]
SLOPES_A = _SLOPES[B_HEADS + C_HEADS:]


def _dot(a, b):
    return jnp.dot(a, b, preferred_element_type=jnp.float32)


def _dot_nt(a, b):
    return lax.dot_general(a, b, (((1,), (1,)), ((), ())), preferred_element_type=jnp.float32)


def _params(semantics):
    return pltpu.CompilerParams(dimension_semantics=semantics, vmem_limit_bytes=VMEM_LIMIT_BYTES_V7X)


def _rmsnorm(x, g):
    return x * lax.rsqrt(jnp.mean(x * x, axis=-1, keepdims=True) + RMS_EPS) * g


def _ada_kernel(c_ref, w_ref, b_ref, o_ref):
    c = c_ref[...]
    sc = c * jax.nn.sigmoid(c)
    o_ref[...] = jnp.sum(w_ref[...] * sc, axis=0, keepdims=True) + b_ref[...]


def _ada_call(c, w_ada, b_ada):
    depth, d, n = w_ada.shape
    tn = 1536
    return pl.pallas_call(
        _ada_kernel,
        grid=(depth, n // tn),
        in_specs=[
            pl.BlockSpec((d, 1), lambda l, j: (0, 0)),
            pl.BlockSpec((None, d, tn), lambda l, j: (l, 0, j)),
            pl.BlockSpec((None, 1, tn), lambda l, j: (l, 0, j)),
        ],
        out_specs=pl.BlockSpec((None, 1, tn), lambda l, j: (l, 0, j)),
        out_shape=jax.ShapeDtypeStruct((depth, 1, n), jnp.float32),
        compiler_params=_params(("arbitrary", "arbitrary")),
        name="adaln_mod",
    )(c.reshape(d, 1), w_ada, b_ada.reshape(depth, 1, n))


IN_PROJ_CHUNK = 512


def _in_proj_kernel(h_ref, g_ref, mod_ref, w_ref, z_ref, zc1_ref, zc2_ref, tmp_ref):
    x = h_ref[...]
    u = _rmsnorm(x, g_ref[...]) * (1.0 + mod_ref[1:2, :]) + mod_ref[0:1, :]
    u = u.astype(jnp.bfloat16)
    for c0 in range(0, Z_WIDTH, IN_PROJ_CHUNK):
        z_ref[:, c0:c0 + IN_PROJ_CHUNK] = _dot(u, w_ref[:, c0:c0 + IN_PROJ_CHUNK]).astype(z_ref.dtype)
    for g, out_ref in ((1, zc1_ref), (2, zc2_ref)):
        c0 = Z_WIDTH + (g - 1) * C_WIDTH
        res = _dot(u, w_ref[:, c0:c0 + C_WIDTH])
        dil = C_GROUPS[g][1]
        rows = x.shape[0] // dil
        ln = LANES_V7X
        for c in range(C_WIDTH // ln):
            tmp_ref[c] = res[:, c * ln:(c + 1) * ln]
        for r in range(dil):
            for c in range(C_WIDTH // ln):
                out_ref[r, :, c * ln:(c + 1) * ln] = tmp_ref[c, pl.ds(r, rows, stride=dil), :].astype(out_ref.dtype)


def _in_proj_call(h, g, mod, w, layer):
    s, d = h.shape
    tm = ROW_TILE
    d1, d2 = C_GROUPS[1][1], C_GROUPS[2][1]
    assert tm % (d2 * BF16_SUBLANES_V7X) == 0
    return pl.pallas_call(
        _in_proj_kernel,
        grid=(s // tm,),
        in_specs=[
            pl.BlockSpec((tm, d), lambda i: (i, 0)),
            pl.BlockSpec((1, d), lambda i: (0, 0)),
            pl.BlockSpec((6, d), lambda i: (0, 0)),
            pl.BlockSpec((None, d, IN_WIDTH), lambda i: (layer, 0, 0), pipeline_mode=pl.Buffered(1)),
        ],
        out_specs=[
            pl.BlockSpec((tm, Z_WIDTH), lambda i: (i, 0)),
            pl.BlockSpec((d1, tm // d1, C_WIDTH), lambda i: (0, i, 0)),
            pl.BlockSpec((d2, tm // d2, C_WIDTH), lambda i: (0, i, 0)),
        ],
        out_shape=[
            jax.ShapeDtypeStruct((s, Z_WIDTH), jnp.bfloat16),
            jax.ShapeDtypeStruct((d1, s // d1, C_WIDTH), jnp.bfloat16),
            jax.ShapeDtypeStruct((d2, s // d2, C_WIDTH), jnp.bfloat16),
        ],
        scratch_shapes=[pltpu.VMEM((C_WIDTH // LANES_V7X, tm, LANES_V7X), jnp.float32)],
        compiler_params=_params(("arbitrary",)),
        name="in_proj",
    )(h, g, mod, w)


SUBLANES_V7X = 8
MOBA_V_ROWS = HEAD_DIM + BF16_SUBLANES_V7X
MOBA_GROUPS = MOBA_BLOCK // SUBLANES_V7X
MOBA_UNROLL_LOG2 = 1
MOBA_UNROLL = 1 << MOBA_UNROLL_LOG2


def _all_sublanes_max(x):
    for shift in (4, 2, 1):
        x = jnp.maximum(x, pltpu.roll(x, shift, axis=0))
    return x


def _moba_kernel(slopes_ref, q_ref, k_ref, v_ref, o_ref, vt_ref, km_ref, base_ref, sel_ref, s_ref, e_ref):
    p = pl.program_id(0)
    i = pl.program_id(1)
    nblk = vt_ref.shape[0]
    blk = MOBA_BLOCK
    half = HEAD_DIM
    heads = (0, 1)
    sub = SUBLANES_V7X
    tile3 = (MOBA_GROUPS, sub, blk)
    key_pos = lax.broadcasted_iota(jnp.int32, tile3, 0) * sub + lax.broadcasted_iota(jnp.int32, tile3, 1)
    qry_pos = lax.broadcasted_iota(jnp.int32, tile3, 2)

    @pl.when(i == 0)
    def _prepare_pair():
        ones = jnp.ones((BF16_SUBLANES_V7X, blk), jnp.bfloat16)

        def body(j, carry):
            rows = pl.ds(pl.multiple_of(j * blk, blk), blk)
            vt = v_ref[rows, :].astype(jnp.float32).T.astype(jnp.bfloat16)
            for h in heads:
                vt_ref[j, h, 0:half, :] = vt[h * half:(h + 1) * half, :]
                vt_ref[j, h, half:, :] = ones
            km_ref[pl.ds(j, 1), :] = jnp.mean(k_ref[rows, :].astype(jnp.float32), axis=0, keepdims=True)
            return carry
        lax.fori_loop(0, nblk, body, 0)
        dist = (qry_pos - key_pos).astype(jnp.float32)
        for h in heads:
            base_ref[h] = -slopes_ref[2 * p + h] * dist

    q2 = q_ref[...]
    lane = lax.broadcasted_iota(jnp.int32, q2.shape, 1)
    qh = [jnp.where(lane < half, q2, jnp.zeros_like(q2)), jnp.where(lane >= half, q2, jnp.zeros_like(q2))]

    blk_id = lax.broadcasted_iota(jnp.int32, (nblk, blk), 0)
    km = km_ref[...]
    km_hi = km.astype(jnp.bfloat16)
    km_lo = (km - km_hi.astype(jnp.float32)).astype(jnp.bfloat16)
    gates = [_dot_nt(km_hi, qh[h]) + _dot_nt(km_lo, qh[h]) for h in heads]
    for h in heads:
        gate = jnp.where(blk_id < i, gates[h], -jnp.inf)
        sel = jnp.full((nblk, blk), NEG_BIG, jnp.float32)
        for _ in range(MOBA_TOPK):
            mx = jnp.max(gate, axis=0, keepdims=True)
            cand = (gate == mx) & (mx > -jnp.inf)
            idx = jnp.min(jnp.where(cand, blk_id, nblk), axis=0, keepdims=True)
            chosen = blk_id == idx
            sel = jnp.where(chosen, 0.0, sel)
            gate = jnp.where(chosen, -jnp.inf, gate)
        sel_ref[h] = sel

    unroll = MOBA_UNROLL
    acc_groups = MOBA_V_ROWS // sub

    def past_block(n):
        return jnp.clip(n - 1, 0, nblk - 1)

    def issue_scores(blocks):
        out = []
        for b in blocks:
            k_blk = k_ref[pl.ds(pl.multiple_of(b * blk, blk), blk), :]
            out.append([_dot_nt(k_blk, qh[h]).reshape(tile3) for h in heads])
        return out

    def stage_scores(raws, slot, items, own_first):
        out = []
        for u, n in enumerate(items):
            per_head = []
            for h in heads:
                s = raws[u][h] + base_ref[h]
                if own_first and u == 0:
                    s = jnp.where(qry_pos >= key_pos, s, NEG_BIG)
                    row = jnp.zeros((sub, blk), jnp.float32)
                else:
                    b = past_block(n)
                    gap = ((i - b) * blk).astype(jnp.float32)
                    row = sel_ref[h, pl.ds(b, 1), :] - slopes_ref[2 * p + h] * gap
                    row = jnp.broadcast_to(jnp.where(n <= i, row, NEG_BIG), (sub, blk))
                s_ref[slot, u, h] = s
                per_head += [_all_sublanes_max(jnp.max(s, axis=0)) + row, row]
            out.append(per_head)
        return out

    def exponentiate(group_slot, ms, staged):
        new_ms, alphas = [], []
        for h in heads:
            m_new = ms[h]
            for u in range(unroll):
                m_new = jnp.maximum(m_new, staged[u][2 * h])
            alphas.append(jnp.exp2(ms[h] - m_new))
            new_ms.append(m_new)
            for u in range(unroll):
                e = jnp.exp2(s_ref[group_slot, u, h] - (m_new - staged[u][2 * h + 1])[None])
                e_ref[group_slot, u, h] = e.reshape(blk, blk).astype(jnp.bfloat16)
        return new_ms, alphas

    def accumulate(group, slot, accs, alphas):
        out = []
        for h in heads:
            acc = alphas[h][None] * accs[h]
            for u in range(unroll):
                n = group * unroll + u
                vblock = jnp.where(n == 0, i, past_block(n))
                acc = acc + _dot(vt_ref[vblock, h], e_ref[slot, u, h]).reshape(acc_groups, sub, blk)
            out.append(acc)
        return out

    def score_group(group, slot, own_first):
        items = [group * unroll + u for u in range(unroll)]
        blocks = [i if (own_first and u == 0) else past_block(n) for u, n in enumerate(items)]
        raws = issue_scores(blocks)
        return lambda: stage_scores(raws, slot, items, own_first)

    def flatten(staged):
        return [x for per_head in staged for x in per_head]

    def unflatten(flat):
        return [flat[4 * u:4 * u + 4] for u in range(unroll)]

    def tick(t, slot, state):
        ms, accs, alphas, staged = state
        accs = accumulate(t - 2, slot, accs, alphas)
        finish_scores = score_group(t, slot, False)
        ms, alphas = exponentiate(1 - slot, ms, staged)
        return ms, accs, alphas, finish_scores()

    pairs = jnp.maximum(lax.shift_right_logical(i + 2 * unroll, MOBA_UNROLL_LOG2 + 1), 1)
    ms = [jnp.full((sub, blk), -jnp.inf, jnp.float32) for _ in heads]
    accs = [jnp.zeros((acc_groups, sub, blk), jnp.float32) for _ in heads]
    staged = score_group(0, 0, True)()
    finish_scores = score_group(1, 1, False)
    ms, alphas = exponentiate(0, ms, staged)
    staged = finish_scores()

    def two_ticks(k, carry):
        state = (list(carry[0:2]), list(carry[2:4]), list(carry[4:6]), unflatten(carry[6:]))
        state = tick(2 * k, 0, state)
        ms, accs, alphas, staged = tick(2 * k + 1, 1, state)
        return tuple(ms + accs + alphas + flatten(staged))

    carry = lax.fori_loop(1, pairs, two_ticks, tuple(ms + accs + alphas + flatten(staged)))
    ms, accs, alphas, staged = list(carry[0:2]), list(carry[2:4]), list(carry[4:6]), unflatten(carry[6:])
    accs = accumulate(2 * pairs - 2, 0, accs, alphas)
    ms, alphas = exponentiate(1, ms, staged)
    accs = accumulate(2 * pairs - 1, 1, accs, alphas)
    o_t = jnp.concatenate([(accs[h][0:half // sub] / accs[h][half // sub][None]).reshape(half, blk)
                           for h in heads], axis=0)
    o_ref[...] = o_t.T.astype(o_ref.dtype)


def _moba_call(z):
    s = z.shape[0]
    nblk = s // MOBA_BLOCK
    pairs = A_HEADS // 2
    ln = LANES_V7X
    qb, kb, vb = Z_A // ln, Z_A // ln + pairs, Z_A // ln + 2 * pairs
    return pl.pallas_call(
        _moba_kernel,
        grid=(pairs, nblk),
        in_specs=[
            pl.BlockSpec(memory_space=pltpu.SMEM),
            pl.BlockSpec((MOBA_BLOCK, ln), lambda p, i: (i, qb + p)),
            pl.BlockSpec((s, ln), lambda p, i: (0, kb + p)),
            pl.BlockSpec((s, ln), lambda p, i: (0, vb + p)),
        ],
        out_specs=pl.BlockSpec((MOBA_BLOCK, ln), lambda p, i: (i, p)),
        out_shape=jax.ShapeDtypeStruct((s, A_HEADS * HEAD_DIM), jnp.bfloat16),
        scratch_shapes=[
            pltpu.VMEM((nblk, 2, MOBA_V_ROWS, MOBA_BLOCK), jnp.bfloat16),
            pltpu.VMEM((nblk, ln), jnp.float32),
            pltpu.VMEM((2, MOBA_GROUPS, SUBLANES_V7X, MOBA_BLOCK), jnp.float32),
            pltpu.VMEM((2, nblk, MOBA_BLOCK), jnp.float32),
            pltpu.VMEM((2, MOBA_UNROLL, 2, MOBA_GROUPS, SUBLANES_V7X, MOBA_BLOCK), jnp.float32),
            pltpu.VMEM((2, MOBA_UNROLL, 2, MOBA_BLOCK, MOBA_BLOCK), jnp.bfloat16),
        ],
        compiler_params=_params(("arbitrary", "arbitrary")),
        name="moba_attn",
    )(jnp.asarray([sl * LOG2E for sl in SLOPES_A], jnp.float32), z, z, z)


def _blocks_per_step(nb):
    return max(c for c in (4, 2, 1) if nb % c == 0)


def _band_masks(first_step, max_steps):
    qi = lax.broadcasted_iota(jnp.int32, (BAND, 2 * BAND), 0)
    kj = lax.broadcasted_iota(jnp.int32, (BAND, 2 * BAND), 1)
    steps = qi + BAND - kj
    inner = (steps >= 0) & (steps <= max_steps)
    first = inner & (jnp.logical_not(first_step) | (kj >= BAND))
    return steps.astype(jnp.float32), first, inner


def _band_units(units, stepsf, lookahead):
    raws = {j: _dot_nt(units[j][0], units[j][1]) for j in range(min(lookahead, len(units)))}
    outs, lses = [], []
    for j, (_, _, vv, mask, sd, sink) in enumerate(units):
        ahead = j + lookahead
        if ahead < len(units):
            raws[ahead] = _dot_nt(units[ahead][0], units[ahead][1])
        s = jnp.where(mask, raws.pop(j) - sd * stepsf, NEG_BIG)
        m = jnp.max(s, axis=1, keepdims=True)
        if sink is not None:
            m = jnp.maximum(m, sink)
        e = jnp.exp(s - m)
        denom = jnp.sum(e, axis=1, keepdims=True)
        if sink is not None:
            denom = denom + jnp.exp(sink - m)
        outs.append(_dot(e.astype(jnp.bfloat16), vv) / denom)
        lses.append(m + jnp.log(denom))
    return outs, lses


def _swa_kernel(sinks_ref, q_ref, kp_ref, ko_ref, vp_ref, vo_ref, o_ref):
    n = pl.program_id(0)
    nbq = q_ref.shape[0] // BAND
    stepsf, mask_first, mask_inner = _band_masks(n == 0, B_WINDOW - 1)
    k_all = jnp.concatenate([kp_ref[...], ko_ref[...]], axis=0)
    v_all = jnp.concatenate([vp_ref[...], vo_ref[...]], axis=0)
    ln = LANES_V7X
    lane = lax.broadcasted_iota(jnp.int32, (BAND, ln), 1)
    low = lane < HEAD_DIM
    units = []
    for qb in range(nbq):
        kk = k_all[qb * BAND:(qb + 2) * BAND]
        vv = v_all[qb * BAND:(qb + 2) * BAND]
        mask = mask_first if qb == 0 else mask_inner
        for b in range(B_HEADS // 2):
            q2 = q_ref[qb * BAND:(qb + 1) * BAND, b * ln:(b + 1) * ln]
            for c in range(2):
                head = B_HEAD_ORDER[2 * b + c]
                qm = jnp.where(low if c == 0 else jnp.logical_not(low), q2, jnp.zeros_like(q2))
                units.append((qm, kk, vv, mask, SLOPES_B[head], sinks_ref[head]))
    outs, _ = _band_units(units, stepsf, B_HEADS)
    for qb in range(nbq):
        for b in range(B_HEADS // 2):
            j = qb * B_HEADS + 2 * b
            o_ref[qb * BAND:(qb + 1) * BAND, b * ln:(b + 1) * ln] = (
                jnp.where(low, outs[j], outs[j + 1]).astype(o_ref.dtype))


def _swa_call(z, sinks):
    s = z.shape[0]
    nbq = _blocks_per_step(s // BAND)
    rows = nbq * BAND
    ln = LANES_V7X
    qw = B_HEADS * HEAD_DIM
    prev = lambda n: jnp.maximum(n * nbq - 1, 0)
    return pl.pallas_call(
        _swa_kernel,
        grid=(s // rows,),
        in_specs=[
            pl.BlockSpec(memory_space=pltpu.SMEM),
            pl.BlockSpec((rows, qw), lambda n: (n, Z_BQ // qw)),
            pl.BlockSpec((BAND, ln), lambda n: (prev(n), Z_BK // ln)),
            pl.BlockSpec((rows, ln), lambda n: (n, Z_BK // ln)),
            pl.BlockSpec((BAND, ln), lambda n: (prev(n), Z_BV // ln)),
            pl.BlockSpec((rows, ln), lambda n: (n, Z_BV // ln)),
        ],
        out_specs=pl.BlockSpec((rows, qw), lambda n: (n, 0)),
        out_shape=jax.ShapeDtypeStruct((s, qw), jnp.bfloat16),
        compiler_params=_params(("arbitrary",)),
        name="swa_attn",
    )(sinks, z, z, z, z, z)


def _dilated_kernel(q_ref, kp_ref, ko_ref, vp_ref, vo_ref, o_ref, lse_ref, *, group, dilation, max_steps):
    n = pl.program_id(1)
    nbq = q_ref.shape[0] // BAND
    stepsf, mask_first, mask_inner = _band_masks(n == 0, max_steps)
    k_all = jnp.concatenate([kp_ref[...], ko_ref[...]], axis=0)
    v_all = jnp.concatenate([vp_ref[...], vo_ref[...]], axis=0)
    cw = q_ref.shape[1]
    lane = lax.broadcasted_iota(jnp.int32, (BAND, cw), 1)
    heads = range(C_HEADS_PER_GROUP)
    mine = [(lane >= h * HEAD_DIM) & (lane < (h + 1) * HEAD_DIM) for h in heads]
    units = []
    for qb in range(nbq):
        q4 = q_ref[qb * BAND:(qb + 1) * BAND, :]
        kk = k_all[qb * BAND:(qb + 2) * BAND]
        vv = v_all[qb * BAND:(qb + 2) * BAND]
        mask = mask_first if qb == 0 else mask_inner
        for h in heads:
            slope = SLOPES_C[group * C_HEADS_PER_GROUP + h] * dilation
            units.append((jnp.where(mine[h], q4, jnp.zeros_like(q4)), kk, vv, mask, slope, None))
    outs, lses = _band_units(units, stepsf, C_HEADS_PER_GROUP)
    for qb in range(nbq):
        o_all = jnp.zeros((BAND, cw), jnp.float32)
        lse_all = jnp.zeros((BAND, cw), jnp.float32)
        for h in heads:
            j = qb * C_HEADS_PER_GROUP + h
            o_all = jnp.where(mine[h], outs[j], o_all)
            lse_all = jnp.where(mine[h], lses[j], lse_all)
        o_ref[qb * BAND:(qb + 1) * BAND, :] = o_all
        lse_ref[qb * BAND:(qb + 1) * BAND, :] = lse_all


def _dilated_call(zc, group, col0):
    window, dil = C_GROUPS[group]
    assert zc.shape[0] == dil
    length = zc.shape[1]
    nbq = _blocks_per_step(length // BAND)
    rows = nbq * BAND
    cw = C_HEADS_PER_GROUP * HEAD_DIM
    base = col0 // cw
    kern = functools.partial(_dilated_kernel, group=group, dilation=dil, max_steps=window // dil)
    own = lambda col: pl.BlockSpec((None, rows, cw), lambda r, n: (r, n, base + col))
    prev = lambda col: pl.BlockSpec((None, BAND, cw), lambda r, n: (r, jnp.maximum(n * nbq - 1, 0), base + col))
    out_blk = pl.BlockSpec((None, rows, cw), lambda r, n: (r, n, 0))
    return pl.pallas_call(
        kern,
        grid=(dil, length // rows),
        in_specs=[own(0), prev(1), own(1), prev(2), own(2)],
        out_specs=[out_blk, out_blk],
        out_shape=[jax.ShapeDtypeStruct((dil, length, cw), jnp.float32)] * 2,
        compiler_params=_params(("arbitrary", "arbitrary")),
        name=f"dilated_attn_g{group}",
    )(zc, zc, zc, zc, zc)


def _merge_kernel(h_ref, zg_ref, oa_ref, ob_ref, o0_ref, l0_ref, o1_ref, l1_ref, o2_ref, l2_ref,
                  wa_ref, wb_ref, wc_ref, wo_ref, g_ref, mod_ref, out_ref, nat_ref):
    tm = h_ref.shape[0]

    def natural(ref, slot):
        dil = ref.shape[0]
        if dil == 1:
            return ref[0]
        ln = LANES_V7X
        chunks = ref.shape[2] // ln
        for r in range(dil):
            for c in range(chunks):
                nat_ref[slot * chunks + c, pl.ds(r, tm // dil, stride=dil), :] = ref[r, :, c * ln:(c + 1) * ln]
        return jnp.concatenate([nat_ref[slot * chunks + c] for c in range(chunks)], axis=1)

    l0, l1, l2 = natural(l0_ref, 0), natural(l1_ref, 0), natural(l2_ref, 1)
    o0, o1, o2 = natural(o0_ref, 0), natural(o1_ref, 2), natural(o2_ref, 3)
    mx = jnp.maximum(jnp.maximum(l0, l1), l2)
    e0, e1, e2 = jnp.exp(l0 - mx), jnp.exp(l1 - mx), jnp.exp(l2 - mx)
    oc = (e0 * o0 + e1 * o1 + e2 * o2) / (e0 + e1 + e2)
    ya = _dot(oa_ref[...], wa_ref[...])
    yb = _dot(ob_ref[...], wb_ref[...])
    yc = _dot(oc.astype(jnp.bfloat16), wc_ref[...])
    d = D_MODEL
    gates = jax.nn.sigmoid(zg_ref[...].astype(jnp.float32))
    merged = gates[:, 0:d] * ya + gates[:, d:2 * d] * yb + gates[:, 2 * d:3 * d] * yc
    y = _dot(merged.astype(jnp.bfloat16), wo_ref[...])
    out_ref[...] = h_ref[...] + mod_ref[2:3, :] * _rmsnorm(y, g_ref[...])


def _merge_call(h, z, oa, ob, oc_lse, wa, wb, wc, wo, g, mod, layer):
    s, d = h.shape
    tm = ROW_TILE
    cw = C_HEADS_PER_GROUP * HEAD_DIM
    row = lambda w: pl.BlockSpec((tm, w), lambda i: (i, 0))
    res = lambda a: pl.BlockSpec((a.shape[0], tm // a.shape[0], cw), lambda i: (0, i, 0))
    full = lambda a: pl.BlockSpec(a.shape, lambda i: (0, 0), pipeline_mode=pl.Buffered(1))
    stacked = lambda a: pl.BlockSpec((None,) + a.shape[1:], lambda i: (layer, 0, 0), pipeline_mode=pl.Buffered(1))
    return pl.pallas_call(
        _merge_kernel,
        grid=(s // tm,),
        in_specs=[row(d), row(3 * d), row(oa.shape[1]), row(ob.shape[1])] + [res(a) for a in oc_lse]
        + [stacked(wa), stacked(wb), stacked(wc), stacked(wo), full(g), full(mod)],
        out_specs=row(d),
        out_shape=jax.ShapeDtypeStruct((s, d), jnp.float32),
        scratch_shapes=[pltpu.VMEM((4 * cw // LANES_V7X, tm, LANES_V7X), jnp.float32)],
        input_output_aliases={0: 0},
        compiler_params=_params(("arbitrary",)),
        name="merge_out_proj",
    )(h, z, oa, ob, *oc_lse, wa, wb, wc, wo, g, mod)


FFN_CHUNK = 1408


def _ffn_kernel(h_ref, g_pre_ref, g_post_ref, mod_ref, wg_ref, wu_ref, wd_ref, out_ref):
    x = h_ref[...]
    u = _rmsnorm(x, g_pre_ref[...]) * (1.0 + mod_ref[4:5, :]) + mod_ref[3:4, :]
    u = u.astype(jnp.bfloat16)
    y = jnp.zeros(x.shape, jnp.float32)
    for c0 in range(0, D_FF, FFN_CHUNK):
        gate = _dot(u, wg_ref[:, c0:c0 + FFN_CHUNK])
        up = _dot(u, wu_ref[:, c0:c0 + FFN_CHUNK])
        act = (gate * jax.nn.sigmoid(gate) * up).astype(jnp.bfloat16)
        y = y + _dot(act, wd_ref[c0:c0 + FFN_CHUNK, :])
    out_ref[...] = x + mod_ref[5:6, :] * _rmsnorm(y, g_post_ref[...])


def _ffn_call(h, g_pre, g_post, mod, wg, wu, wd, layer):
    s, d = h.shape
    tm = ROW_TILE
    row = pl.BlockSpec((tm, d), lambda i: (i, 0))
    full = lambda a: pl.BlockSpec(a.shape, lambda i: (0, 0), pipeline_mode=pl.Buffered(1))
    stacked = lambda a: pl.BlockSpec((None,) + a.shape[1:], lambda i: (layer, 0, 0), pipeline_mode=pl.Buffered(1))
    return pl.pallas_call(
        _ffn_kernel,
        grid=(s // tm,),
        in_specs=[row, full(g_pre), full(g_post), full(mod), stacked(wg), stacked(wu), stacked(wd)],
        out_specs=row,
        out_shape=jax.ShapeDtypeStruct((s, d), jnp.float32),
        input_output_aliases={0: 0},
        compiler_params=_params(("arbitrary",)),
        name="swiglu_ffn",
    )(h, g_pre, g_post, mod, wg, wu, wd)


def _permute_in_proj(w_in):
    hd = HEAD_DIM
    o_bq = 3 * A_HEADS * hd
    o_bkv = o_bq + B_HEADS * hd
    o_c = o_bkv + 2 * B_KV_HEADS * hd
    o_g = o_c + 3 * C_HEADS * hd
    cw = C_HEADS_PER_GROUP * hd
    sl = lambda a, b: w_in[:, :, a:b]
    parts = [sl(o_g, o_g + 3 * D_MODEL), sl(0, A_HEADS * hd) * (QK_SCALE * LOG2E), sl(A_HEADS * hd, o_bq)]
    parts += [sl(o_bq + hh * hd, o_bq + (hh + 1) * hd) * QK_SCALE for hh in B_HEAD_ORDER]
    parts += [sl(o_bkv, o_c)]
    for g in range(len(C_GROUPS)):
        parts += [sl(o_c + g * 3 * cw, o_c + g * 3 * cw + cw) * QK_SCALE,
                  sl(o_c + g * 3 * cw + cw, o_c + (g + 1) * 3 * cw)]
    out = jnp.concatenate(parts, axis=2).astype(jnp.bfloat16)
    assert out.shape[2] == IN_WIDTH
    return out


def kernel(x, c, w_ada, b_ada, g_pre_mix, g_post_mix, w_in, sinks, w_br_a, w_br_b, w_br_c,
           w_out, g_pre_ffn, g_post_ffn, w_gate, w_up, w_down):
    bn, s, d = x.shape
    assert bn == 1 and d == D_MODEL and s % SEQ_MULTIPLE == 0
    bf = jnp.bfloat16
    hd = HEAD_DIM
    w_in_p = _permute_in_proj(w_in)
    w_br_b_p = jnp.concatenate([w_br_b[:, hh * hd:(hh + 1) * hd] for hh in B_HEAD_ORDER], axis=1).astype(bf)
    w_br_a_b, w_br_c_b, w_out_b = w_br_a.astype(bf), w_br_c.astype(bf), w_out.astype(bf)
    w_gate_b, w_up_b, w_down_b = w_gate.astype(bf), w_up.astype(bf), w_down.astype(bf)

    mod_all = _ada_call(c, w_ada, b_ada).reshape(DEPTH, 6, d)
    h = x.reshape(s, d)
    for l in range(DEPTH):
        mod = mod_all[l]
        z, zc1, zc2 = _in_proj_call(h, g_pre_mix[l].reshape(1, d), mod, w_in_p, l)
        oa = _moba_call(z)
        ob = _swa_call(z, sinks[l])
        oc_lse = (_dilated_call(z.reshape(1, s, Z_WIDTH), 0, Z_C0)
                  + _dilated_call(zc1, 1, 0) + _dilated_call(zc2, 2, 0))
        h = _merge_call(h, z, oa, ob, oc_lse, w_br_a_b, w_br_b_p, w_br_c_b, w_out_b,
                        g_post_mix[l].reshape(1, d), mod, l)
        h = _ffn_call(h, g_pre_ffn[l].reshape(1, d), g_post_ffn[l].reshape(1, d), mod,
                      w_gate_b, w_up_b, w_down_b, l)
    return h.reshape(bn, s, d)
```

```python
import functools

import numpy as np
import jax
import jax.numpy as jnp
from jax import lax
from jax.experimental import pallas as pl
from jax.experimental.pallas import tpu as pltpu

D_MODEL = 1024
DEPTH = 4
HEAD_DIM = 64
A_HEADS = 8
MOBA_BLOCK = 256
MOBA_TOPK = 3
B_HEADS = 8
B_KV_HEADS = 2
B_WINDOW = 128
C_GROUPS = ((128, 1), (512, 4), (2048, 16))
C_HEADS_PER_GROUP = 4
C_HEADS = len(C_GROUPS) * C_HEADS_PER_GROUP
BAND = 128
D_FF = 2816
N_ALIBI_HEADS = A_HEADS + B_HEADS + C_HEADS
SEQ_MULTIPLE = 2048
RMS_EPS = 1e-6

LANES_V7X = 128
BF16_SUBLANES_V7X = 16
VMEM_LIMIT_BYTES_V7X = 56 * 1024 * 1024

ROW_TILE = 256
C_WIDTH = 3 * C_HEADS_PER_GROUP * HEAD_DIM

Z_GATES = 0
Z_A = 3 * D_MODEL
Z_BQ = Z_A + 3 * A_HEADS * HEAD_DIM
Z_BK = Z_BQ + B_HEADS * HEAD_DIM
Z_BV = Z_BK + B_KV_HEADS * HEAD_DIM
Z_C0 = Z_BV + B_KV_HEADS * HEAD_DIM
Z_WIDTH = Z_C0 + C_WIDTH
IN_WIDTH = Z_WIDTH + (len(C_GROUPS) - 1) * C_WIDTH
B_HEAD_ORDER = (0, 4, 1, 5, 2, 6, 3, 7)

NEG_BIG = -1e30
QK_SCALE = HEAD_DIM ** -0.5
LOG2E = 1.4426950408889634


def _alibi_slopes():
    n = N_ALIBI_HEADS
    return [float(2.0 ** (-8.0 * (i + 1) / n)) for i in range(n)]


_SLOPES = _alibi_slopes()
SLOPES_B = _SLOPES[:B_HEADS]
SLOPES_C = _SLOPES[B_HEADS:B_HEADS + C_HEADS]
SLOPES_A = _SLOPES[B_HEADS + C_HEADS:]


def _dot(a, b):
    return jnp.dot(a, b, preferred_element_type=jnp.float32)


def _dot_nt(a, b):
    return lax.dot_general(a, b, (((1,), (1,)), ((), ())), preferred_element_type=jnp.float32)


def _params(semantics):
    return pltpu.CompilerParams(dimension_semantics=semantics, vmem_limit_bytes=VMEM_LIMIT_BYTES_V7X)


def _rmsnorm(x, g):
    return x * lax.rsqrt(jnp.mean(x * x, axis=-1, keepdims=True) + RMS_EPS) * g


def _ada_kernel(c_ref, w_ref, b_ref, o_ref):
    c = c_ref[...]
    sc = c * jax.nn.sigmoid(c)
    o_ref[...] = jnp.sum(w_ref[...] * sc, axis=0, keepdims=True) + b_ref[...]


def _ada_call(c, w_ada, b_ada):
    depth, d, n = w_ada.shape
    tn = 1536
    return pl.pallas_call(
        _ada_kernel,
        grid=(depth, n // tn),
        in_specs=[
            pl.BlockSpec((d, 1), lambda l, j: (0, 0)),
            pl.BlockSpec((None, d, tn), lambda l, j: (l, 0, j)),
            pl.BlockSpec((None, 1, tn), lambda l, j: (l, 0, j)),
        ],
        out_specs=pl.BlockSpec((None, 1, tn), lambda l, j: (l, 0, j)),
        out_shape=jax.ShapeDtypeStruct((depth, 1, n), jnp.float32),
        compiler_params=_params(("arbitrary", "arbitrary")),
        name="adaln_mod",
    )(c.reshape(d, 1), w_ada, b_ada.reshape(depth, 1, n))


IN_PROJ_CHUNK = 512


def _in_proj_kernel(h_ref, g_ref, mod_ref, w_ref, z_ref, zc1_ref, zc2_ref, tmp_ref):
    x = h_ref[...]
    u = _rmsnorm(x, g_ref[...]) * (1.0 + mod_ref[1:2, :]) + mod_ref[0:1, :]
    u = u.astype(jnp.bfloat16)
    for c0 in range(0, Z_WIDTH, IN_PROJ_CHUNK):
        z_ref[:, c0:c0 + IN_PROJ_CHUNK] = _dot(u, w_ref[:, c0:c0 + IN_PROJ_CHUNK]).astype(z_ref.dtype)
    for g, out_ref in ((1, zc1_ref), (2, zc2_ref)):
        c0 = Z_WIDTH + (g - 1) * C_WIDTH
        res = _dot(u, w_ref[:, c0:c0 + C_WIDTH])
        dil = C_GROUPS[g][1]
        rows = x.shape[0] // dil
        ln = LANES_V7X
        for c in range(C_WIDTH // ln):
            tmp_ref[c] = res[:, c * ln:(c + 1) * ln]
        for r in range(dil):
            for c in range(C_WIDTH // ln):
                out_ref[r, :, c * ln:(c + 1) * ln] = tmp_ref[c, pl.ds(r, rows, stride=dil), :].astype(out_ref.dtype)


def _in_proj_call(h, g, mod, w, layer):
    s, d = h.shape
    tm = ROW_TILE
    d1, d2 = C_GROUPS[1][1], C_GROUPS[2][1]
    assert tm % (d2 * BF16_SUBLANES_V7X) == 0
    return pl.pallas_call(
        _in_proj_kernel,
        grid=(s // tm,),
        in_specs=[
            pl.BlockSpec((tm, d), lambda i: (i, 0)),
            pl.BlockSpec((1, d), lambda i: (0, 0)),
            pl.BlockSpec((6, d), lambda i: (0, 0)),
            pl.BlockSpec((None, d, IN_WIDTH), lambda i: (layer, 0, 0), pipeline_mode=pl.Buffered(1)),
        ],
        out_specs=[
            pl.BlockSpec((tm, Z_WIDTH), lambda i: (i, 0)),
            pl.BlockSpec((d1, tm // d1, C_WIDTH), lambda i: (0, i, 0)),
            pl.BlockSpec((d2, tm // d2, C_WIDTH), lambda i: (0, i, 0)),
        ],
        out_shape=[
            jax.ShapeDtypeStruct((s, Z_WIDTH), jnp.bfloat16),
            jax.ShapeDtypeStruct((d1, s // d1, C_WIDTH), jnp.bfloat16),
            jax.ShapeDtypeStruct((d2, s // d2, C_WIDTH), jnp.bfloat16),
        ],
        scratch_shapes=[pltpu.VMEM((C_WIDTH // LANES_V7X, tm, LANES_V7X), jnp.float32)],
        compiler_params=_params(("arbitrary",)),
        name="in_proj",
    )(h, g, mod, w)


SUBLANES_V7X = 8
MOBA_V_ROWS = HEAD_DIM + BF16_SUBLANES_V7X
MOBA_GROUPS = MOBA_BLOCK // SUBLANES_V7X
MOBA_UNROLL_LOG2 = 1
MOBA_UNROLL = 1 << MOBA_UNROLL_LOG2


def _all_sublanes_max(x):
    for shift in (4, 2, 1):
        x = jnp.maximum(x, pltpu.roll(x, shift, axis=0))
    return x


def _moba_kernel(slopes_ref, q_ref, k_ref, v_ref, o_ref, vt_ref, km_ref, ka_ref, sel_ref, s_ref, e_ref):
    p = pl.program_id(0)
    i = pl.program_id(1)
    nblk = vt_ref.shape[0]
    blk = MOBA_BLOCK
    half = HEAD_DIM
    heads = (0, 1)
    sub = SUBLANES_V7X
    tile3 = (MOBA_GROUPS, sub, blk)
    key_pos = lax.broadcasted_iota(jnp.int32, tile3, 0) * sub + lax.broadcasted_iota(jnp.int32, tile3, 1)
    qry_pos = lax.broadcasted_iota(jnp.int32, tile3, 2)
    lane = lax.broadcasted_iota(jnp.int32, (blk, LANES_V7X), 1)
    own = [lane < half, lane >= half]
    spare = [half, 0]
    bias_lanes = [(lane == spare[h]) | (lane == spare[h] + 1) for h in heads]

    @pl.when(i == 0)
    def _prepare_pair():
        ones = jnp.ones((BF16_SUBLANES_V7X, blk), jnp.bfloat16)

        def body(j, carry):
            rows = pl.ds(pl.multiple_of(j * blk, blk), blk)
            vt = v_ref[rows, :].astype(jnp.float32).T.astype(jnp.bfloat16)
            for h in heads:
                vt_ref[j, h, 0:half, :] = vt[h * half:(h + 1) * half, :]
                vt_ref[j, h, half:, :] = ones
            k2 = k_ref[rows, :]
            km_ref[pl.ds(j, 1), :] = jnp.mean(k2.astype(jnp.float32), axis=0, keepdims=True)
            k2f = k2.astype(jnp.float32)
            pos = lax.broadcasted_iota(jnp.int32, k2.shape, 0).astype(jnp.float32)
            for h in heads:
                ka = jnp.where(own[h], k2f, jnp.where(bias_lanes[h], pos, 0.0))
                ka_ref[h, rows, :] = ka.astype(jnp.bfloat16)
            return carry
        lax.fori_loop(0, nblk, body, 0)

    q2 = q_ref[...]
    qz = [jnp.where(own[h], q2, jnp.zeros_like(q2)) for h in heads]
    q2f = q2.astype(jnp.float32)
    qh = []
    for h in heads:
        hi = slopes_ref[A_HEADS + 2 * p + h]
        lo = slopes_ref[2 * A_HEADS + 2 * p + h]
        extra = jnp.where(lane == spare[h], hi, jnp.where(lane == spare[h] + 1, lo, 0.0))
        qh.append(jnp.where(own[h], q2f, extra).astype(jnp.bfloat16))

    blk_id = lax.broadcasted_iota(jnp.int32, (nblk, blk), 0)
    km = km_ref[...]
    km_hi = km.astype(jnp.bfloat16)
    km_lo = (km - km_hi.astype(jnp.float32)).astype(jnp.bfloat16)
    gates = [_dot_nt(km_hi, qz[h]) + _dot_nt(km_lo, qz[h]) for h in heads]
    for h in heads:
        gate = jnp.where(blk_id < i, gates[h], -jnp.inf)
        sel = jnp.full((nblk, blk), NEG_BIG, jnp.float32)
        for _ in range(MOBA_TOPK):
            mx = jnp.max(gate, axis=0, keepdims=True)
            cand = (gate == mx) & (mx > -jnp.inf)
            idx = jnp.min(jnp.where(cand, blk_id, nblk), axis=0, keepdims=True)
            chosen = blk_id == idx
            sel = jnp.where(chosen, 0.0, sel)
            gate = jnp.where(chosen, -jnp.inf, gate)
        sel_ref[h] = sel

    unroll = MOBA_UNROLL
    acc_groups = MOBA_V_ROWS // sub

    def past_block(n):
        return jnp.clip(n - 1, 0, nblk - 1)

    def issue_scores(blocks):
        out = []
        for b in blocks:
            rows = pl.ds(pl.multiple_of(b * blk, blk), blk)
            out.append([_dot_nt(ka_ref[h, rows, :], qh[h]).reshape(tile3) for h in heads])
        return out

    def stage_scores(raws, slot, items, own_first):
        out = []
        for u, n in enumerate(items):
            per_head = []
            for h in heads:
                s = raws[u][h]
                if own_first and u == 0:
                    s = jnp.where(qry_pos >= key_pos, s, NEG_BIG)
                    row = jnp.zeros((sub, blk), jnp.float32)
                else:
                    b = past_block(n)
                    gap = ((i - b) * blk).astype(jnp.float32)
                    row = sel_ref[h, pl.ds(b, 1), :] - slopes_ref[2 * p + h] * gap
                    row = jnp.broadcast_to(jnp.where(n <= i, row, NEG_BIG), (sub, blk))
                s_ref[slot, u, h] = s
                per_head += [_all_sublanes_max(jnp.max(s, axis=0)) + row, row]
            out.append(per_head)
        return out

    def exponentiate(group_slot, ms, staged):
        new_ms, alphas = [], []
        for h in heads:
            m_new = ms[h]
            for u in range(unroll):
                m_new = jnp.maximum(m_new, staged[u][2 * h])
            alphas.append(jnp.exp2(ms[h] - m_new))
            new_ms.append(m_new)
            for u in range(unroll):
                e = jnp.exp2(s_ref[group_slot, u, h] - (m_new - staged[u][2 * h + 1])[None])
                e_ref[group_slot, u, h] = e.reshape(blk, blk).astype(jnp.bfloat16)
        return new_ms, alphas

    def accumulate(group, slot, accs, alphas):
        out = []
        for h in heads:
            acc = alphas[h][None] * accs[h]
            for u in range(unroll):
                n = group * unroll + u
                vblock = jnp.where(n == 0, i, past_block(n))
                acc = acc + _dot(vt_ref[vblock, h], e_ref[slot, u, h]).reshape(acc_groups, sub, blk)
            out.append(acc)
        return out

    def score_group(group, slot, own_first):
        items = [group * unroll + u for u in range(unroll)]
        blocks = [i if (own_first and u == 0) else past_block(n) for u, n in enumerate(items)]
        raws = issue_scores(blocks)
        return lambda: stage_scores(raws, slot, items, own_first)

    def flatten(staged):
        return [x for per_head in staged for x in per_head]

    def unflatten(flat):
        return [flat[4 * u:4 * u + 4] for u in range(unroll)]

    def tick(t, slot, state):
        ms, accs, alphas, staged = state
        accs = accumulate(t - 2, slot, accs, alphas)
        finish_scores = score_group(t, slot, False)
        ms, alphas = exponentiate(1 - slot, ms, staged)
        return ms, accs, alphas, finish_scores()

    pairs = jnp.maximum(lax.shift_right_logical(i + 2 * unroll, MOBA_UNROLL_LOG2 + 1), 1)
    ms = [jnp.full((sub, blk), -jnp.inf, jnp.float32) for _ in heads]
    accs = [jnp.zeros((acc_groups, sub, blk), jnp.float32) for _ in heads]
    staged = score_group(0, 0, True)()
    finish_scores = score_group(1, 1, False)
    ms, alphas = exponentiate(0, ms, staged)
    staged = finish_scores()

    def two_ticks(k, carry):
        state = (list(carry[0:2]), list(carry[2:4]), list(carry[4:6]), unflatten(carry[6:]))
        state = tick(2 * k, 0, state)
        ms, accs, alphas, staged = tick(2 * k + 1, 1, state)
        return tuple(ms + accs + alphas + flatten(staged))

    carry = lax.fori_loop(1, pairs, two_ticks, tuple(ms + accs + alphas + flatten(staged)))
    ms, accs, alphas, staged = list(carry[0:2]), list(carry[2:4]), list(carry[4:6]), unflatten(carry[6:])
    accs = accumulate(2 * pairs - 2, 0, accs, alphas)
    ms, alphas = exponentiate(1, ms, staged)
    accs = accumulate(2 * pairs - 1, 1, accs, alphas)
    o_t = jnp.concatenate([(accs[h][0:half // sub] / accs[h][half // sub][None]).reshape(half, blk)
                           for h in heads], axis=0)
    o_ref[...] = o_t.T.astype(o_ref.dtype)


def _moba_slopes():
    full = np.asarray([sl * LOG2E for sl in SLOPES_A], np.float32)
    hi = full.astype(jnp.bfloat16).astype(np.float32)
    lo = (full - hi).astype(jnp.bfloat16).astype(np.float32)
    return jnp.asarray(np.concatenate([full, hi, lo]))


def _moba_call(z):
    s = z.shape[0]
    nblk = s // MOBA_BLOCK
    pairs = A_HEADS // 2
    ln = LANES_V7X
    qb, kb, vb = Z_A // ln, Z_A // ln + pairs, Z_A // ln + 2 * pairs
    return pl.pallas_call(
        _moba_kernel,
        grid=(pairs, nblk),
        in_specs=[
            pl.BlockSpec(memory_space=pltpu.SMEM),
            pl.BlockSpec((MOBA_BLOCK, ln), lambda p, i: (i, qb + p)),
            pl.BlockSpec((s, ln), lambda p, i: (0, kb + p)),
            pl.BlockSpec((s, ln), lambda p, i: (0, vb + p)),
        ],
        out_specs=pl.BlockSpec((MOBA_BLOCK, ln), lambda p, i: (i, p)),
        out_shape=jax.ShapeDtypeStruct((s, A_HEADS * HEAD_DIM), jnp.bfloat16),
        scratch_shapes=[
            pltpu.VMEM((nblk, 2, MOBA_V_ROWS, MOBA_BLOCK), jnp.bfloat16),
            pltpu.VMEM((nblk, ln), jnp.float32),
            pltpu.VMEM((2, s, ln), jnp.bfloat16),
            pltpu.VMEM((2, nblk, MOBA_BLOCK), jnp.float32),
            pltpu.VMEM((2, MOBA_UNROLL, 2, MOBA_GROUPS, SUBLANES_V7X, MOBA_BLOCK), jnp.float32),
            pltpu.VMEM((2, MOBA_UNROLL, 2, MOBA_BLOCK, MOBA_BLOCK), jnp.bfloat16),
        ],
        compiler_params=_params(("arbitrary", "arbitrary")),
        name="moba_attn",
    )(_moba_slopes(), z, z, z)


def _blocks_per_step(nb):
    return max(c for c in (4, 2, 1) if nb % c == 0)


def _band_masks(first_step, max_steps):
    qi = lax.broadcasted_iota(jnp.int32, (BAND, 2 * BAND), 0)
    kj = lax.broadcasted_iota(jnp.int32, (BAND, 2 * BAND), 1)
    steps = qi + BAND - kj
    inner = (steps >= 0) & (steps <= max_steps)
    first = inner & (jnp.logical_not(first_step) | (kj >= BAND))
    return steps.astype(jnp.float32), first, inner


def _band_units(units, stepsf, lookahead):
    raws = {j: _dot_nt(units[j][0], units[j][1]) for j in range(min(lookahead, len(units)))}
    outs, lses = [], []
    for j, (_, _, vv, mask, sd, sink) in enumerate(units):
        ahead = j + lookahead
        if ahead < len(units):
            raws[ahead] = _dot_nt(units[ahead][0], units[ahead][1])
        s = jnp.where(mask, raws.pop(j) - sd * stepsf, NEG_BIG)
        m = jnp.max(s, axis=1, keepdims=True)
        if sink is not None:
            m = jnp.maximum(m, sink)
        e = jnp.exp(s - m)
        denom = jnp.sum(e, axis=1, keepdims=True)
        if sink is not None:
            denom = denom + jnp.exp(sink - m)
        outs.append(_dot(e.astype(jnp.bfloat16), vv) / denom)
        lses.append(m + jnp.log(denom))
    return outs, lses


def _swa_kernel(sinks_ref, q_ref, kp_ref, ko_ref, vp_ref, vo_ref, o_ref):
    n = pl.program_id(0)
    nbq = q_ref.shape[0] // BAND
    stepsf, mask_first, mask_inner = _band_masks(n == 0, B_WINDOW - 1)
    k_all = jnp.concatenate([kp_ref[...], ko_ref[...]], axis=0)
    v_all = jnp.concatenate([vp_ref[...], vo_ref[...]], axis=0)
    ln = LANES_V7X
    lane = lax.broadcasted_iota(jnp.int32, (BAND, ln), 1)
    low = lane < HEAD_DIM
    units = []
    for qb in range(nbq):
        kk = k_all[qb * BAND:(qb + 2) * BAND]
        vv = v_all[qb * BAND:(qb + 2) * BAND]
        mask = mask_first if qb == 0 else mask_inner
        for b in range(B_HEADS // 2):
            q2 = q_ref[qb * BAND:(qb + 1) * BAND, b * ln:(b + 1) * ln]
            for c in range(2):
                head = B_HEAD_ORDER[2 * b + c]
                qm = jnp.where(low if c == 0 else jnp.logical_not(low), q2, jnp.zeros_like(q2))
                units.append((qm, kk, vv, mask, SLOPES_B[head], sinks_ref[head]))
    outs, _ = _band_units(units, stepsf, B_HEADS)
    for qb in range(nbq):
        for b in range(B_HEADS // 2):
            j = qb * B_HEADS + 2 * b
            o_ref[qb * BAND:(qb + 1) * BAND, b * ln:(b + 1) * ln] = (
                jnp.where(low, outs[j], outs[j + 1]).astype(o_ref.dtype))


def _swa_call(z, sinks):
    s = z.shape[0]
    nbq = _blocks_per_step(s // BAND)
    rows = nbq * BAND
    ln = LANES_V7X
    qw = B_HEADS * HEAD_DIM
    prev = lambda n: jnp.maximum(n * nbq - 1, 0)
    return pl.pallas_call(
        _swa_kernel,
        grid=(s // rows,),
        in_specs=[
            pl.BlockSpec(memory_space=pltpu.SMEM),
            pl.BlockSpec((rows, qw), lambda n: (n, Z_BQ // qw)),
            pl.BlockSpec((BAND, ln), lambda n: (prev(n), Z_BK // ln)),
            pl.BlockSpec((rows, ln), lambda n: (n, Z_BK // ln)),
            pl.BlockSpec((BAND, ln), lambda n: (prev(n), Z_BV // ln)),
            pl.BlockSpec((rows, ln), lambda n: (n, Z_BV // ln)),
        ],
        out_specs=pl.BlockSpec((rows, qw), lambda n: (n, 0)),
        out_shape=jax.ShapeDtypeStruct((s, qw), jnp.bfloat16),
        compiler_params=_params(("arbitrary",)),
        name="swa_attn",
    )(sinks, z, z, z, z, z)


def _dilated_kernel(q_ref, kp_ref, ko_ref, vp_ref, vo_ref, o_ref, lse_ref, *, group, dilation, max_steps):
    n = pl.program_id(1)
    nbq = q_ref.shape[0] // BAND
    stepsf, mask_first, mask_inner = _band_masks(n == 0, max_steps)
    k_all = jnp.concatenate([kp_ref[...], ko_ref[...]], axis=0)
    v_all = jnp.concatenate([vp_ref[...], vo_ref[...]], axis=0)
    cw = q_ref.shape[1]
    lane = lax.broadcasted_iota(jnp.int32, (BAND, cw), 1)
    heads = range(C_HEADS_PER_GROUP)
    mine = [(lane >= h * HEAD_DIM) & (lane < (h + 1) * HEAD_DIM) for h in heads]
    units = []
    for qb in range(nbq):
        q4 = q_ref[qb * BAND:(qb + 1) * BAND, :]
        kk = k_all[qb * BAND:(qb + 2) * BAND]
        vv = v_all[qb * BAND:(qb + 2) * BAND]
        mask = mask_first if qb == 0 else mask_inner
        for h in heads:
            slope = SLOPES_C[group * C_HEADS_PER_GROUP + h] * dilation
            units.append((jnp.where(mine[h], q4, jnp.zeros_like(q4)), kk, vv, mask, slope, None))
    outs, lses = _band_units(units, stepsf, C_HEADS_PER_GROUP)
    for qb in range(nbq):
        o_all = jnp.zeros((BAND, cw), jnp.float32)
        lse_all = jnp.zeros((BAND, cw), jnp.float32)
        for h in heads:
            j = qb * C_HEADS_PER_GROUP + h
            o_all = jnp.where(mine[h], outs[j], o_all)
            lse_all = jnp.where(mine[h], lses[j], lse_all)
        o_ref[qb * BAND:(qb + 1) * BAND, :] = o_all
        lse_ref[qb * BAND:(qb + 1) * BAND, :] = lse_all


def _dilated_call(zc, group, col0):
    window, dil = C_GROUPS[group]
    assert zc.shape[0] == dil
    length = zc.shape[1]
    nbq = _blocks_per_step(length // BAND)
    rows = nbq * BAND
    cw = C_HEADS_PER_GROUP * HEAD_DIM
    base = col0 // cw
    kern = functools.partial(_dilated_kernel, group=group, dilation=dil, max_steps=window // dil)
    own = lambda col: pl.BlockSpec((None, rows, cw), lambda r, n: (r, n, base + col))
    prev = lambda col: pl.BlockSpec((None, BAND, cw), lambda r, n: (r, jnp.maximum(n * nbq - 1, 0), base + col))
    out_blk = pl.BlockSpec((None, rows, cw), lambda r, n: (r, n, 0))
    return pl.pallas_call(
        kern,
        grid=(dil, length // rows),
        in_specs=[own(0), prev(1), own(1), prev(2), own(2)],
        out_specs=[out_blk, out_blk],
        out_shape=[jax.ShapeDtypeStruct((dil, length, cw), jnp.float32)] * 2,
        compiler_params=_params(("arbitrary", "arbitrary")),
        name=f"dilated_attn_g{group}",
    )(zc, zc, zc, zc, zc)


def _merge_kernel(h_ref, zg_ref, oa_ref, ob_ref, o0_ref, l0_ref, o1_ref, l1_ref, o2_ref, l2_ref,
                  wa_ref, wb_ref, wc_ref, wo_ref, g_ref, mod_ref, out_ref, nat_ref):
    tm = h_ref.shape[0]

    def natural(ref, slot):
        dil = ref.shape[0]
        if dil == 1:
            return ref[0]
        ln = LANES_V7X
        chunks = ref.shape[2] // ln
        for r in range(dil):
            for c in range(chunks):
                nat_ref[slot * chunks + c, pl.ds(r, tm // dil, stride=dil), :] = ref[r, :, c * ln:(c + 1) * ln]
        return jnp.concatenate([nat_ref[slot * chunks + c] for c in range(chunks)], axis=1)

    l0, l1, l2 = natural(l0_ref, 0), natural(l1_ref, 0), natural(l2_ref, 1)
    o0, o1, o2 = natural(o0_ref, 0), natural(o1_ref, 2), natural(o2_ref, 3)
    mx = jnp.maximum(jnp.maximum(l0, l1), l2)
    e0, e1, e2 = jnp.exp(l0 - mx), jnp.exp(l1 - mx), jnp.exp(l2 - mx)
    oc = ((e0 * o0 + e1 * o1 + e2 * o2) / (e0 + e1 + e2)).astype(jnp.bfloat16)
    d = D_MODEL
    subs = range(0, tm, MERGE_SUB_ROWS)
    branches = [(_dot(oa_ref[pl.ds(r0, MERGE_SUB_ROWS), :], wa_ref[...]),
                 _dot(ob_ref[pl.ds(r0, MERGE_SUB_ROWS), :], wb_ref[...]),
                 _dot(oc[r0:r0 + MERGE_SUB_ROWS], wc_ref[...])) for r0 in subs]
    ys = []
    for r0, (ya, yb, yc) in zip(subs, branches):
        gates = jax.nn.sigmoid(zg_ref[pl.ds(r0, MERGE_SUB_ROWS), :].astype(jnp.float32))
        merged = gates[:, 0:d] * ya + gates[:, d:2 * d] * yb + gates[:, 2 * d:3 * d] * yc
        ys.append(_dot(merged.astype(jnp.bfloat16), wo_ref[...]))
    for r0, y in zip(subs, ys):
        rs = pl.ds(r0, MERGE_SUB_ROWS)
        out_ref[rs, :] = h_ref[rs, :] + mod_ref[2:3, :] * _rmsnorm(y, g_ref[...])


MERGE_SUB_ROWS = 256


def _merge_call(h, z, oa, ob, oc_lse, wa, wb, wc, wo, g, mod, layer):
    s, d = h.shape
    tm = 2 * MERGE_SUB_ROWS
    cw = C_HEADS_PER_GROUP * HEAD_DIM
    row = lambda w: pl.BlockSpec((tm, w), lambda i: (i, 0))
    res = lambda a: pl.BlockSpec((a.shape[0], tm // a.shape[0], cw), lambda i: (0, i, 0))
    full = lambda a: pl.BlockSpec(a.shape, lambda i: (0, 0), pipeline_mode=pl.Buffered(1))
    stacked = lambda a: pl.BlockSpec((None,) + a.shape[1:], lambda i: (layer, 0, 0), pipeline_mode=pl.Buffered(1))
    return pl.pallas_call(
        _merge_kernel,
        grid=(s // tm,),
        in_specs=[row(d), row(3 * d), row(oa.shape[1]), row(ob.shape[1])] + [res(a) for a in oc_lse]
        + [stacked(wa), stacked(wb), stacked(wc), stacked(wo), full(g), full(mod)],
        out_specs=row(d),
        out_shape=jax.ShapeDtypeStruct((s, d), jnp.float32),
        scratch_shapes=[pltpu.VMEM((4 * cw // LANES_V7X, tm, LANES_V7X), jnp.float32)],
        input_output_aliases={0: 0} if layer > 0 else {},
        compiler_params=_params(("arbitrary",)),
        name="merge_out_proj",
    )(h, z, oa, ob, *oc_lse, wa, wb, wc, wo, g, mod)


FFN_CHUNK = 1408


def _ffn_kernel(h_ref, g_pre_ref, g_post_ref, mod_ref, wg_ref, wu_ref, wd_ref, out_ref):
    x = h_ref[...]
    u = _rmsnorm(x, g_pre_ref[...]) * (1.0 + mod_ref[4:5, :]) + mod_ref[3:4, :]
    u = u.astype(jnp.bfloat16)
    y = jnp.zeros(x.shape, jnp.float32)
    for c0 in range(0, D_FF, FFN_CHUNK):
        gate = _dot(u, wg_ref[:, c0:c0 + FFN_CHUNK])
        up = _dot(u, wu_ref[:, c0:c0 + FFN_CHUNK])
        act = (gate * jax.nn.sigmoid(gate) * up).astype(jnp.bfloat16)
        y = y + _dot(act, wd_ref[c0:c0 + FFN_CHUNK, :])
    out_ref[...] = x + mod_ref[5:6, :] * _rmsnorm(y, g_post_ref[...])


def _ffn_call(h, g_pre, g_post, mod, wg, wu, wd, layer):
    s, d = h.shape
    tm = ROW_TILE
    row = pl.BlockSpec((tm, d), lambda i: (i, 0))
    full = lambda a: pl.BlockSpec(a.shape, lambda i: (0, 0), pipeline_mode=pl.Buffered(1))
    stacked = lambda a: pl.BlockSpec((None,) + a.shape[1:], lambda i: (layer, 0, 0), pipeline_mode=pl.Buffered(1))
    return pl.pallas_call(
        _ffn_kernel,
        grid=(s // tm,),
        in_specs=[row, full(g_pre), full(g_post), full(mod), stacked(wg), stacked(wu), stacked(wd)],
        out_specs=row,
        out_shape=jax.ShapeDtypeStruct((s, d), jnp.float32),
        input_output_aliases={0: 0},
        compiler_params=_params(("arbitrary",)),
        name="swiglu_ffn",
    )(h, g_pre, g_post, mod, wg, wu, wd)


def _permute_in_proj(w_in):
    hd = HEAD_DIM
    o_bq = 3 * A_HEADS * hd
    o_bkv = o_bq + B_HEADS * hd
    o_c = o_bkv + 2 * B_KV_HEADS * hd
    o_g = o_c + 3 * C_HEADS * hd
    cw = C_HEADS_PER_GROUP * hd
    sl = lambda a, b: w_in[:, :, a:b]
    parts = [sl(o_g, o_g + 3 * D_MODEL), sl(0, A_HEADS * hd) * (QK_SCALE * LOG2E), sl(A_HEADS * hd, o_bq)]
    parts += [sl(o_bq + hh * hd, o_bq + (hh + 1) * hd) * QK_SCALE for hh in B_HEAD_ORDER]
    parts += [sl(o_bkv, o_c)]
    for g in range(len(C_GROUPS)):
        parts += [sl(o_c + g * 3 * cw, o_c + g * 3 * cw + cw) * QK_SCALE,
                  sl(o_c + g * 3 * cw + cw, o_c + (g + 1) * 3 * cw)]
    out = jnp.concatenate([part.astype(jnp.bfloat16) for part in parts], axis=2)
    assert out.shape[2] == IN_WIDTH
    return out


def kernel(x, c, w_ada, b_ada, g_pre_mix, g_post_mix, w_in, sinks, w_br_a, w_br_b, w_br_c,
           w_out, g_pre_ffn, g_post_ffn, w_gate, w_up, w_down):
    bn, s, d = x.shape
    assert bn == 1 and d == D_MODEL and s % SEQ_MULTIPLE == 0
    bf = jnp.bfloat16
    hd = HEAD_DIM
    w_in_p = _permute_in_proj(w_in)
    w_br_b_p = jnp.concatenate([w_br_b[:, hh * hd:(hh + 1) * hd] for hh in B_HEAD_ORDER], axis=1).astype(bf)
    w_br_a_b, w_br_c_b, w_out_b = w_br_a.astype(bf), w_br_c.astype(bf), w_out.astype(bf)
    w_gate_b, w_up_b, w_down_b = w_gate.astype(bf), w_up.astype(bf), w_down.astype(bf)

    mod_all = _ada_call(c, w_ada, b_ada).reshape(DEPTH, 6, d)
    h = x.reshape(s, d)
    for l in range(DEPTH):
        mod = mod_all[l]
        z, zc1, zc2 = _in_proj_call(h, g_pre_mix[l].reshape(1, d), mod, w_in_p, l)
        oa = _moba_call(z)
        ob = _swa_call(z, sinks[l])
        oc_lse = (_dilated_call(z.reshape(1, s, Z_WIDTH), 0, Z_C0)
                  + _dilated_call(zc1, 1, 0) + _dilated_call(zc2, 2, 0))
        h = _merge_call(h, z, oa, ob, oc_lse, w_br_a_b, w_br_b_p, w_br_c_b, w_out_b,
                        g_post_mix[l].reshape(1, d), mod, l)
        h = _ffn_call(h, g_pre_ffn[l].reshape(1, d), g_post_ffn[l].reshape(1, d), mod,
                      w_gate_b, w_up_b, w_down_b, l)
    return h.reshape(bn, s, d)
```

```python
import numpy as np
import jax
import jax.numpy as jnp
from jax import lax
from jax.experimental import pallas as pl
from jax.experimental.pallas import tpu as pltpu

D_MODEL = 1024
DEPTH = 4
HEAD_DIM = 64
A_HEADS = 8
MOBA_BLOCK = 256
MOBA_TOPK = 3
B_HEADS = 8
B_KV_HEADS = 2
B_WINDOW = 128
C_GROUPS = ((128, 1), (512, 4), (2048, 16))
C_HEADS_PER_GROUP = 4
C_HEADS = len(C_GROUPS) * C_HEADS_PER_GROUP
BAND = 128
D_FF = 2816
N_ALIBI_HEADS = A_HEADS + B_HEADS + C_HEADS
SEQ_MULTIPLE = 2048
RMS_EPS = 1e-6

LANES_V7X = 128
BF16_SUBLANES_V7X = 16
VMEM_LIMIT_BYTES_V7X = 56 * 1024 * 1024

ROW_TILE = 256
C_WIDTH = 3 * C_HEADS_PER_GROUP * HEAD_DIM

Z_GATES = 0
Z_A = 3 * D_MODEL
Z_BQ = Z_A + 3 * A_HEADS * HEAD_DIM
Z_BK = Z_BQ + B_HEADS * HEAD_DIM
Z_BV = Z_BK + B_KV_HEADS * HEAD_DIM
Z_C0 = Z_BV + B_KV_HEADS * HEAD_DIM
Z_WIDTH = Z_C0 + C_WIDTH
IN_WIDTH = Z_WIDTH + (len(C_GROUPS) - 1) * C_WIDTH
B_HEAD_ORDER = (0, 4, 1, 5, 2, 6, 3, 7)

NEG_BIG = -1e30
QK_SCALE = HEAD_DIM ** -0.5
LOG2E = 1.4426950408889634


def _alibi_slopes():
    n = N_ALIBI_HEADS
    return [float(2.0 ** (-8.0 * (i + 1) / n)) for i in range(n)]


_SLOPES = _alibi_slopes()
SLOPES_B = _SLOPES[:B_HEADS]
SLOPES_C = _SLOPES[B_HEADS:B_HEADS + C_HEADS]
SLOPES_A = _SLOPES[B_HEADS + C_HEADS:]


def _dot(a, b):
    return jnp.dot(a, b, preferred_element_type=jnp.float32)


def _dot_nt(a, b):
    return lax.dot_general(a, b, (((1,), (1,)), ((), ())), preferred_element_type=jnp.float32)


def _params(semantics):
    return pltpu.CompilerParams(dimension_semantics=semantics, vmem_limit_bytes=VMEM_LIMIT_BYTES_V7X)


def _rmsnorm(x, g):
    return x * lax.rsqrt(jnp.mean(x * x, axis=-1, keepdims=True) + RMS_EPS) * g


def _ada_kernel(c_ref, w_ref, b_ref, o_ref):
    c = c_ref[...]
    sc = c * jax.nn.sigmoid(c)
    o_ref[...] = jnp.sum(w_ref[...] * sc, axis=0, keepdims=True) + b_ref[...]


def _ada_call(c, w_ada, b_ada):
    depth, d, n = w_ada.shape
    tn = 1536
    return pl.pallas_call(
        _ada_kernel,
        grid=(depth, n // tn),
        in_specs=[
            pl.BlockSpec((d, 1), lambda l, j: (0, 0)),
            pl.BlockSpec((None, d, tn), lambda l, j: (l, 0, j)),
            pl.BlockSpec((None, 1, tn), lambda l, j: (l, 0, j)),
        ],
        out_specs=pl.BlockSpec((None, 1, tn), lambda l, j: (l, 0, j)),
        out_shape=jax.ShapeDtypeStruct((depth, 1, n), jnp.float32),
        compiler_params=_params(("arbitrary", "arbitrary")),
        name="adaln_mod",
    )(c.reshape(d, 1), w_ada, b_ada.reshape(depth, 1, n))


IN_PROJ_CHUNK = 512


def _in_proj_kernel(h_ref, g_ref, mod_ref, w_ref, z_ref, zc1_ref, zc2_ref, tmp_ref):
    x = h_ref[...]
    u = _rmsnorm(x, g_ref[...]) * (1.0 + mod_ref[1:2, :]) + mod_ref[0:1, :]
    u = u.astype(jnp.bfloat16)
    for c0 in range(0, Z_WIDTH, IN_PROJ_CHUNK):
        z_ref[:, c0:c0 + IN_PROJ_CHUNK] = _dot(u, w_ref[:, c0:c0 + IN_PROJ_CHUNK]).astype(z_ref.dtype)
    for g, out_ref in ((1, zc1_ref), (2, zc2_ref)):
        c0 = Z_WIDTH + (g - 1) * C_WIDTH
        res = _dot(u, w_ref[:, c0:c0 + C_WIDTH])
        dil = C_GROUPS[g][1]
        rows = x.shape[0] // dil
        ln = LANES_V7X
        for c in range(C_WIDTH // ln):
            tmp_ref[c] = res[:, c * ln:(c + 1) * ln]
        for r in range(dil):
            for c in range(C_WIDTH // ln):
                out_ref[r, :, c * ln:(c + 1) * ln] = tmp_ref[c, pl.ds(r, rows, stride=dil), :].astype(out_ref.dtype)


def _in_proj_call(h, g, mod, w, layer):
    s, d = h.shape
    tm = ROW_TILE
    d1, d2 = C_GROUPS[1][1], C_GROUPS[2][1]
    assert tm % (d2 * BF16_SUBLANES_V7X) == 0
    return pl.pallas_call(
        _in_proj_kernel,
        grid=(s // tm,),
        in_specs=[
            pl.BlockSpec((tm, d), lambda i: (i, 0)),
            pl.BlockSpec((1, d), lambda i: (0, 0)),
            pl.BlockSpec((6, d), lambda i: (0, 0)),
            pl.BlockSpec((None, d, IN_WIDTH), lambda i: (layer, 0, 0), pipeline_mode=pl.Buffered(1)),
        ],
        out_specs=[
            pl.BlockSpec((tm, Z_WIDTH), lambda i: (i, 0)),
            pl.BlockSpec((d1, tm // d1, C_WIDTH), lambda i: (0, i, 0)),
            pl.BlockSpec((d2, tm // d2, C_WIDTH), lambda i: (0, i, 0)),
        ],
        out_shape=[
            jax.ShapeDtypeStruct((s, Z_WIDTH), jnp.bfloat16),
            jax.ShapeDtypeStruct((d1, s // d1, C_WIDTH), jnp.bfloat16),
            jax.ShapeDtypeStruct((d2, s // d2, C_WIDTH), jnp.bfloat16),
        ],
        scratch_shapes=[pltpu.VMEM((C_WIDTH // LANES_V7X, tm, LANES_V7X), jnp.float32)],
        compiler_params=_params(("arbitrary",)),
        name="in_proj",
    )(h, g, mod, w)


SUBLANES_V7X = 8
MOBA_V_ROWS = HEAD_DIM + BF16_SUBLANES_V7X
MOBA_GROUPS = MOBA_BLOCK // SUBLANES_V7X
MOBA_UNROLL_LOG2 = 1
MOBA_UNROLL = 1 << MOBA_UNROLL_LOG2


def _all_sublanes_max(x):
    for shift in (4, 2, 1):
        x = jnp.maximum(x, pltpu.roll(x, shift, axis=0))
    return x


def _moba_lanes():
    lane = lax.broadcasted_iota(jnp.int32, (MOBA_BLOCK, LANES_V7X), 1)
    own = [lane < HEAD_DIM, lane >= HEAD_DIM]
    spare = [HEAD_DIM, 0]
    bias_lanes = [(lane == spare[h]) | (lane == spare[h] + 1) for h in (0, 1)]
    return lane, own, spare, bias_lanes


def _moba_kernel(slopes_ref, q_ref, k_ref, v_ref, o_ref, vt_ref, km_ref, ka_ref, sel_ref, s_ref, e_ref):
    p = pl.program_id(0)
    nblk = vt_ref.shape[0]
    blk = MOBA_BLOCK
    half = HEAD_DIM
    _, own, _, bias_lanes = _moba_lanes()
    ones = jnp.ones((BF16_SUBLANES_V7X, blk), jnp.bfloat16)

    def prepare(j, carry):
        rows = pl.ds(pl.multiple_of(j * blk, blk), blk)
        vt = v_ref[rows, :].astype(jnp.float32).T.astype(jnp.bfloat16)
        k2 = k_ref[rows, :]
        km_ref[pl.ds(j, 1), :] = jnp.mean(k2.astype(jnp.float32), axis=0, keepdims=True)
        k2f = k2.astype(jnp.float32)
        pos = lax.broadcasted_iota(jnp.int32, k2.shape, 0).astype(jnp.float32)
        for h in (0, 1):
            vt_ref[j, h, 0:half, :] = vt[h * half:(h + 1) * half, :]
            vt_ref[j, h, half:, :] = ones
            ka = jnp.where(own[h], k2f, jnp.where(bias_lanes[h], pos, 0.0))
            ka_ref[h, rows, :] = ka.astype(jnp.bfloat16)
        return carry

    lax.fori_loop(0, nblk, prepare, 0)

    def tile(i, carry):
        _moba_tile(i, p, slopes_ref, q_ref, o_ref, vt_ref, km_ref, ka_ref, sel_ref, s_ref, e_ref)
        return carry

    lax.fori_loop(0, nblk, tile, 0)


def _moba_tile(i, p, slopes_ref, q_ref, o_ref, vt_ref, km_ref, ka_ref, sel_ref, s_ref, e_ref):
    nblk = vt_ref.shape[0]
    blk = MOBA_BLOCK
    half = HEAD_DIM
    heads = (0, 1)
    sub = SUBLANES_V7X
    tile3 = (MOBA_GROUPS, sub, blk)
    key_pos = lax.broadcasted_iota(jnp.int32, tile3, 0) * sub + lax.broadcasted_iota(jnp.int32, tile3, 1)
    qry_pos = lax.broadcasted_iota(jnp.int32, tile3, 2)
    lane, own, spare, _ = _moba_lanes()
    tile_rows = pl.ds(pl.multiple_of(i * blk, blk), blk)

    q2 = q_ref[tile_rows, :]
    qz = [jnp.where(own[h], q2, jnp.zeros_like(q2)) for h in heads]
    q2f = q2.astype(jnp.float32)
    qh = []
    for h in heads:
        hi = slopes_ref[A_HEADS + 2 * p + h]
        lo = slopes_ref[2 * A_HEADS + 2 * p + h]
        extra = jnp.where(lane == spare[h], hi, jnp.where(lane == spare[h] + 1, lo, 0.0))
        qh.append(jnp.where(own[h], q2f, extra).astype(jnp.bfloat16))

    def select_blocks():
        blk_id = lax.broadcasted_iota(jnp.int32, (nblk, blk), 0)
        km = km_ref[...]
        km_hi = km.astype(jnp.bfloat16)
        km_lo = (km - km_hi.astype(jnp.float32)).astype(jnp.bfloat16)
        gates = [_dot_nt(km_hi, qz[h]) + _dot_nt(km_lo, qz[h]) for h in heads]
        for h in heads:
            gate = jnp.where(blk_id < i, gates[h], -jnp.inf)
            sel = jnp.full((nblk, blk), NEG_BIG, jnp.float32)
            for _ in range(MOBA_TOPK):
                mx = jnp.max(gate, axis=0, keepdims=True)
                cand = (gate == mx) & (mx > -jnp.inf)
                idx = jnp.min(jnp.where(cand, blk_id, nblk), axis=0, keepdims=True)
                chosen = blk_id == idx
                sel = jnp.where(chosen, 0.0, sel)
                gate = jnp.where(chosen, -jnp.inf, gate)
            sel_ref[h] = sel

    unroll = MOBA_UNROLL
    acc_groups = MOBA_V_ROWS // sub

    def past_block(n):
        return jnp.clip(n - 1, 0, nblk - 1)

    def issue_scores(blocks):
        out = []
        for b in blocks:
            rows = pl.ds(pl.multiple_of(b * blk, blk), blk)
            out.append([_dot_nt(ka_ref[h, rows, :], qh[h]).reshape(tile3) for h in heads])
        return out

    def stage_scores(raws, slot, items, own_first):
        out = []
        for u, n in enumerate(items):
            per_head = []
            for h in heads:
                s = raws[u][h]
                if own_first and u == 0:
                    s = jnp.where(qry_pos >= key_pos, s, NEG_BIG)
                    row = jnp.zeros((sub, blk), jnp.float32)
                else:
                    b = past_block(n)
                    gap = ((i - b) * blk).astype(jnp.float32)
                    row = sel_ref[h, pl.ds(b, 1), :] - slopes_ref[2 * p + h] * gap
                    row = jnp.broadcast_to(jnp.where(n <= i, row, NEG_BIG), (sub, blk))
                s_ref[slot, u, h] = s
                per_head += [_all_sublanes_max(jnp.max(s, axis=0)) + row, row]
            out.append(per_head)
        return out

    def exponentiate(group_slot, ms, staged):
        new_ms, alphas = [], []
        for h in heads:
            m_new = ms[h]
            for u in range(unroll):
                m_new = jnp.maximum(m_new, staged[u][2 * h])
            alphas.append(jnp.exp2(ms[h] - m_new))
            new_ms.append(m_new)
            for u in range(unroll):
                e = jnp.exp2(s_ref[group_slot, u, h] - (m_new - staged[u][2 * h + 1])[None])
                e_ref[group_slot, u, h] = e.reshape(blk, blk).astype(jnp.bfloat16)
        return new_ms, alphas

    def accumulate(group, slot, accs, alphas):
        out = []
        for h in heads:
            acc = alphas[h][None] * accs[h]
            for u in range(unroll):
                n = group * unroll + u
                vblock = jnp.where(n == 0, i, past_block(n))
                acc = acc + _dot(vt_ref[vblock, h], e_ref[slot, u, h]).reshape(acc_groups, sub, blk)
            out.append(acc)
        return out

    def score_group(group, slot, own_first):
        items = [group * unroll + u for u in range(unroll)]
        blocks = [i if (own_first and u == 0) else past_block(n) for u, n in enumerate(items)]
        raws = issue_scores(blocks)
        return lambda: stage_scores(raws, slot, items, own_first)

    def flatten(staged):
        return [x for per_head in staged for x in per_head]

    def unflatten(flat):
        return [flat[4 * u:4 * u + 4] for u in range(unroll)]

    def tick(t, slot, state):
        ms, accs, alphas, staged = state
        accs = accumulate(t - 2, slot, accs, alphas)
        finish_scores = score_group(t, slot, False)
        ms, alphas = exponentiate(1 - slot, ms, staged)
        return ms, accs, alphas, finish_scores()

    pairs = jnp.maximum(lax.shift_right_logical(i + 2 * unroll, MOBA_UNROLL_LOG2 + 1), 1)
    ms = [jnp.full((sub, blk), -jnp.inf, jnp.float32) for _ in heads]
    accs = [jnp.zeros((acc_groups, sub, blk), jnp.float32) for _ in heads]
    select_blocks()
    staged = score_group(0, 0, True)()
    finish_scores = score_group(1, 1, False)
    ms, alphas = exponentiate(0, ms, staged)
    staged = finish_scores()

    def two_ticks(k, carry):
        state = (list(carry[0:2]), list(carry[2:4]), list(carry[4:6]), unflatten(carry[6:]))
        state = tick(2 * k, 0, state)
        ms, accs, alphas, staged = tick(2 * k + 1, 1, state)
        return tuple(ms + accs + alphas + flatten(staged))

    carry = lax.fori_loop(1, pairs, two_ticks, tuple(ms + accs + alphas + flatten(staged)))
    ms, accs, alphas, staged = list(carry[0:2]), list(carry[2:4]), list(carry[4:6]), unflatten(carry[6:])
    accs = accumulate(2 * pairs - 2, 0, accs, alphas)
    ms, alphas = exponentiate(1, ms, staged)
    accs = accumulate(2 * pairs - 1, 1, accs, alphas)
    o_t = jnp.concatenate([(accs[h][0:half // sub] / accs[h][half // sub][None]).reshape(half, blk)
                           for h in heads], axis=0)
    o_ref[tile_rows, :] = o_t.T.astype(o_ref.dtype)


def _moba_slopes():
    full = np.asarray([sl * LOG2E for sl in SLOPES_A], np.float32)
    hi = full.astype(jnp.bfloat16).astype(np.float32)
    lo = (full - hi).astype(jnp.bfloat16).astype(np.float32)
    return jnp.asarray(np.concatenate([full, hi, lo]))


def _moba_call(z):
    s = z.shape[0]
    nblk = s // MOBA_BLOCK
    pairs = A_HEADS // 2
    ln = LANES_V7X
    qb, kb, vb = Z_A // ln, Z_A // ln + pairs, Z_A // ln + 2 * pairs
    return pl.pallas_call(
        _moba_kernel,
        grid=(pairs,),
        in_specs=[
            pl.BlockSpec(memory_space=pltpu.SMEM),
            pl.BlockSpec((s, ln), lambda p: (0, qb + p)),
            pl.BlockSpec((s, ln), lambda p: (0, kb + p), pipeline_mode=pl.Buffered(1)),
            pl.BlockSpec((s, ln), lambda p: (0, vb + p), pipeline_mode=pl.Buffered(1)),
        ],
        out_specs=pl.BlockSpec((s, ln), lambda p: (0, p)),
        out_shape=jax.ShapeDtypeStruct((s, A_HEADS * HEAD_DIM), jnp.bfloat16),
        scratch_shapes=[
            pltpu.VMEM((nblk, 2, MOBA_V_ROWS, MOBA_BLOCK), jnp.bfloat16),
            pltpu.VMEM((nblk, ln), jnp.float32),
            pltpu.VMEM((2, s, ln), jnp.bfloat16),
            pltpu.VMEM((2, nblk, MOBA_BLOCK), jnp.float32),
            pltpu.VMEM((2, MOBA_UNROLL, 2, MOBA_GROUPS, SUBLANES_V7X, MOBA_BLOCK), jnp.float32),
            pltpu.VMEM((2, MOBA_UNROLL, 2, MOBA_BLOCK, MOBA_BLOCK), jnp.bfloat16),
        ],
        compiler_params=_params(("arbitrary",)),
        name="moba_attn",
    )(_moba_slopes(), z, z, z)


def _blocks_per_step(nb):
    return max(c for c in (4, 2, 1) if nb % c == 0)


def _band_penalty_table(slope_dils, max_steps):
    steps = np.arange(BAND)[:, None] + BAND - np.arange(2 * BAND)[None, :]
    inside = (steps >= 0) & (steps <= max_steps)
    table = [np.where(inside, np.float32(sd) * steps.astype(np.float32), np.float32(-NEG_BIG)) for sd in slope_dils]
    return jnp.asarray(np.stack(table).astype(np.float32))


def _band_no_prev(first_step):
    kj = lax.broadcasted_iota(jnp.int32, (BAND, 2 * BAND), 1)
    return jnp.logical_and(first_step, kj < BAND)


def _band_units(units, lookahead):
    raws = {j: _dot_nt(units[j][0], units[j][1]) for j in range(min(lookahead, len(units)))}
    outs, lses = [], []
    for j, (_, _, vv, penalty, drop, sink) in enumerate(units):
        ahead = j + lookahead
        if ahead < len(units):
            raws[ahead] = _dot_nt(units[ahead][0], units[ahead][1])
        s = raws.pop(j) - penalty
        if drop is not None:
            s = jnp.where(drop, NEG_BIG, s)
        m = jnp.max(s, axis=1, keepdims=True)
        if sink is not None:
            m = jnp.maximum(m, sink)
        e = jnp.exp(s - m)
        denom = jnp.sum(e, axis=1, keepdims=True)
        if sink is not None:
            denom = denom + jnp.exp(sink - m)
        outs.append(_dot(e.astype(jnp.bfloat16), vv) / denom)
        lses.append(m + jnp.log(denom))
    return outs, lses


def _swa_kernel(sinks_ref, bias_ref, q_ref, kp_ref, ko_ref, vp_ref, vo_ref, o_ref):
    n = pl.program_id(0)
    nbq = q_ref.shape[0] // BAND
    no_prev = _band_no_prev(n == 0)
    k_all = jnp.concatenate([kp_ref[...], ko_ref[...]], axis=0)
    v_all = jnp.concatenate([vp_ref[...], vo_ref[...]], axis=0)
    ln = LANES_V7X
    lane = lax.broadcasted_iota(jnp.int32, (BAND, ln), 1)
    low = lane < HEAD_DIM
    units = []
    for qb in range(nbq):
        kk = k_all[qb * BAND:(qb + 2) * BAND]
        vv = v_all[qb * BAND:(qb + 2) * BAND]
        drop = no_prev if qb == 0 else None
        for b in range(B_HEADS // 2):
            q2 = q_ref[qb * BAND:(qb + 1) * BAND, b * ln:(b + 1) * ln]
            for c in range(2):
                head = B_HEAD_ORDER[2 * b + c]
                qm = jnp.where(low if c == 0 else jnp.logical_not(low), q2, jnp.zeros_like(q2))
                units.append((qm, kk, vv, bias_ref[2 * b + c], drop, sinks_ref[head]))
    outs, _ = _band_units(units, B_HEADS)
    for qb in range(nbq):
        for b in range(B_HEADS // 2):
            j = qb * B_HEADS + 2 * b
            o_ref[qb * BAND:(qb + 1) * BAND, b * ln:(b + 1) * ln] = (
                jnp.where(low, outs[j], outs[j + 1]).astype(o_ref.dtype))


def _swa_call(z, sinks):
    s = z.shape[0]
    nbq = _blocks_per_step(s // BAND)
    rows = nbq * BAND
    ln = LANES_V7X
    qw = B_HEADS * HEAD_DIM
    prev = lambda n: jnp.maximum(n * nbq - 1, 0)
    bias = _band_penalty_table([SLOPES_B[head] for head in B_HEAD_ORDER], B_WINDOW - 1)
    return pl.pallas_call(
        _swa_kernel,
        grid=(s // rows,),
        in_specs=[
            pl.BlockSpec(memory_space=pltpu.SMEM),
            pl.BlockSpec(bias.shape, lambda n: (0, 0, 0), pipeline_mode=pl.Buffered(1)),
            pl.BlockSpec((rows, qw), lambda n: (n, Z_BQ // qw)),
            pl.BlockSpec((BAND, ln), lambda n: (prev(n), Z_BK // ln)),
            pl.BlockSpec((rows, ln), lambda n: (n, Z_BK // ln)),
            pl.BlockSpec((BAND, ln), lambda n: (prev(n), Z_BV // ln)),
            pl.BlockSpec((rows, ln), lambda n: (n, Z_BV // ln)),
        ],
        out_specs=pl.BlockSpec((rows, qw), lambda n: (n, 0)),
        out_shape=jax.ShapeDtypeStruct((s, qw), jnp.bfloat16),
        compiler_params=_params(("arbitrary",)),
        name="swa_attn",
    )(sinks, bias, z, z, z, z, z)


def _dilated_kernel(bias_ref, q_ref, kp_ref, ko_ref, vp_ref, vo_ref, o_ref, lse_ref):
    n = pl.program_id(1)
    nbq = q_ref.shape[0] // BAND
    no_prev = _band_no_prev(n == 0)
    k_all = jnp.concatenate([kp_ref[...], ko_ref[...]], axis=0)
    v_all = jnp.concatenate([vp_ref[...], vo_ref[...]], axis=0)
    cw = q_ref.shape[1]
    lane = lax.broadcasted_iota(jnp.int32, (BAND, cw), 1)
    heads = range(C_HEADS_PER_GROUP)
    mine = [(lane >= h * HEAD_DIM) & (lane < (h + 1) * HEAD_DIM) for h in heads]
    biases = [bias_ref[h] for h in heads]
    units = []
    for qb in range(nbq):
        q4 = q_ref[qb * BAND:(qb + 1) * BAND, :]
        kk = k_all[qb * BAND:(qb + 2) * BAND]
        vv = v_all[qb * BAND:(qb + 2) * BAND]
        drop = no_prev if qb == 0 else None
        for h in heads:
            units.append((jnp.where(mine[h], q4, jnp.zeros_like(q4)), kk, vv, biases[h], drop, None))
    outs, lses = _band_units(units, C_HEADS_PER_GROUP)
    for qb in range(nbq):
        o_all = jnp.zeros((BAND, cw), jnp.float32)
        lse_all = jnp.zeros((BAND, cw), jnp.float32)
        for h in heads:
            j = qb * C_HEADS_PER_GROUP + h
            o_all = jnp.where(mine[h], outs[j], o_all)
            lse_all = jnp.where(mine[h], lses[j], lse_all)
        o_ref[qb * BAND:(qb + 1) * BAND, :] = o_all
        lse_ref[qb * BAND:(qb + 1) * BAND, :] = lse_all


def _dilated_call(zc, group, col0):
    window, dil = C_GROUPS[group]
    assert zc.shape[0] == dil
    length = zc.shape[1]
    nbq = _blocks_per_step(length // BAND)
    rows = nbq * BAND
    cw = C_HEADS_PER_GROUP * HEAD_DIM
    base = col0 // cw
    slopes = [SLOPES_C[group * C_HEADS_PER_GROUP + h] * dil for h in range(C_HEADS_PER_GROUP)]
    bias = _band_penalty_table(slopes, window // dil)
    own = lambda col: pl.BlockSpec((None, rows, cw), lambda r, n: (r, n, base + col))
    prev = lambda col: pl.BlockSpec((None, BAND, cw), lambda r, n: (r, jnp.maximum(n * nbq - 1, 0), base + col))
    out_blk = pl.BlockSpec((None, rows, cw), lambda r, n: (r, n, 0))
    return pl.pallas_call(
        _dilated_kernel,
        grid=(dil, length // rows),
        in_specs=[pl.BlockSpec(bias.shape, lambda r, n: (0, 0, 0), pipeline_mode=pl.Buffered(1)),
                  own(0), prev(1), own(1), prev(2), own(2)],
        out_specs=[out_blk, out_blk],
        out_shape=[jax.ShapeDtypeStruct((dil, length, cw), jnp.float32)] * 2,
        compiler_params=_params(("arbitrary", "arbitrary")),
        name=f"dilated_attn_g{group}",
    )(bias, zc, zc, zc, zc, zc)


def _merge_kernel(h_ref, zg_ref, oa_ref, ob_ref, o0_ref, l0_ref, o1_ref, l1_ref, o2_ref, l2_ref,
                  wa_ref, wb_ref, wc_ref, wo_ref, g_ref, mod_ref, out_ref, nat_ref):
    tm = h_ref.shape[0]

    def natural(ref, slot):
        dil = ref.shape[0]
        if dil == 1:
            return ref[0]
        ln = LANES_V7X
        chunks = ref.shape[2] // ln
        for r in range(dil):
            for c in range(chunks):
                nat_ref[slot * chunks + c, pl.ds(r, tm // dil, stride=dil), :] = ref[r, :, c * ln:(c + 1) * ln]
        return jnp.concatenate([nat_ref[slot * chunks + c] for c in range(chunks)], axis=1)

    l0, l1, l2 = natural(l0_ref, 0), natural(l1_ref, 0), natural(l2_ref, 1)
    o0, o1, o2 = natural(o0_ref, 0), natural(o1_ref, 2), natural(o2_ref, 3)
    mx = jnp.maximum(jnp.maximum(l0, l1), l2)
    e0, e1, e2 = jnp.exp(l0 - mx), jnp.exp(l1 - mx), jnp.exp(l2 - mx)
    oc = ((e0 * o0 + e1 * o1 + e2 * o2) / (e0 + e1 + e2)).astype(jnp.bfloat16)
    d = D_MODEL
    subs = range(0, tm, MERGE_SUB_ROWS)
    branches = [(_dot(oa_ref[pl.ds(r0, MERGE_SUB_ROWS), :], wa_ref[...]),
                 _dot(ob_ref[pl.ds(r0, MERGE_SUB_ROWS), :], wb_ref[...]),
                 _dot(oc[r0:r0 + MERGE_SUB_ROWS], wc_ref[...])) for r0 in subs]
    ys = []
    for r0, (ya, yb, yc) in zip(subs, branches):
        gates = jax.nn.sigmoid(zg_ref[pl.ds(r0, MERGE_SUB_ROWS), :].astype(jnp.float32))
        merged = gates[:, 0:d] * ya + gates[:, d:2 * d] * yb + gates[:, 2 * d:3 * d] * yc
        ys.append(_dot(merged.astype(jnp.bfloat16), wo_ref[...]))
    for r0, y in zip(subs, ys):
        rs = pl.ds(r0, MERGE_SUB_ROWS)
        out_ref[rs, :] = h_ref[rs, :] + mod_ref[2:3, :] * _rmsnorm(y, g_ref[...])


MERGE_SUB_ROWS = 256


def _merge_call(h, z, oa, ob, oc_lse, wa, wb, wc, wo, g, mod, layer):
    s, d = h.shape
    tm = 2 * MERGE_SUB_ROWS
    cw = C_HEADS_PER_GROUP * HEAD_DIM
    row = lambda w: pl.BlockSpec((tm, w), lambda i: (i, 0))
    res = lambda a: pl.BlockSpec((a.shape[0], tm // a.shape[0], cw), lambda i: (0, i, 0))
    full = lambda a: pl.BlockSpec(a.shape, lambda i: (0, 0), pipeline_mode=pl.Buffered(1))
    stacked = lambda a: pl.BlockSpec((None,) + a.shape[1:], lambda i: (layer, 0, 0), pipeline_mode=pl.Buffered(1))
    return pl.pallas_call(
        _merge_kernel,
        grid=(s // tm,),
        in_specs=[row(d), row(3 * d), row(oa.shape[1]), row(ob.shape[1])] + [res(a) for a in oc_lse]
        + [stacked(wa), stacked(wb), stacked(wc), stacked(wo), full(g), full(mod)],
        out_specs=row(d),
        out_shape=jax.ShapeDtypeStruct((s, d), jnp.float32),
        scratch_shapes=[pltpu.VMEM((4 * cw // LANES_V7X, tm, LANES_V7X), jnp.float32)],
        input_output_aliases={0: 0} if layer > 0 else {},
        compiler_params=_params(("arbitrary",)),
        name="merge_out_proj",
    )(h, z, oa, ob, *oc_lse, wa, wb, wc, wo, g, mod)


FFN_CHUNK = 1408


def _ffn_kernel(h_ref, g_pre_ref, g_post_ref, mod_ref, wg_ref, wu_ref, wd_ref, out_ref):
    x = h_ref[...]
    u = _rmsnorm(x, g_pre_ref[...]) * (1.0 + mod_ref[4:5, :]) + mod_ref[3:4, :]
    u = u.astype(jnp.bfloat16)
    y = jnp.zeros(x.shape, jnp.float32)
    for c0 in range(0, D_FF, FFN_CHUNK):
        gate = _dot(u, wg_ref[:, c0:c0 + FFN_CHUNK])
        up = _dot(u, wu_ref[:, c0:c0 + FFN_CHUNK])
        act = (gate * jax.nn.sigmoid(gate) * up).astype(jnp.bfloat16)
        y = y + _dot(act, wd_ref[c0:c0 + FFN_CHUNK, :])
    out_ref[...] = x + mod_ref[5:6, :] * _rmsnorm(y, g_post_ref[...])


def _ffn_call(h, g_pre, g_post, mod, wg, wu, wd, layer):
    s, d = h.shape
    tm = ROW_TILE
    row = pl.BlockSpec((tm, d), lambda i: (i, 0))
    full = lambda a: pl.BlockSpec(a.shape, lambda i: (0, 0), pipeline_mode=pl.Buffered(1))
    stacked = lambda a: pl.BlockSpec((None,) + a.shape[1:], lambda i: (layer, 0, 0), pipeline_mode=pl.Buffered(1))
    return pl.pallas_call(
        _ffn_kernel,
        grid=(s // tm,),
        in_specs=[row, full(g_pre), full(g_post), full(mod), stacked(wg), stacked(wu), stacked(wd)],
        out_specs=row,
        out_shape=jax.ShapeDtypeStruct((s, d), jnp.float32),
        input_output_aliases={0: 0},
        compiler_params=_params(("arbitrary",)),
        name="swiglu_ffn",
    )(h, g_pre, g_post, mod, wg, wu, wd)


def _permute_in_proj(w_in):
    hd = HEAD_DIM
    o_bq = 3 * A_HEADS * hd
    o_bkv = o_bq + B_HEADS * hd
    o_c = o_bkv + 2 * B_KV_HEADS * hd
    o_g = o_c + 3 * C_HEADS * hd
    cw = C_HEADS_PER_GROUP * hd
    sl = lambda a, b: w_in[:, :, a:b]
    parts = [sl(o_g, o_g + 3 * D_MODEL), sl(0, A_HEADS * hd) * (QK_SCALE * LOG2E), sl(A_HEADS * hd, o_bq)]
    parts += [sl(o_bq + hh * hd, o_bq + (hh + 1) * hd) * QK_SCALE for hh in B_HEAD_ORDER]
    parts += [sl(o_bkv, o_c)]
    for g in range(len(C_GROUPS)):
        parts += [sl(o_c + g * 3 * cw, o_c + g * 3 * cw + cw) * QK_SCALE,
                  sl(o_c + g * 3 * cw + cw, o_c + (g + 1) * 3 * cw)]
    out = jnp.concatenate([part.astype(jnp.bfloat16) for part in parts], axis=2)
    assert out.shape[2] == IN_WIDTH
    return out


def kernel(x, c, w_ada, b_ada, g_pre_mix, g_post_mix, w_in, sinks, w_br_a, w_br_b, w_br_c,
           w_out, g_pre_ffn, g_post_ffn, w_gate, w_up, w_down):
    bn, s, d = x.shape
    assert bn == 1 and d == D_MODEL and s % SEQ_MULTIPLE == 0
    bf = jnp.bfloat16
    hd = HEAD_DIM
    w_in_p = _permute_in_proj(w_in)
    w_br_b_p = jnp.concatenate([w_br_b[:, hh * hd:(hh + 1) * hd] for hh in B_HEAD_ORDER], axis=1).astype(bf)
    w_br_a_b, w_br_c_b, w_out_b = w_br_a.astype(bf), w_br_c.astype(bf), w_out.astype(bf)
    w_gate_b, w_up_b, w_down_b = w_gate.astype(bf), w_up.astype(bf), w_down.astype(bf)

    mod_all = _ada_call(c, w_ada, b_ada).reshape(DEPTH, 6, d)
    h = x.reshape(s, d)
    for l in range(DEPTH):
        mod = mod_all[l]
        z, zc1, zc2 = _in_proj_call(h, g_pre_mix[l].reshape(1, d), mod, w_in_p, l)
        oa = _moba_call(z)
        ob = _swa_call(z, sinks[l])
        oc_lse = (_dilated_call(z.reshape(1, s, Z_WIDTH), 0, Z_C0)
                  + _dilated_call(zc1, 1, 0) + _dilated_call(zc2, 2, 0))
        h = _merge_call(h, z, oa, ob, oc_lse, w_br_a_b, w_br_b_p, w_br_c_b, w_out_b,
                        g_post_mix[l].reshape(1, d), mod, l)
        h = _ffn_call(h, g_pre_ffn[l].reshape(1, d), g_post_ffn[l].reshape(1, d), mod,
                      w_gate_b, w_up_b, w_down_b, l)
    return h.reshape(bn, s, d)
```

```python
import numpy as np
import jax
import jax.numpy as jnp
from jax import lax
from jax.experimental import pallas as pl
from jax.experimental.pallas import tpu as pltpu

D_MODEL = 1024
DEPTH = 4
HEAD_DIM = 64
A_HEADS = 8
MOBA_BLOCK = 256
MOBA_TOPK = 3
B_HEADS = 8
B_KV_HEADS = 2
B_WINDOW = 128
C_GROUPS = ((128, 1), (512, 4), (2048, 16))
C_HEADS_PER_GROUP = 4
C_HEADS = len(C_GROUPS) * C_HEADS_PER_GROUP
BAND = 128
D_FF = 2816
N_ALIBI_HEADS = A_HEADS + B_HEADS + C_HEADS
SEQ_MULTIPLE = 2048
RMS_EPS = 1e-6

LANES_V7X = 128
BF16_SUBLANES_V7X = 16
VMEM_LIMIT_BYTES_V7X = 56 * 1024 * 1024

ROW_TILE = 512
C_WIDTH = 3 * C_HEADS_PER_GROUP * HEAD_DIM

Z_GATES = 0
Z_A = 3 * D_MODEL
Z_BQ = Z_A + 3 * A_HEADS * HEAD_DIM
Z_BK = Z_BQ + B_HEADS * HEAD_DIM
Z_BV = Z_BK + B_KV_HEADS * HEAD_DIM
Z_C0 = Z_BV + B_KV_HEADS * HEAD_DIM
Z_WIDTH = Z_C0 + C_WIDTH
IN_WIDTH = Z_WIDTH + (len(C_GROUPS) - 1) * C_WIDTH
B_HEAD_ORDER = (0, 4, 1, 5, 2, 6, 3, 7)

NEG_BIG = -1e30
QK_SCALE = HEAD_DIM ** -0.5
LOG2E = 1.4426950408889634


def _alibi_slopes():
    n = N_ALIBI_HEADS
    return [float(2.0 ** (-8.0 * (i + 1) / n)) for i in range(n)]


_SLOPES = _alibi_slopes()
SLOPES_B = _SLOPES[:B_HEADS]
SLOPES_C = _SLOPES[B_HEADS:B_HEADS + C_HEADS]
SLOPES_A = _SLOPES[B_HEADS + C_HEADS:]


def _dot(a, b):
    return jnp.dot(a, b, preferred_element_type=jnp.float32)


def _dot_nt(a, b):
    return lax.dot_general(a, b, (((1,), (1,)), ((), ())), preferred_element_type=jnp.float32)


def _params(semantics):
    return pltpu.CompilerParams(dimension_semantics=semantics, vmem_limit_bytes=VMEM_LIMIT_BYTES_V7X)


def _rmsnorm(x, g):
    return x * lax.rsqrt(jnp.mean(x * x, axis=-1, keepdims=True) + RMS_EPS) * g


def _ada_kernel(c_ref, w_ref, b_ref, o_ref):
    c = c_ref[...]
    sc = c * jax.nn.sigmoid(c)
    o_ref[...] = jnp.sum(w_ref[...] * sc, axis=0, keepdims=True) + b_ref[...]


def _ada_call(c, w_ada, b_ada):
    depth, d, n = w_ada.shape
    tn = 1536
    return pl.pallas_call(
        _ada_kernel,
        grid=(depth, n // tn),
        in_specs=[
            pl.BlockSpec((d, 1), lambda l, j: (0, 0)),
            pl.BlockSpec((None, d, tn), lambda l, j: (l, 0, j)),
            pl.BlockSpec((None, 1, tn), lambda l, j: (l, 0, j)),
        ],
        out_specs=pl.BlockSpec((None, 1, tn), lambda l, j: (l, 0, j)),
        out_shape=jax.ShapeDtypeStruct((depth, 1, n), jnp.float32),
        compiler_params=_params(("arbitrary", "arbitrary")),
        name="adaln_mod",
    )(c.reshape(d, 1), w_ada, b_ada.reshape(depth, 1, n))


IN_PROJ_CHUNK = 512


def _in_proj_kernel(h_ref, g_ref, mod_ref, w_ref, z_ref, zc1_ref, zc2_ref, tmp_ref):
    x = h_ref[...]
    u = _rmsnorm(x, g_ref[...]) * (1.0 + mod_ref[1:2, :]) + mod_ref[0:1, :]
    u = u.astype(jnp.bfloat16)
    for c0 in range(0, Z_WIDTH, IN_PROJ_CHUNK):
        res = _dot(u, w_ref[:, c0:c0 + IN_PROJ_CHUNK])
        if c0 < Z_A:
            res = jax.nn.sigmoid(res)
        z_ref[:, c0:c0 + IN_PROJ_CHUNK] = res.astype(z_ref.dtype)
    for g, out_ref in ((1, zc1_ref), (2, zc2_ref)):
        c0 = Z_WIDTH + (g - 1) * C_WIDTH
        res = _dot(u, w_ref[:, c0:c0 + C_WIDTH])
        dil = C_GROUPS[g][1]
        rows = x.shape[0] // dil
        ln = LANES_V7X
        for c in range(C_WIDTH // ln):
            tmp_ref[c] = res[:, c * ln:(c + 1) * ln]
        for r in range(dil):
            for c in range(C_WIDTH // ln):
                out_ref[r, :, c * ln:(c + 1) * ln] = tmp_ref[c, pl.ds(r, rows, stride=dil), :].astype(out_ref.dtype)


def _in_proj_call(h, g, mod, w, layer):
    s, d = h.shape
    tm = ROW_TILE
    d1, d2 = C_GROUPS[1][1], C_GROUPS[2][1]
    assert tm % (d2 * BF16_SUBLANES_V7X) == 0
    return pl.pallas_call(
        _in_proj_kernel,
        grid=(s // tm,),
        in_specs=[
            pl.BlockSpec((tm, d), lambda i: (i, 0)),
            pl.BlockSpec((1, d), lambda i: (0, 0)),
            pl.BlockSpec((6, d), lambda i: (0, 0)),
            pl.BlockSpec((None, d, IN_WIDTH), lambda i: (layer, 0, 0), pipeline_mode=pl.Buffered(1)),
        ],
        out_specs=[
            pl.BlockSpec((tm, Z_WIDTH), lambda i: (i, 0)),
            pl.BlockSpec((d1, tm // d1, C_WIDTH), lambda i: (0, i, 0)),
            pl.BlockSpec((d2, tm // d2, C_WIDTH), lambda i: (0, i, 0)),
        ],
        out_shape=[
            jax.ShapeDtypeStruct((s, Z_WIDTH), jnp.bfloat16),
            jax.ShapeDtypeStruct((d1, s // d1, C_WIDTH), jnp.bfloat16),
            jax.ShapeDtypeStruct((d2, s // d2, C_WIDTH), jnp.bfloat16),
        ],
        scratch_shapes=[pltpu.VMEM((C_WIDTH // LANES_V7X, tm, LANES_V7X), jnp.float32)],
        compiler_params=_params(("arbitrary",)),
        name="in_proj",
    )(h, g, mod, w)


SUBLANES_V7X = 8
MOBA_V_ROWS = HEAD_DIM + BF16_SUBLANES_V7X
MOBA_GROUPS = MOBA_BLOCK // SUBLANES_V7X
MOBA_UNROLL_LOG2 = 1
MOBA_UNROLL = 1 << MOBA_UNROLL_LOG2


def _all_sublanes_max(x):
    for shift in (4, 2, 1):
        x = jnp.maximum(x, pltpu.roll(x, shift, axis=0))
    return x


def _moba_lanes():
    lane = lax.broadcasted_iota(jnp.int32, (MOBA_BLOCK, LANES_V7X), 1)
    own = [lane < HEAD_DIM, lane >= HEAD_DIM]
    spare = [HEAD_DIM, 0]
    bias_lanes = [(lane == spare[h]) | (lane == spare[h] + 1) for h in (0, 1)]
    return lane, own, spare, bias_lanes


def _moba_kernel(slopes_ref, q_ref, k_ref, v_ref, o_ref, vt_ref, km_ref, ka_ref, sel_ref, s_ref, e_ref, st_ref, acc_ref):
    p = pl.program_id(0)
    nblk = vt_ref.shape[0]
    blk = MOBA_BLOCK
    half = HEAD_DIM
    _, own, _, bias_lanes = _moba_lanes()
    ones = jnp.ones((BF16_SUBLANES_V7X, blk), jnp.bfloat16)

    def prepare(j, carry):
        rows = pl.ds(pl.multiple_of(j * blk, blk), blk)
        vt = v_ref[rows, :].astype(jnp.float32).T.astype(jnp.bfloat16)
        k2 = k_ref[rows, :]
        km_ref[pl.ds(j, 1), :] = jnp.mean(k2.astype(jnp.float32), axis=0, keepdims=True)
        k2f = k2.astype(jnp.float32)
        pos = lax.broadcasted_iota(jnp.int32, k2.shape, 0).astype(jnp.float32)
        for h in (0, 1):
            vt_ref[j, h, 0:half, :] = vt[h * half:(h + 1) * half, :]
            vt_ref[j, h, half:, :] = ones
            ka = jnp.where(own[h], k2f, jnp.where(bias_lanes[h], pos, 0.0))
            ka_ref[h, rows, :] = ka.astype(jnp.bfloat16)
        return carry

    lax.fori_loop(0, nblk, prepare, 0)
    _moba_select(0, q_ref, km_ref, sel_ref)

    def tile(i, carry):
        _moba_tile(i, p, slopes_ref, q_ref, o_ref, vt_ref, km_ref, ka_ref, sel_ref, s_ref, e_ref, st_ref, acc_ref)
        return carry

    lax.fori_loop(0, nblk, tile, 0)


def _moba_select(i, q_ref, km_ref, sel_ref):
    nblk, blk = sel_ref.shape[1], MOBA_BLOCK
    _, own, _, _ = _moba_lanes()
    q2 = q_ref[pl.ds(pl.multiple_of(i * blk, blk), blk), :]
    blk_id = lax.broadcasted_iota(jnp.int32, (nblk, blk), 0)
    km = km_ref[...]
    km_hi = km.astype(jnp.bfloat16)
    km_lo = (km - km_hi.astype(jnp.float32)).astype(jnp.bfloat16)
    for h in (0, 1):
        qz = jnp.where(own[h], q2, jnp.zeros_like(q2))
        gate = _dot_nt(km_hi, qz) + _dot_nt(km_lo, qz)
        gate = jnp.where(blk_id < i, gate, -jnp.inf)
        sel = jnp.full((nblk, blk), NEG_BIG, jnp.float32)
        for _ in range(MOBA_TOPK):
            mx = jnp.max(gate, axis=0, keepdims=True)
            cand = (gate == mx) & (mx > -jnp.inf)
            idx = jnp.min(jnp.where(cand, blk_id, nblk), axis=0, keepdims=True)
            chosen = blk_id == idx
            sel = jnp.where(chosen, 0.0, sel)
            gate = jnp.where(chosen, -jnp.inf, gate)
        sel_ref[h] = sel


def _moba_tile(i, p, slopes_ref, q_ref, o_ref, vt_ref, km_ref, ka_ref, sel_ref, s_ref, e_ref, st_ref, acc_ref):
    nblk = vt_ref.shape[0]
    blk = MOBA_BLOCK
    half = HEAD_DIM
    heads = (0, 1)
    sub = SUBLANES_V7X
    tile3 = (MOBA_GROUPS, sub, blk)
    key_pos = lax.broadcasted_iota(jnp.int32, tile3, 0) * sub + lax.broadcasted_iota(jnp.int32, tile3, 1)
    qry_pos = lax.broadcasted_iota(jnp.int32, tile3, 2)
    lane, own, spare, _ = _moba_lanes()
    tile_rows = pl.ds(pl.multiple_of(i * blk, blk), blk)

    q2 = q_ref[tile_rows, :]
    q2f = q2.astype(jnp.float32)
    qh = []
    for h in heads:
        hi = slopes_ref[A_HEADS + 2 * p + h]
        lo = slopes_ref[2 * A_HEADS + 2 * p + h]
        extra = jnp.where(lane == spare[h], hi, jnp.where(lane == spare[h] + 1, lo, 0.0))
        qh.append(jnp.where(own[h], q2f, extra).astype(jnp.bfloat16))

    unroll = MOBA_UNROLL
    acc_groups = MOBA_V_ROWS // sub

    def past_block(n):
        return jnp.clip(n - 1, 0, nblk - 1)

    def issue_scores(blocks):
        out = []
        for b in blocks:
            rows = pl.ds(pl.multiple_of(b * blk, blk), blk)
            out.append([_dot_nt(ka_ref[h, rows, :], qh[h]).reshape(tile3) for h in heads])
        return out

    def stage_scores(raws, slot, items, own_first):
        for u, n in enumerate(items):
            for h in heads:
                s = raws[u][h]
                if own_first and u == 0:
                    s = jnp.where(qry_pos >= key_pos, s, NEG_BIG)
                    row = jnp.zeros((sub, blk), jnp.float32)
                else:
                    b = past_block(n)
                    gap = ((i - b) * blk).astype(jnp.float32)
                    row = sel_ref[h, pl.ds(b, 1), :] - slopes_ref[2 * p + h] * gap
                    row = jnp.broadcast_to(jnp.where(n <= i, row, NEG_BIG), (sub, blk))
                s_ref[slot, u, h] = s
                st_ref[slot, u, h, 0] = _all_sublanes_max(jnp.max(s, axis=0)) + row
                st_ref[slot, u, h, 1] = row

    def exponentiate(group_slot, ms):
        new_ms, alphas = [], []
        for h in heads:
            m_new = ms[h]
            for u in range(unroll):
                m_new = jnp.maximum(m_new, st_ref[group_slot, u, h, 0])
            alphas.append(jnp.exp2(ms[h] - m_new))
            new_ms.append(m_new)
            for u in range(unroll):
                e = jnp.exp2(s_ref[group_slot, u, h] - (m_new - st_ref[group_slot, u, h, 1])[None])
                e_ref[group_slot, u, h] = e.reshape(blk, blk).astype(jnp.bfloat16)
        return new_ms, alphas

    def accumulate(group, slot, alphas):
        for h in heads:
            acc = alphas[h][None] * acc_ref[h]
            for u in range(unroll):
                n = group * unroll + u
                vblock = jnp.where(n == 0, i, past_block(n))
                acc = acc + _dot(vt_ref[vblock, h], e_ref[slot, u, h]).reshape(acc_groups, sub, blk)
            acc_ref[h] = acc

    def score_group(group, slot, own_first):
        items = [group * unroll + u for u in range(unroll)]
        blocks = [i if (own_first and u == 0) else past_block(n) for u, n in enumerate(items)]
        raws = issue_scores(blocks)
        return lambda: stage_scores(raws, slot, items, own_first)

    def tick(t, slot, ms, alphas):
        accumulate(t - 2, slot, alphas)
        finish_scores = score_group(t, slot, False)
        ms, alphas = exponentiate(1 - slot, ms)
        finish_scores()
        return ms, alphas

    pairs = jnp.maximum(lax.shift_right_logical(i + 2 * unroll, MOBA_UNROLL_LOG2 + 1), 1)
    ms = [jnp.full((sub, blk), -jnp.inf, jnp.float32) for _ in heads]
    acc_ref[...] = jnp.zeros(acc_ref.shape, jnp.float32)
    score_group(0, 0, True)()
    finish_scores = score_group(1, 1, False)
    ms, alphas = exponentiate(0, ms)
    finish_scores()

    def two_ticks(k, carry):
        ms, alphas = tick(2 * k, 0, list(carry[0:2]), list(carry[2:4]))
        ms, alphas = tick(2 * k + 1, 1, ms, alphas)
        return tuple(ms + alphas)

    carry = lax.fori_loop(1, pairs, two_ticks, tuple(ms + alphas))
    ms, alphas = list(carry[0:2]), list(carry[2:4])
    accumulate(2 * pairs - 2, 0, alphas)
    ms, alphas = exponentiate(1, ms)
    _moba_select(jnp.minimum(i + 1, nblk - 1), q_ref, km_ref, sel_ref)
    accumulate(2 * pairs - 1, 1, alphas)
    accs = [acc_ref[h] for h in heads]
    o_t = jnp.concatenate([(accs[h][0:half // sub] / accs[h][half // sub][None]).reshape(half, blk)
                           for h in heads], axis=0)
    o_ref[tile_rows, :] = o_t.T.astype(o_ref.dtype)


def _moba_slopes():
    full = np.asarray([sl * LOG2E for sl in SLOPES_A], np.float32)
    hi = full.astype(jnp.bfloat16).astype(np.float32)
    lo = (full - hi).astype(jnp.bfloat16).astype(np.float32)
    return jnp.asarray(np.concatenate([full, hi, lo]))


def _moba_call(z):
    s = z.shape[0]
    nblk = s // MOBA_BLOCK
    pairs = A_HEADS // 2
    ln = LANES_V7X
    qb, kb, vb = Z_A // ln, Z_A // ln + pairs, Z_A // ln + 2 * pairs
    return pl.pallas_call(
        _moba_kernel,
        grid=(pairs,),
        in_specs=[
            pl.BlockSpec(memory_space=pltpu.SMEM),
            pl.BlockSpec((s, ln), lambda p: (0, qb + p)),
            pl.BlockSpec((s, ln), lambda p: (0, kb + p), pipeline_mode=pl.Buffered(1)),
            pl.BlockSpec((s, ln), lambda p: (0, vb + p), pipeline_mode=pl.Buffered(1)),
        ],
        out_specs=pl.BlockSpec((s, ln), lambda p: (0, p)),
        out_shape=jax.ShapeDtypeStruct((s, A_HEADS * HEAD_DIM), jnp.bfloat16),
        scratch_shapes=[
            pltpu.VMEM((nblk, 2, MOBA_V_ROWS, MOBA_BLOCK), jnp.bfloat16),
            pltpu.VMEM((nblk, ln), jnp.float32),
            pltpu.VMEM((2, s, ln), jnp.bfloat16),
            pltpu.VMEM((2, nblk, MOBA_BLOCK), jnp.float32),
            pltpu.VMEM((2, MOBA_UNROLL, 2, MOBA_GROUPS, SUBLANES_V7X, MOBA_BLOCK), jnp.float32),
            pltpu.VMEM((2, MOBA_UNROLL, 2, MOBA_BLOCK, MOBA_BLOCK), jnp.bfloat16),
            pltpu.VMEM((2, MOBA_UNROLL, 2, 2, SUBLANES_V7X, MOBA_BLOCK), jnp.float32),
            pltpu.VMEM((2, MOBA_V_ROWS // SUBLANES_V7X, SUBLANES_V7X, MOBA_BLOCK), jnp.float32),
        ],
        compiler_params=_params(("arbitrary",)),
        name="moba_attn",
    )(_moba_slopes(), z, z, z)


def _blocks_per_step(nb):
    return max(c for c in (4, 2, 1) if nb % c == 0)


def _band_penalty_table(slope_dils, max_steps):
    steps = np.arange(BAND)[:, None] + BAND - np.arange(2 * BAND)[None, :]
    inside = (steps >= 0) & (steps <= max_steps)
    table = [np.where(inside, np.float32(sd) * steps.astype(np.float32), np.float32(-NEG_BIG)) for sd in slope_dils]
    return jnp.asarray(np.stack(table).astype(np.float32))


def _band_no_prev(first_step):
    kj = lax.broadcasted_iota(jnp.int32, (BAND, 2 * BAND), 1)
    return jnp.logical_and(first_step, kj < BAND)


def _band_units(units, lookahead):
    raws = {j: _dot_nt(units[j][0], units[j][1]) for j in range(min(lookahead, len(units)))}
    outs, lses = [], []
    for j, (_, _, vv, penalty, drop, sink) in enumerate(units):
        ahead = j + lookahead
        if ahead < len(units):
            raws[ahead] = _dot_nt(units[ahead][0], units[ahead][1])
        s = raws.pop(j) - penalty
        if drop is not None:
            s = jnp.where(drop, NEG_BIG, s)
        m = jnp.max(s, axis=1, keepdims=True)
        if sink is not None:
            m = jnp.maximum(m, sink)
        e = jnp.exp(s - m)
        denom = jnp.sum(e, axis=1, keepdims=True)
        if sink is not None:
            denom = denom + jnp.exp(sink - m)
        outs.append(_dot(e.astype(jnp.bfloat16), vv) / denom)
        lses.append(m + jnp.log(denom))
    return outs, lses


def _swa_kernel(sinks_ref, bias_ref, q_ref, kp_ref, ko_ref, vp_ref, vo_ref, o_ref):
    n = pl.program_id(0)
    nbq = q_ref.shape[0] // BAND
    no_prev = _band_no_prev(n == 0)
    k_all = jnp.concatenate([kp_ref[...], ko_ref[...]], axis=0)
    v_all = jnp.concatenate([vp_ref[...], vo_ref[...]], axis=0)
    ln = LANES_V7X
    lane = lax.broadcasted_iota(jnp.int32, (BAND, ln), 1)
    low = lane < HEAD_DIM
    units = []
    for qb in range(nbq):
        kk = k_all[qb * BAND:(qb + 2) * BAND]
        vv = v_all[qb * BAND:(qb + 2) * BAND]
        drop = no_prev if qb == 0 else None
        for b in range(B_HEADS // 2):
            q2 = q_ref[qb * BAND:(qb + 1) * BAND, b * ln:(b + 1) * ln]
            for c in range(2):
                head = B_HEAD_ORDER[2 * b + c]
                qm = jnp.where(low if c == 0 else jnp.logical_not(low), q2, jnp.zeros_like(q2))
                units.append((qm, kk, vv, bias_ref[2 * b + c], drop, sinks_ref[head]))
    outs, _ = _band_units(units, B_HEADS)
    for qb in range(nbq):
        for b in range(B_HEADS // 2):
            j = qb * B_HEADS + 2 * b
            o_ref[qb * BAND:(qb + 1) * BAND, b * ln:(b + 1) * ln] = (
                jnp.where(low, outs[j], outs[j + 1]).astype(o_ref.dtype))


def _swa_call(z, sinks):
    s = z.shape[0]
    nbq = _blocks_per_step(s // BAND)
    rows = nbq * BAND
    ln = LANES_V7X
    qw = B_HEADS * HEAD_DIM
    prev = lambda n: jnp.maximum(n * nbq - 1, 0)
    bias = _band_penalty_table([SLOPES_B[head] for head in B_HEAD_ORDER], B_WINDOW - 1)
    return pl.pallas_call(
        _swa_kernel,
        grid=(s // rows,),
        in_specs=[
            pl.BlockSpec(memory_space=pltpu.SMEM),
            pl.BlockSpec(bias.shape, lambda n: (0, 0, 0), pipeline_mode=pl.Buffered(1)),
            pl.BlockSpec((rows, qw), lambda n: (n, Z_BQ // qw)),
            pl.BlockSpec((BAND, ln), lambda n: (prev(n), Z_BK // ln)),
            pl.BlockSpec((rows, ln), lambda n: (n, Z_BK // ln)),
            pl.BlockSpec((BAND, ln), lambda n: (prev(n), Z_BV // ln)),
            pl.BlockSpec((rows, ln), lambda n: (n, Z_BV // ln)),
        ],
        out_specs=pl.BlockSpec((rows, qw), lambda n: (n, 0)),
        out_shape=jax.ShapeDtypeStruct((s, qw), jnp.bfloat16),
        compiler_params=_params(("arbitrary",)),
        name="swa_attn",
    )(sinks, bias, z, z, z, z, z)


def _dilated_kernel(bias_ref, q_ref, kp_ref, ko_ref, vp_ref, vo_ref, o_ref, lse_ref):
    n = pl.program_id(1)
    nbq = q_ref.shape[0] // BAND
    no_prev = _band_no_prev(n == 0)
    k_all = jnp.concatenate([kp_ref[...], ko_ref[...]], axis=0)
    v_all = jnp.concatenate([vp_ref[...], vo_ref[...]], axis=0)
    cw = q_ref.shape[1]
    lane = lax.broadcasted_iota(jnp.int32, (BAND, cw), 1)
    heads = range(C_HEADS_PER_GROUP)
    mine = [(lane >= h * HEAD_DIM) & (lane < (h + 1) * HEAD_DIM) for h in heads]
    biases = [bias_ref[h] for h in heads]
    units = []
    for qb in range(nbq):
        q4 = q_ref[qb * BAND:(qb + 1) * BAND, :]
        kk = k_all[qb * BAND:(qb + 2) * BAND]
        vv = v_all[qb * BAND:(qb + 2) * BAND]
        drop = no_prev if qb == 0 else None
        for h in heads:
            units.append((jnp.where(mine[h], q4, jnp.zeros_like(q4)), kk, vv, biases[h], drop, None))
    outs, lses = _band_units(units, C_HEADS_PER_GROUP)
    for qb in range(nbq):
        o_all = jnp.zeros((BAND, cw), jnp.float32)
        lse_all = jnp.zeros((BAND, cw), jnp.float32)
        for h in heads:
            j = qb * C_HEADS_PER_GROUP + h
            o_all = jnp.where(mine[h], outs[j], o_all)
            lse_all = jnp.where(mine[h], lses[j], lse_all)
        o_ref[qb * BAND:(qb + 1) * BAND, :] = o_all
        lse_ref[qb * BAND:(qb + 1) * BAND, :] = lse_all


def _dilated_call(zc, group, col0):
    window, dil = C_GROUPS[group]
    assert zc.shape[0] == dil
    length = zc.shape[1]
    nbq = _blocks_per_step(length // BAND)
    rows = nbq * BAND
    cw = C_HEADS_PER_GROUP * HEAD_DIM
    base = col0 // cw
    slopes = [SLOPES_C[group * C_HEADS_PER_GROUP + h] * dil for h in range(C_HEADS_PER_GROUP)]
    bias = _band_penalty_table(slopes, window // dil)
    own = lambda col: pl.BlockSpec((None, rows, cw), lambda r, n: (r, n, base + col))
    prev = lambda col: pl.BlockSpec((None, BAND, cw), lambda r, n: (r, jnp.maximum(n * nbq - 1, 0), base + col))
    out_blk = pl.BlockSpec((None, rows, cw), lambda r, n: (r, n, 0))
    return pl.pallas_call(
        _dilated_kernel,
        grid=(dil, length // rows),
        in_specs=[pl.BlockSpec(bias.shape, lambda r, n: (0, 0, 0), pipeline_mode=pl.Buffered(1)),
                  own(0), prev(1), own(1), prev(2), own(2)],
        out_specs=[out_blk, out_blk],
        out_shape=[jax.ShapeDtypeStruct((dil, length, cw), jnp.float32)] * 2,
        compiler_params=_params(("arbitrary", "arbitrary")),
        name=f"dilated_attn_g{group}",
    )(bias, zc, zc, zc, zc, zc)


def _merge_kernel(h_ref, zg_ref, oa_ref, ob_ref, o0_ref, l0_ref, o1_ref, l1_ref, o2_ref, l2_ref,
                  wa_ref, wb_ref, wc_ref, wo_ref, g_ref, mod_ref, out_ref, nat_ref):
    tm = h_ref.shape[0]

    def natural(ref, slot):
        dil = ref.shape[0]
        if dil == 1:
            return ref[0]
        ln = LANES_V7X
        chunks = ref.shape[2] // ln
        for r in range(dil):
            for c in range(chunks):
                nat_ref[slot * chunks + c, pl.ds(r, tm // dil, stride=dil), :] = ref[r, :, c * ln:(c + 1) * ln]
        return jnp.concatenate([nat_ref[slot * chunks + c] for c in range(chunks)], axis=1)

    l0, l1, l2 = natural(l0_ref, 0), natural(l1_ref, 0), natural(l2_ref, 1)
    o0, o1, o2 = natural(o0_ref, 0), natural(o1_ref, 2), natural(o2_ref, 3)
    mx = jnp.maximum(jnp.maximum(l0, l1), l2)
    e0, e1, e2 = jnp.exp(l0 - mx), jnp.exp(l1 - mx), jnp.exp(l2 - mx)
    oc = ((e0 * o0 + e1 * o1 + e2 * o2) / (e0 + e1 + e2)).astype(jnp.bfloat16)
    d = D_MODEL
    subs = range(0, tm, MERGE_SUB_ROWS)
    branches = [(_dot(oa_ref[pl.ds(r0, MERGE_SUB_ROWS), :], wa_ref[...]),
                 _dot(ob_ref[pl.ds(r0, MERGE_SUB_ROWS), :], wb_ref[...]),
                 _dot(oc[r0:r0 + MERGE_SUB_ROWS], wc_ref[...])) for r0 in subs]
    ys = []
    for r0, (ya, yb, yc) in zip(subs, branches):
        gates = zg_ref[pl.ds(r0, MERGE_SUB_ROWS), :].astype(jnp.float32)
        merged = gates[:, 0:d] * ya + gates[:, d:2 * d] * yb + gates[:, 2 * d:3 * d] * yc
        ys.append(_dot(merged.astype(jnp.bfloat16), wo_ref[...]))
    for r0, y in zip(subs, ys):
        rs = pl.ds(r0, MERGE_SUB_ROWS)
        out_ref[rs, :] = h_ref[rs, :] + mod_ref[2:3, :] * _rmsnorm(y, g_ref[...])


MERGE_SUB_ROWS = 256


def _merge_call(h, z, oa, ob, oc_lse, wa, wb, wc, wo, g, mod, layer):
    s, d = h.shape
    tm = 2 * MERGE_SUB_ROWS
    cw = C_HEADS_PER_GROUP * HEAD_DIM
    row = lambda w: pl.BlockSpec((tm, w), lambda i: (i, 0))
    res = lambda a: pl.BlockSpec((a.shape[0], tm // a.shape[0], cw), lambda i: (0, i, 0))
    full = lambda a: pl.BlockSpec(a.shape, lambda i: (0, 0), pipeline_mode=pl.Buffered(1))
    stacked = lambda a: pl.BlockSpec((None,) + a.shape[1:], lambda i: (layer, 0, 0), pipeline_mode=pl.Buffered(1))
    return pl.pallas_call(
        _merge_kernel,
        grid=(s // tm,),
        in_specs=[row(d), row(3 * d), row(oa.shape[1]), row(ob.shape[1])] + [res(a) for a in oc_lse]
        + [stacked(wa), stacked(wb), stacked(wc), stacked(wo), full(g), full(mod)],
        out_specs=row(d),
        out_shape=jax.ShapeDtypeStruct((s, d), jnp.float32),
        scratch_shapes=[pltpu.VMEM((4 * cw // LANES_V7X, tm, LANES_V7X), jnp.float32)],
        input_output_aliases={0: 0} if layer > 0 else {},
        compiler_params=_params(("arbitrary",)),
        name="merge_out_proj",
    )(h, z, oa, ob, *oc_lse, wa, wb, wc, wo, g, mod)


FFN_CHUNK = 1408


def _ffn_kernel(h_ref, g_pre_ref, g_post_ref, mod_ref, wg_ref, wu_ref, wd_ref, out_ref):
    x = h_ref[...]
    u = _rmsnorm(x, g_pre_ref[...]) * (1.0 + mod_ref[4:5, :]) + mod_ref[3:4, :]
    u = u.astype(jnp.bfloat16)
    y = jnp.zeros(x.shape, jnp.float32)
    for c0 in range(0, D_FF, FFN_CHUNK):
        gate = _dot(u, wg_ref[:, c0:c0 + FFN_CHUNK])
        up = _dot(u, wu_ref[:, c0:c0 + FFN_CHUNK])
        act = (gate * jax.nn.sigmoid(gate) * up).astype(jnp.bfloat16)
        y = y + _dot(act, wd_ref[c0:c0 + FFN_CHUNK, :])
    out_ref[...] = x + mod_ref[5:6, :] * _rmsnorm(y, g_post_ref[...])


def _ffn_call(h, g_pre, g_post, mod, wg, wu, wd, layer):
    s, d = h.shape
    tm = ROW_TILE
    row = pl.BlockSpec((tm, d), lambda i: (i, 0))
    full = lambda a: pl.BlockSpec(a.shape, lambda i: (0, 0), pipeline_mode=pl.Buffered(1))
    stacked = lambda a: pl.BlockSpec((None,) + a.shape[1:], lambda i: (layer, 0, 0), pipeline_mode=pl.Buffered(1))
    return pl.pallas_call(
        _ffn_kernel,
        grid=(s // tm,),
        in_specs=[row, full(g_pre), full(g_post), full(mod), stacked(wg), stacked(wu), stacked(wd)],
        out_specs=row,
        out_shape=jax.ShapeDtypeStruct((s, d), jnp.float32),
        input_output_aliases={0: 0},
        compiler_params=_params(("arbitrary",)),
        name="swiglu_ffn",
    )(h, g_pre, g_post, mod, wg, wu, wd)


def _permute_in_proj(w_in):
    hd = HEAD_DIM
    o_bq = 3 * A_HEADS * hd
    o_bkv = o_bq + B_HEADS * hd
    o_c = o_bkv + 2 * B_KV_HEADS * hd
    o_g = o_c + 3 * C_HEADS * hd
    cw = C_HEADS_PER_GROUP * hd
    sl = lambda a, b: w_in[:, :, a:b]
    parts = [sl(o_g, o_g + 3 * D_MODEL), sl(0, A_HEADS * hd) * (QK_SCALE * LOG2E), sl(A_HEADS * hd, o_bq)]
    parts += [sl(o_bq + hh * hd, o_bq + (hh + 1) * hd) * QK_SCALE for hh in B_HEAD_ORDER]
    parts += [sl(o_bkv, o_c)]
    for g in range(len(C_GROUPS)):
        parts += [sl(o_c + g * 3 * cw, o_c + g * 3 * cw + cw) * QK_SCALE,
                  sl(o_c + g * 3 * cw + cw, o_c + (g + 1) * 3 * cw)]
    out = jnp.concatenate([part.astype(jnp.bfloat16) for part in parts], axis=2)
    assert out.shape[2] == IN_WIDTH
    return out


def kernel(x, c, w_ada, b_ada, g_pre_mix, g_post_mix, w_in, sinks, w_br_a, w_br_b, w_br_c,
           w_out, g_pre_ffn, g_post_ffn, w_gate, w_up, w_down):
    bn, s, d = x.shape
    assert bn == 1 and d == D_MODEL and s % SEQ_MULTIPLE == 0
    bf = jnp.bfloat16
    hd = HEAD_DIM
    w_in_p = _permute_in_proj(w_in)
    w_br_b_p = jnp.concatenate([w_br_b[:, hh * hd:(hh + 1) * hd] for hh in B_HEAD_ORDER], axis=1).astype(bf)
    w_br_a_b, w_br_c_b, w_out_b = w_br_a.astype(bf), w_br_c.astype(bf), w_out.astype(bf)
    w_gate_b, w_up_b, w_down_b = w_gate.astype(bf), w_up.astype(bf), w_down.astype(bf)

    mod_all = _ada_call(c, w_ada, b_ada).reshape(DEPTH, 6, d)
    h = x.reshape(s, d)
    for l in range(DEPTH):
        mod = mod_all[l]
        z, zc1, zc2 = _in_proj_call(h, g_pre_mix[l].reshape(1, d), mod, w_in_p, l)
        oa = _moba_call(z)
        ob = _swa_call(z, sinks[l])
        oc_lse = (_dilated_call(z.reshape(1, s, Z_WIDTH), 0, Z_C0)
                  + _dilated_call(zc1, 1, 0) + _dilated_call(zc2, 2, 0))
        h = _merge_call(h, z, oa, ob, oc_lse, w_br_a_b, w_br_b_p, w_br_c_b, w_out_b,
                        g_post_mix[l].reshape(1, d), mod, l)
        h = _ffn_call(h, g_pre_ffn[l].reshape(1, d), g_post_ffn[l].reshape(1, d), mod,
                      w_gate_b, w_up_b, w_down_b, l)
    return h.reshape(bn, s, d)
```

```python
import numpy as np
import jax
import jax.numpy as jnp
from jax import lax
from jax.experimental import pallas as pl
from jax.experimental.pallas import tpu as pltpu

D_MODEL = 1024
DEPTH = 4
HEAD_DIM = 64
A_HEADS = 8
MOBA_BLOCK = 256
MOBA_TOPK = 3
B_HEADS = 8
B_KV_HEADS = 2
B_WINDOW = 128
C_GROUPS = ((128, 1), (512, 4), (2048, 16))
C_HEADS_PER_GROUP = 4
C_HEADS = len(C_GROUPS) * C_HEADS_PER_GROUP
BAND = 128
D_FF = 2816
N_ALIBI_HEADS = A_HEADS + B_HEADS + C_HEADS
SEQ_MULTIPLE = 2048
RMS_EPS = 1e-6

LANES_V7X = 128
BF16_SUBLANES_V7X = 16
VMEM_LIMIT_BYTES_V7X = 56 * 1024 * 1024

ROW_TILE = 512
C_WIDTH = 3 * C_HEADS_PER_GROUP * HEAD_DIM

Z_GATES = 0
Z_A = 3 * D_MODEL
Z_BQ = Z_A + 3 * A_HEADS * HEAD_DIM
Z_BK = Z_BQ + B_HEADS * HEAD_DIM
Z_BV = Z_BK + B_KV_HEADS * HEAD_DIM
Z_C0 = Z_BV + B_KV_HEADS * HEAD_DIM
Z_WIDTH = Z_C0 + C_WIDTH
IN_WIDTH = Z_WIDTH + (len(C_GROUPS) - 1) * C_WIDTH
B_HEAD_ORDER = (0, 4, 1, 5, 2, 6, 3, 7)

NEG_BIG = -1e30
QK_SCALE = HEAD_DIM ** -0.5
LOG2E = 1.4426950408889634


def _alibi_slopes():
    n = N_ALIBI_HEADS
    return [float(2.0 ** (-8.0 * (i + 1) / n)) for i in range(n)]


_SLOPES = _alibi_slopes()
SLOPES_B = _SLOPES[:B_HEADS]
SLOPES_C = _SLOPES[B_HEADS:B_HEADS + C_HEADS]
SLOPES_A = _SLOPES[B_HEADS + C_HEADS:]


def _dot(a, b):
    return jnp.dot(a, b, preferred_element_type=jnp.float32)


def _dot_nt(a, b):
    return lax.dot_general(a, b, (((1,), (1,)), ((), ())), preferred_element_type=jnp.float32)


def _params(semantics):
    return pltpu.CompilerParams(dimension_semantics=semantics, vmem_limit_bytes=VMEM_LIMIT_BYTES_V7X)


def _rmsnorm(x, g):
    return x * lax.rsqrt(jnp.mean(x * x, axis=-1, keepdims=True) + RMS_EPS) * g


def _ada_kernel(c_ref, w_ref, b_ref, o_ref):
    c = c_ref[...]
    sc = c * jax.nn.sigmoid(c)
    o_ref[...] = jnp.sum(w_ref[...] * sc, axis=0, keepdims=True) + b_ref[...]


def _ada_call(c, w_ada, b_ada):
    depth, d, n = w_ada.shape
    tn = 1536
    return pl.pallas_call(
        _ada_kernel,
        grid=(depth, n // tn),
        in_specs=[
            pl.BlockSpec((d, 1), lambda l, j: (0, 0)),
            pl.BlockSpec((None, d, tn), lambda l, j: (l, 0, j)),
            pl.BlockSpec((None, 1, tn), lambda l, j: (l, 0, j)),
        ],
        out_specs=pl.BlockSpec((None, 1, tn), lambda l, j: (l, 0, j)),
        out_shape=jax.ShapeDtypeStruct((depth, 1, n), jnp.float32),
        compiler_params=_params(("arbitrary", "arbitrary")),
        name="adaln_mod",
    )(c.reshape(d, 1), w_ada, b_ada.reshape(depth, 1, n))


IN_PROJ_CHUNK = 512


def _in_proj_kernel(h_ref, g_ref, mod_ref, w_ref, z_ref, zc1_ref, zc2_ref, tmp_ref):
    x = h_ref[...]
    u = _rmsnorm(x, g_ref[...]) * (1.0 + mod_ref[1:2, :]) + mod_ref[0:1, :]
    u = u.astype(jnp.bfloat16)
    for c0 in range(0, Z_WIDTH, IN_PROJ_CHUNK):
        z_ref[:, c0:c0 + IN_PROJ_CHUNK] = _dot(u, w_ref[:, c0:c0 + IN_PROJ_CHUNK]).astype(z_ref.dtype)
    for g, out_ref in ((1, zc1_ref), (2, zc2_ref)):
        c0 = Z_WIDTH + (g - 1) * C_WIDTH
        res = _dot(u, w_ref[:, c0:c0 + C_WIDTH])
        dil = C_GROUPS[g][1]
        rows = x.shape[0] // dil
        ln = LANES_V7X
        for c in range(C_WIDTH // ln):
            tmp_ref[c] = res[:, c * ln:(c + 1) * ln]
        for r in range(dil):
            for c in range(C_WIDTH // ln):
                out_ref[r, :, c * ln:(c + 1) * ln] = tmp_ref[c, pl.ds(r, rows, stride=dil), :].astype(out_ref.dtype)


def _in_proj_call(h, g, mod, w, layer):
    s, d = h.shape
    tm = ROW_TILE
    d1, d2 = C_GROUPS[1][1], C_GROUPS[2][1]
    assert tm % (d2 * BF16_SUBLANES_V7X) == 0
    return pl.pallas_call(
        _in_proj_kernel,
        grid=(s // tm,),
        in_specs=[
            pl.BlockSpec((tm, d), lambda i: (i, 0)),
            pl.BlockSpec((1, d), lambda i: (0, 0)),
            pl.BlockSpec((6, d), lambda i: (0, 0)),
            pl.BlockSpec((None, d, IN_WIDTH), lambda i: (layer, 0, 0), pipeline_mode=pl.Buffered(1)),
        ],
        out_specs=[
            pl.BlockSpec((tm, Z_WIDTH), lambda i: (i, 0)),
            pl.BlockSpec((d1, tm // d1, C_WIDTH), lambda i: (0, i, 0)),
            pl.BlockSpec((d2, tm // d2, C_WIDTH), lambda i: (0, i, 0)),
        ],
        out_shape=[
            jax.ShapeDtypeStruct((s, Z_WIDTH), jnp.bfloat16),
            jax.ShapeDtypeStruct((d1, s // d1, C_WIDTH), jnp.bfloat16),
            jax.ShapeDtypeStruct((d2, s // d2, C_WIDTH), jnp.bfloat16),
        ],
        scratch_shapes=[pltpu.VMEM((C_WIDTH // LANES_V7X, tm, LANES_V7X), jnp.float32)],
        compiler_params=_params(("arbitrary",)),
        name="in_proj",
    )(h, g, mod, w)


SUBLANES_V7X = 8
MOBA_V_ROWS = HEAD_DIM + BF16_SUBLANES_V7X
MOBA_GROUPS = MOBA_BLOCK // SUBLANES_V7X
MOBA_UNROLL_LOG2 = 1
MOBA_UNROLL = 1 << MOBA_UNROLL_LOG2
MOBA_TILES = 2


def _all_sublanes_max(x):
    for shift in (4, 2, 1):
        x = jnp.maximum(x, pltpu.roll(x, shift, axis=0))
    return x


def _moba_lanes():
    lane = lax.broadcasted_iota(jnp.int32, (MOBA_BLOCK, LANES_V7X), 1)
    own = [lane < HEAD_DIM, lane >= HEAD_DIM]
    spare = [HEAD_DIM, 0]
    bias_lanes = [(lane == spare[h]) | (lane == spare[h] + 1) for h in (0, 1)]
    return lane, own, spare, bias_lanes


def _moba_kernel(slopes_ref, q_ref, k_ref, v_ref, o_ref, vt_ref, km_ref, ka_ref, sel_ref, s_ref, e_ref, st_ref, acc_ref):
    p = pl.program_id(0)
    nblk = vt_ref.shape[0]
    blk = MOBA_BLOCK
    half = HEAD_DIM
    _, own, _, bias_lanes = _moba_lanes()
    ones = jnp.ones((BF16_SUBLANES_V7X, blk), jnp.bfloat16)

    def prepare(j, carry):
        rows = pl.ds(pl.multiple_of(j * blk, blk), blk)
        vt = v_ref[rows, :].astype(jnp.float32).T.astype(jnp.bfloat16)
        k2 = k_ref[rows, :]
        km_ref[pl.ds(j, 1), :] = jnp.mean(k2.astype(jnp.float32), axis=0, keepdims=True)
        k2f = k2.astype(jnp.float32)
        pos = lax.broadcasted_iota(jnp.int32, k2.shape, 0).astype(jnp.float32)
        for h in (0, 1):
            vt_ref[j, h, 0:half, :] = vt[h * half:(h + 1) * half, :]
            vt_ref[j, h, half:, :] = ones
            ka = jnp.where(own[h], k2f, jnp.where(bias_lanes[h], pos, 0.0))
            ka_ref[h, rows, :] = ka.astype(jnp.bfloat16)
        return carry

    lax.fori_loop(0, nblk, prepare, 0)
    for w in range(MOBA_TILES):
        _moba_select(w, w, q_ref, km_ref, sel_ref)

    def tiles(t, carry):
        _moba_tiles(t, p, slopes_ref, q_ref, o_ref, vt_ref, km_ref, ka_ref, sel_ref, s_ref, e_ref, st_ref, acc_ref)
        return carry

    lax.fori_loop(0, nblk // MOBA_TILES, tiles, 0)


def _moba_select(i, w, q_ref, km_ref, sel_ref):
    nblk, blk = sel_ref.shape[1], MOBA_BLOCK
    _, own, _, _ = _moba_lanes()
    q2 = q_ref[pl.ds(pl.multiple_of(i * blk, blk), blk), :]
    blk_id = lax.broadcasted_iota(jnp.int32, (nblk, blk), 0)
    km = km_ref[...]
    km_hi = km.astype(jnp.bfloat16)
    km_lo = (km - km_hi.astype(jnp.float32)).astype(jnp.bfloat16)
    for h in (0, 1):
        qz = jnp.where(own[h], q2, jnp.zeros_like(q2))
        gate = _dot_nt(km_hi, qz) + _dot_nt(km_lo, qz)
        gate = jnp.where(blk_id < i, gate, -jnp.inf)
        sel = jnp.full((nblk, blk), NEG_BIG, jnp.float32)
        for _ in range(MOBA_TOPK):
            mx = jnp.max(gate, axis=0, keepdims=True)
            cand = (gate == mx) & (mx > -jnp.inf)
            idx = jnp.min(jnp.where(cand, blk_id, nblk), axis=0, keepdims=True)
            chosen = blk_id == idx
            sel = jnp.where(chosen, 0.0, sel)
            gate = jnp.where(chosen, -jnp.inf, gate)
        sel_ref[2 * w + h] = sel


def _moba_tiles(t, p, slopes_ref, q_ref, o_ref, vt_ref, km_ref, ka_ref, sel_ref, s_ref, e_ref, st_ref, acc_ref):
    nblk = vt_ref.shape[0]
    blk = MOBA_BLOCK
    half = HEAD_DIM
    sub = SUBLANES_V7X
    tile3 = (MOBA_GROUPS, sub, blk)
    key_pos = lax.broadcasted_iota(jnp.int32, tile3, 0) * sub + lax.broadcasted_iota(jnp.int32, tile3, 1)
    qry_pos = lax.broadcasted_iota(jnp.int32, tile3, 2)
    lane, own, spare, _ = _moba_lanes()
    streams = range(2 * MOBA_TILES)
    tile_of = [t * MOBA_TILES + st // 2 for st in streams]
    head_of = [st % 2 for st in streams]
    last_tile = tile_of[-1]

    qh = []
    for st in streams:
        h = head_of[st]
        q2f = q_ref[pl.ds(pl.multiple_of(tile_of[st] * blk, blk), blk), :].astype(jnp.float32)
        hi = slopes_ref[A_HEADS + 2 * p + h]
        lo = slopes_ref[2 * A_HEADS + 2 * p + h]
        extra = jnp.where(lane == spare[h], hi, jnp.where(lane == spare[h] + 1, lo, 0.0))
        qh.append(jnp.where(own[h], q2f, extra).astype(jnp.bfloat16))

    unroll = MOBA_UNROLL
    acc_groups = MOBA_V_ROWS // sub

    def past_block(n):
        return jnp.clip(n - 1, 0, nblk - 1)

    def item_block(n, st, own_first):
        return tile_of[st] if own_first else past_block(n)

    def issue_scores(items, own_first):
        out = []
        for u, n in enumerate(items):
            per_stream = []
            for st in streams:
                b = item_block(n, st, own_first and u == 0)
                rows = pl.ds(pl.multiple_of(b * blk, blk), blk)
                per_stream.append(_dot_nt(ka_ref[head_of[st], rows, :], qh[st]).reshape(tile3))
            out.append(per_stream)
        return out

    def stage_scores(raws, slot, items, own_first):
        for u, n in enumerate(items):
            for st in streams:
                s = raws[u][st]
                if own_first and u == 0:
                    s = jnp.where(qry_pos >= key_pos, s, NEG_BIG)
                    row = jnp.zeros((sub, blk), jnp.float32)
                else:
                    b = past_block(n)
                    gap = ((tile_of[st] - b) * blk).astype(jnp.float32)
                    row = sel_ref[st, pl.ds(b, 1), :] - slopes_ref[2 * p + head_of[st]] * gap
                    row = jnp.broadcast_to(jnp.where(n <= tile_of[st], row, NEG_BIG), (sub, blk))
                s_ref[slot, u, st] = s
                st_ref[slot, u, st, 0] = _all_sublanes_max(jnp.max(s, axis=0)) + row
                st_ref[slot, u, st, 1] = row

    def exponentiate(group_slot, ms):
        new_ms, alphas = [], []
        for st in streams:
            m_new = ms[st]
            for u in range(unroll):
                m_new = jnp.maximum(m_new, st_ref[group_slot, u, st, 0])
            alphas.append(jnp.exp2(ms[st] - m_new))
            new_ms.append(m_new)
            for u in range(unroll):
                e = jnp.exp2(s_ref[group_slot, u, st] - (m_new - st_ref[group_slot, u, st, 1])[None])
                e_ref[group_slot, u, st] = e.reshape(blk, blk).astype(jnp.bfloat16)
        return new_ms, alphas

    def accumulate(group, slot, alphas):
        for st in streams:
            acc = alphas[st][None] * acc_ref[st]
            for u in range(unroll):
                n = group * unroll + u
                vblock = jnp.where(n == 0, tile_of[st], past_block(n))
                acc = acc + _dot(vt_ref[vblock, head_of[st]], e_ref[slot, u, st]).reshape(acc_groups, sub, blk)
            acc_ref[st] = acc

    def score_group(group, slot, own_first):
        items = [group * unroll + u for u in range(unroll)]
        raws = issue_scores(items, own_first)
        return lambda: stage_scores(raws, slot, items, own_first)

    def tick(g, slot, ms, alphas):
        accumulate(g - 2, slot, alphas)
        finish_scores = score_group(g, slot, False)
        ms, alphas = exponentiate(1 - slot, ms)
        finish_scores()
        return ms, alphas

    n_st = len(streams)
    pairs = jnp.maximum(lax.shift_right_logical(last_tile + 2 * unroll, MOBA_UNROLL_LOG2 + 1), 1)
    ms = [jnp.full((sub, blk), -jnp.inf, jnp.float32) for _ in streams]
    acc_ref[...] = jnp.zeros(acc_ref.shape, jnp.float32)
    score_group(0, 0, True)()
    finish_scores = score_group(1, 1, False)
    ms, alphas = exponentiate(0, ms)
    finish_scores()

    def two_ticks(k, carry):
        ms, alphas = tick(2 * k, 0, list(carry[:n_st]), list(carry[n_st:]))
        ms, alphas = tick(2 * k + 1, 1, ms, alphas)
        return tuple(ms + alphas)

    carry = lax.fori_loop(1, pairs, two_ticks, tuple(ms + alphas))
    ms, alphas = list(carry[:n_st]), list(carry[n_st:])
    accumulate(2 * pairs - 2, 0, alphas)
    ms, alphas = exponentiate(1, ms)
    for w in range(MOBA_TILES):
        _moba_select(jnp.minimum(tile_of[2 * w] + MOBA_TILES, nblk - 1), w, q_ref, km_ref, sel_ref)
    accumulate(2 * pairs - 1, 1, alphas)
    for w in range(MOBA_TILES):
        accs = [acc_ref[2 * w + h] for h in (0, 1)]
        o_t = jnp.concatenate([(acc[0:half // sub] / acc[half // sub][None]).reshape(half, blk) for acc in accs],
                              axis=0)
        o_ref[pl.ds(pl.multiple_of(tile_of[2 * w] * blk, blk), blk), :] = o_t.T.astype(o_ref.dtype)


def _moba_slopes():
    full = np.asarray([sl * LOG2E for sl in SLOPES_A], np.float32)
    hi = full.astype(jnp.bfloat16).astype(np.float32)
    lo = (full - hi).astype(jnp.bfloat16).astype(np.float32)
    return jnp.asarray(np.concatenate([full, hi, lo]))


def _moba_call(z):
    s = z.shape[0]
    nblk = s // MOBA_BLOCK
    pairs = A_HEADS // 2
    ln = LANES_V7X
    qb, kb, vb = Z_A // ln, Z_A // ln + pairs, Z_A // ln + 2 * pairs
    return pl.pallas_call(
        _moba_kernel,
        grid=(pairs,),
        in_specs=[
            pl.BlockSpec(memory_space=pltpu.SMEM),
            pl.BlockSpec((s, ln), lambda p: (0, qb + p)),
            pl.BlockSpec((s, ln), lambda p: (0, kb + p), pipeline_mode=pl.Buffered(1)),
            pl.BlockSpec((s, ln), lambda p: (0, vb + p), pipeline_mode=pl.Buffered(1)),
        ],
        out_specs=pl.BlockSpec((s, ln), lambda p: (0, p)),
        out_shape=jax.ShapeDtypeStruct((s, A_HEADS * HEAD_DIM), jnp.bfloat16),
        scratch_shapes=[
            pltpu.VMEM((nblk, 2, MOBA_V_ROWS, MOBA_BLOCK), jnp.bfloat16),
            pltpu.VMEM((nblk, ln), jnp.float32),
            pltpu.VMEM((2, s, ln), jnp.bfloat16),
            pltpu.VMEM((2 * MOBA_TILES, nblk, MOBA_BLOCK), jnp.float32),
            pltpu.VMEM((2, MOBA_UNROLL, 2 * MOBA_TILES, MOBA_GROUPS, SUBLANES_V7X, MOBA_BLOCK), jnp.float32),
            pltpu.VMEM((2, MOBA_UNROLL, 2 * MOBA_TILES, MOBA_BLOCK, MOBA_BLOCK), jnp.bfloat16),
            pltpu.VMEM((2, MOBA_UNROLL, 2 * MOBA_TILES, 2, SUBLANES_V7X, MOBA_BLOCK), jnp.float32),
            pltpu.VMEM((2 * MOBA_TILES, MOBA_V_ROWS // SUBLANES_V7X, SUBLANES_V7X, MOBA_BLOCK), jnp.float32),
        ],
        compiler_params=_params(("arbitrary",)),
        name="moba_attn",
    )(_moba_slopes(), z, z, z)


def _blocks_per_step(nb):
    return max(c for c in (4, 2, 1) if nb % c == 0)


def _band_penalty_table(slope_dils, max_steps):
    steps = np.arange(BAND)[:, None] + BAND - np.arange(2 * BAND)[None, :]
    inside = (steps >= 0) & (steps <= max_steps)
    table = [np.where(inside, np.float32(sd) * steps.astype(np.float32), np.float32(-NEG_BIG)) for sd in slope_dils]
    return jnp.asarray(np.stack(table).astype(np.float32))


def _band_no_prev(first_step):
    kj = lax.broadcasted_iota(jnp.int32, (BAND, 2 * BAND), 1)
    return jnp.logical_and(first_step, kj < BAND)


def _band_units(units, lookahead):
    raws = {j: _dot_nt(units[j][0], units[j][1]) for j in range(min(lookahead, len(units)))}
    outs, lses = [], []
    for j, (_, _, vv, penalty, drop, sink) in enumerate(units):
        ahead = j + lookahead
        if ahead < len(units):
            raws[ahead] = _dot_nt(units[ahead][0], units[ahead][1])
        s = raws.pop(j) - penalty
        if drop is not None:
            s = jnp.where(drop, NEG_BIG, s)
        m = jnp.max(s, axis=1, keepdims=True)
        if sink is not None:
            m = jnp.maximum(m, sink)
        e = jnp.exp(s - m)
        denom = jnp.sum(e, axis=1, keepdims=True)
        if sink is not None:
            denom = denom + jnp.exp(sink - m)
        outs.append(_dot(e.astype(jnp.bfloat16), vv) / denom)
        lses.append(m + jnp.log(denom))
    return outs, lses


def _swa_kernel(sinks_ref, bias_ref, q_ref, kp_ref, ko_ref, vp_ref, vo_ref, o_ref):
    n = pl.program_id(0)
    nbq = q_ref.shape[0] // BAND
    no_prev = _band_no_prev(n == 0)
    k_all = jnp.concatenate([kp_ref[...], ko_ref[...]], axis=0)
    v_all = jnp.concatenate([vp_ref[...], vo_ref[...]], axis=0)
    ln = LANES_V7X
    lane = lax.broadcasted_iota(jnp.int32, (BAND, ln), 1)
    low = lane < HEAD_DIM
    units = []
    for qb in range(nbq):
        kk = k_all[qb * BAND:(qb + 2) * BAND]
        vv = v_all[qb * BAND:(qb + 2) * BAND]
        drop = no_prev if qb == 0 else None
        for b in range(B_HEADS // 2):
            q2 = q_ref[qb * BAND:(qb + 1) * BAND, b * ln:(b + 1) * ln]
            for c in range(2):
                head = B_HEAD_ORDER[2 * b + c]
                qm = jnp.where(low if c == 0 else jnp.logical_not(low), q2, jnp.zeros_like(q2))
                units.append((qm, kk, vv, bias_ref[2 * b + c], drop, sinks_ref[head]))
    outs, _ = _band_units(units, B_HEADS)
    for qb in range(nbq):
        for b in range(B_HEADS // 2):
            j = qb * B_HEADS + 2 * b
            o_ref[qb * BAND:(qb + 1) * BAND, b * ln:(b + 1) * ln] = (
                jnp.where(low, outs[j], outs[j + 1]).astype(o_ref.dtype))


def _swa_call(z, sinks):
    s = z.shape[0]
    nbq = _blocks_per_step(s // BAND)
    rows = nbq * BAND
    ln = LANES_V7X
    qw = B_HEADS * HEAD_DIM
    prev = lambda n: jnp.maximum(n * nbq - 1, 0)
    bias = _band_penalty_table([SLOPES_B[head] for head in B_HEAD_ORDER], B_WINDOW - 1)
    return pl.pallas_call(
        _swa_kernel,
        grid=(s // rows,),
        in_specs=[
            pl.BlockSpec(memory_space=pltpu.SMEM),
            pl.BlockSpec(bias.shape, lambda n: (0, 0, 0), pipeline_mode=pl.Buffered(1)),
            pl.BlockSpec((rows, qw), lambda n: (n, Z_BQ // qw)),
            pl.BlockSpec((BAND, ln), lambda n: (prev(n), Z_BK // ln)),
            pl.BlockSpec((rows, ln), lambda n: (n, Z_BK // ln)),
            pl.BlockSpec((BAND, ln), lambda n: (prev(n), Z_BV // ln)),
            pl.BlockSpec((rows, ln), lambda n: (n, Z_BV // ln)),
        ],
        out_specs=pl.BlockSpec((rows, qw), lambda n: (n, 0)),
        out_shape=jax.ShapeDtypeStruct((s, qw), jnp.bfloat16),
        compiler_params=_params(("arbitrary",)),
        name="swa_attn",
    )(sinks, bias, z, z, z, z, z)


def _dilated_kernel(bias_ref, q_ref, kp_ref, ko_ref, vp_ref, vo_ref, o_ref, lse_ref):
    n = pl.program_id(1)
    nbq = q_ref.shape[0] // BAND
    no_prev = _band_no_prev(n == 0)
    k_all = jnp.concatenate([kp_ref[...], ko_ref[...]], axis=0)
    v_all = jnp.concatenate([vp_ref[...], vo_ref[...]], axis=0)
    cw = q_ref.shape[1]
    lane = lax.broadcasted_iota(jnp.int32, (BAND, cw), 1)
    heads = range(C_HEADS_PER_GROUP)
    mine = [(lane >= h * HEAD_DIM) & (lane < (h + 1) * HEAD_DIM) for h in heads]
    biases = [bias_ref[h] for h in heads]
    units = []
    for qb in range(nbq):
        q4 = q_ref[qb * BAND:(qb + 1) * BAND, :]
        kk = k_all[qb * BAND:(qb + 2) * BAND]
        vv = v_all[qb * BAND:(qb + 2) * BAND]
        drop = no_prev if qb == 0 else None
        for h in heads:
            units.append((jnp.where(mine[h], q4, jnp.zeros_like(q4)), kk, vv, biases[h], drop, None))
    outs, lses = _band_units(units, C_HEADS_PER_GROUP)
    for qb in range(nbq):
        o_all = jnp.zeros((BAND, cw), jnp.float32)
        lse_all = jnp.zeros((BAND, cw), jnp.float32)
        for h in heads:
            j = qb * C_HEADS_PER_GROUP + h
            o_all = jnp.where(mine[h], outs[j], o_all)
            lse_all = jnp.where(mine[h], lses[j], lse_all)
        o_ref[qb * BAND:(qb + 1) * BAND, :] = o_all
        lse_ref[qb * BAND:(qb + 1) * BAND, :] = lse_all


def _dilated_call(zc, group, col0):
    window, dil = C_GROUPS[group]
    assert zc.shape[0] == dil
    length = zc.shape[1]
    nbq = _blocks_per_step(length // BAND)
    rows = nbq * BAND
    cw = C_HEADS_PER_GROUP * HEAD_DIM
    base = col0 // cw
    slopes = [SLOPES_C[group * C_HEADS_PER_GROUP + h] * dil for h in range(C_HEADS_PER_GROUP)]
    bias = _band_penalty_table(slopes, window // dil)
    own = lambda col: pl.BlockSpec((None, rows, cw), lambda r, n: (r, n, base + col))
    prev = lambda col: pl.BlockSpec((None, BAND, cw), lambda r, n: (r, jnp.maximum(n * nbq - 1, 0), base + col))
    out_blk = pl.BlockSpec((None, rows, cw), lambda r, n: (r, n, 0))
    return pl.pallas_call(
        _dilated_kernel,
        grid=(dil, length // rows),
        in_specs=[pl.BlockSpec(bias.shape, lambda r, n: (0, 0, 0), pipeline_mode=pl.Buffered(1)),
                  own(0), prev(1), own(1), prev(2), own(2)],
        out_specs=[out_blk, out_blk],
        out_shape=[jax.ShapeDtypeStruct((dil, length, cw), jnp.float32)] * 2,
        compiler_params=_params(("arbitrary", "arbitrary")),
        name=f"dilated_attn_g{group}",
    )(bias, zc, zc, zc, zc, zc)


def _merge_kernel(h_ref, zg_ref, oa_ref, ob_ref, o0_ref, l0_ref, o1_ref, l1_ref, o2_ref, l2_ref,
                  wa_ref, wb_ref, wc_ref, wo_ref, g_ref, mod_ref, out_ref, nat_ref):
    tm = h_ref.shape[0]

    def natural(ref, slot):
        dil = ref.shape[0]
        if dil == 1:
            return ref[0]
        ln = LANES_V7X
        chunks = ref.shape[2] // ln
        for r in range(dil):
            for c in range(chunks):
                nat_ref[slot * chunks + c, pl.ds(r, tm // dil, stride=dil), :] = ref[r, :, c * ln:(c + 1) * ln]
        return jnp.concatenate([nat_ref[slot * chunks + c] for c in range(chunks)], axis=1)

    l0, l1, l2 = natural(l0_ref, 0), natural(l1_ref, 0), natural(l2_ref, 1)
    o0, o1, o2 = natural(o0_ref, 0), natural(o1_ref, 2), natural(o2_ref, 3)
    mx = jnp.maximum(jnp.maximum(l0, l1), l2)
    e0, e1, e2 = jnp.exp(l0 - mx), jnp.exp(l1 - mx), jnp.exp(l2 - mx)
    oc = ((e0 * o0 + e1 * o1 + e2 * o2) / (e0 + e1 + e2)).astype(jnp.bfloat16)
    d = D_MODEL
    subs = range(0, tm, MERGE_SUB_ROWS)
    branches = [(_dot(oa_ref[pl.ds(r0, MERGE_SUB_ROWS), :], wa_ref[...]),
                 _dot(ob_ref[pl.ds(r0, MERGE_SUB_ROWS), :], wb_ref[...]),
                 _dot(oc[r0:r0 + MERGE_SUB_ROWS], wc_ref[...])) for r0 in subs]
    ys = []
    for r0, (ya, yb, yc) in zip(subs, branches):
        gates = jax.nn.sigmoid(zg_ref[pl.ds(r0, MERGE_SUB_ROWS), :].astype(jnp.float32))
        merged = gates[:, 0:d] * ya + gates[:, d:2 * d] * yb + gates[:, 2 * d:3 * d] * yc
        ys.append(_dot(merged.astype(jnp.bfloat16), wo_ref[...]))
    for r0, y in zip(subs, ys):
        rs = pl.ds(r0, MERGE_SUB_ROWS)
        out_ref[rs, :] = h_ref[rs, :] + mod_ref[2:3, :] * _rmsnorm(y, g_ref[...])


MERGE_SUB_ROWS = 256


def _merge_call(h, z, oa, ob, oc_lse, wa, wb, wc, wo, g, mod, layer):
    s, d = h.shape
    tm = 2 * MERGE_SUB_ROWS
    cw = C_HEADS_PER_GROUP * HEAD_DIM
    row = lambda w: pl.BlockSpec((tm, w), lambda i: (i, 0))
    res = lambda a: pl.BlockSpec((a.shape[0], tm // a.shape[0], cw), lambda i: (0, i, 0))
    full = lambda a: pl.BlockSpec(a.shape, lambda i: (0, 0), pipeline_mode=pl.Buffered(1))
    stacked = lambda a: pl.BlockSpec((None,) + a.shape[1:], lambda i: (layer, 0, 0), pipeline_mode=pl.Buffered(1))
    return pl.pallas_call(
        _merge_kernel,
        grid=(s // tm,),
        in_specs=[row(d), row(3 * d), row(oa.shape[1]), row(ob.shape[1])] + [res(a) for a in oc_lse]
        + [stacked(wa), stacked(wb), stacked(wc), stacked(wo), full(g), full(mod)],
        out_specs=row(d),
        out_shape=jax.ShapeDtypeStruct((s, d), jnp.float32),
        scratch_shapes=[pltpu.VMEM((4 * cw // LANES_V7X, tm, LANES_V7X), jnp.float32)],
        input_output_aliases={0: 0} if layer > 0 else {},
        compiler_params=_params(("arbitrary",)),
        name="merge_out_proj",
    )(h, z, oa, ob, *oc_lse, wa, wb, wc, wo, g, mod)


FFN_CHUNK = 1408


def _ffn_kernel(h_ref, g_pre_ref, g_post_ref, mod_ref, wg_ref, wu_ref, wd_ref, out_ref):
    x = h_ref[...]
    u = _rmsnorm(x, g_pre_ref[...]) * (1.0 + mod_ref[4:5, :]) + mod_ref[3:4, :]
    u = u.astype(jnp.bfloat16)
    y = jnp.zeros(x.shape, jnp.float32)
    for c0 in range(0, D_FF, FFN_CHUNK):
        gate = _dot(u, wg_ref[:, c0:c0 + FFN_CHUNK])
        up = _dot(u, wu_ref[:, c0:c0 + FFN_CHUNK])
        act = (gate * jax.nn.sigmoid(gate) * up).astype(jnp.bfloat16)
        y = y + _dot(act, wd_ref[c0:c0 + FFN_CHUNK, :])
    out_ref[...] = x + mod_ref[5:6, :] * _rmsnorm(y, g_post_ref[...])


def _ffn_call(h, g_pre, g_post, mod, wg, wu, wd, layer):
    s, d = h.shape
    tm = ROW_TILE
    row = pl.BlockSpec((tm, d), lambda i: (i, 0))
    full = lambda a: pl.BlockSpec(a.shape, lambda i: (0, 0), pipeline_mode=pl.Buffered(1))
    stacked = lambda a: pl.BlockSpec((None,) + a.shape[1:], lambda i: (layer, 0, 0), pipeline_mode=pl.Buffered(1))
    return pl.pallas_call(
        _ffn_kernel,
        grid=(s // tm,),
        in_specs=[row, full(g_pre), full(g_post), full(mod), stacked(wg), stacked(wu), stacked(wd)],
        out_specs=row,
        out_shape=jax.ShapeDtypeStruct((s, d), jnp.float32),
        input_output_aliases={0: 0},
        compiler_params=_params(("arbitrary",)),
        name="swiglu_ffn",
    )(h, g_pre, g_post, mod, wg, wu, wd)


def _permute_in_proj(w_in):
    hd = HEAD_DIM
    o_bq = 3 * A_HEADS * hd
    o_bkv = o_bq + B_HEADS * hd
    o_c = o_bkv + 2 * B_KV_HEADS * hd
    o_g = o_c + 3 * C_HEADS * hd
    cw = C_HEADS_PER_GROUP * hd
    sl = lambda a, b: w_in[:, :, a:b]
    parts = [sl(o_g, o_g + 3 * D_MODEL), sl(0, A_HEADS * hd) * (QK_SCALE * LOG2E), sl(A_HEADS * hd, o_bq)]
    parts += [sl(o_bq + hh * hd, o_bq + (hh + 1) * hd) * QK_SCALE for hh in B_HEAD_ORDER]
    parts += [sl(o_bkv, o_c)]
    for g in range(len(C_GROUPS)):
        parts += [sl(o_c + g * 3 * cw, o_c + g * 3 * cw + cw) * QK_SCALE,
                  sl(o_c + g * 3 * cw + cw, o_c + (g + 1) * 3 * cw)]
    out = jnp.concatenate([part.astype(jnp.bfloat16) for part in parts], axis=2)
    assert out.shape[2] == IN_WIDTH
    return out


def kernel(x, c, w_ada, b_ada, g_pre_mix, g_post_mix, w_in, sinks, w_br_a, w_br_b, w_br_c,
           w_out, g_pre_ffn, g_post_ffn, w_gate, w_up, w_down):
    bn, s, d = x.shape
    assert bn == 1 and d == D_MODEL and s % SEQ_MULTIPLE == 0
    bf = jnp.bfloat16
    hd = HEAD_DIM
    w_in_p = _permute_in_proj(w_in)
    w_br_b_p = jnp.concatenate([w_br_b[:, hh * hd:(hh + 1) * hd] for hh in B_HEAD_ORDER], axis=1).astype(bf)
    w_br_a_b, w_br_c_b, w_out_b = w_br_a.astype(bf), w_br_c.astype(bf), w_out.astype(bf)
    w_gate_b, w_up_b, w_down_b = w_gate.astype(bf), w_up.astype(bf), w_down.astype(bf)

    mod_all = _ada_call(c, w_ada, b_ada).reshape(DEPTH, 6, d)
    h = x.reshape(s, d)
    for l in range(DEPTH):
        mod = mod_all[l]
        z, zc1, zc2 = _in_proj_call(h, g_pre_mix[l].reshape(1, d), mod, w_in_p, l)
        oa = _moba_call(z)
        ob = _swa_call(z, sinks[l])
        oc_lse = (_dilated_call(z.reshape(1, s, Z_WIDTH), 0, Z_C0)
                  + _dilated_call(zc1, 1, 0) + _dilated_call(zc2, 2, 0))
        h = _merge_call(h, z, oa, ob, oc_lse, w_br_a_b, w_br_b_p, w_br_c_b, w_out_b,
                        g_post_mix[l].reshape(1, d), mod, l)
        h = _ffn_call(h, g_pre_ffn[l].reshape(1, d), g_post_ffn[l].reshape(1, d), mod,
                      w_gate_b, w_up_b, w_down_b, l)
    return h.reshape(bn, s, d)
```

```python
import numpy as np
import jax
import jax.numpy as jnp
from jax import lax
from jax.experimental import pallas as pl
from jax.experimental.pallas import tpu as pltpu

D_MODEL = 1024
DEPTH = 4
HEAD_DIM = 64
A_HEADS = 8
MOBA_BLOCK = 256
MOBA_TOPK = 3
B_HEADS = 8
B_KV_HEADS = 2
B_WINDOW = 128
C_GROUPS = ((128, 1), (512, 4), (2048, 16))
C_HEADS_PER_GROUP = 4
C_HEADS = len(C_GROUPS) * C_HEADS_PER_GROUP
BAND = 128
D_FF = 2816
N_ALIBI_HEADS = A_HEADS + B_HEADS + C_HEADS
SEQ_MULTIPLE = 2048
RMS_EPS = 1e-6

LANES_V7X = 128
BF16_SUBLANES_V7X = 16
VMEM_LIMIT_BYTES_V7X = 56 * 1024 * 1024

ROW_TILE = 512
C_WIDTH = 3 * C_HEADS_PER_GROUP * HEAD_DIM

Z_GATES = 0
Z_A = 3 * D_MODEL
Z_BQ = Z_A + 3 * A_HEADS * HEAD_DIM
Z_BK = Z_BQ + B_HEADS * HEAD_DIM
Z_BV = Z_BK + B_KV_HEADS * HEAD_DIM
Z_C0 = Z_BV + B_KV_HEADS * HEAD_DIM
Z_WIDTH = Z_C0 + C_WIDTH
IN_WIDTH = Z_WIDTH + (len(C_GROUPS) - 1) * C_WIDTH
B_HEAD_ORDER = (0, 4, 1, 5, 2, 6, 3, 7)

NEG_BIG = -1e30
QK_SCALE = HEAD_DIM ** -0.5
LOG2E = 1.4426950408889634


def _alibi_slopes():
    n = N_ALIBI_HEADS
    return [float(2.0 ** (-8.0 * (i + 1) / n)) for i in range(n)]


_SLOPES = _alibi_slopes()
SLOPES_B = _SLOPES[:B_HEADS]
SLOPES_C = _SLOPES[B_HEADS:B_HEADS + C_HEADS]
SLOPES_A = _SLOPES[B_HEADS + C_HEADS:]


def _dot(a, b):
    return jnp.dot(a, b, preferred_element_type=jnp.float32)


def _dot_nt(a, b):
    return lax.dot_general(a, b, (((1,), (1,)), ((), ())), preferred_element_type=jnp.float32)


def _params(semantics):
    return pltpu.CompilerParams(dimension_semantics=semantics, vmem_limit_bytes=VMEM_LIMIT_BYTES_V7X)


def _rmsnorm(x, g):
    return x * lax.rsqrt(jnp.mean(x * x, axis=-1, keepdims=True) + RMS_EPS) * g


def _ada_kernel(c_ref, w_ref, b_ref, o_ref):
    c = c_ref[...]
    sc = c * jax.nn.sigmoid(c)
    o_ref[...] = jnp.sum(w_ref[...] * sc, axis=0, keepdims=True) + b_ref[...]


def _ada_call(c, w_ada, b_ada):
    depth, d, n = w_ada.shape
    tn = 1536
    return pl.pallas_call(
        _ada_kernel,
        grid=(depth, n // tn),
        in_specs=[
            pl.BlockSpec((d, 1), lambda l, j: (0, 0)),
            pl.BlockSpec((None, d, tn), lambda l, j: (l, 0, j)),
            pl.BlockSpec((None, 1, tn), lambda l, j: (l, 0, j)),
        ],
        out_specs=pl.BlockSpec((None, 1, tn), lambda l, j: (l, 0, j)),
        out_shape=jax.ShapeDtypeStruct((depth, 1, n), jnp.float32),
        compiler_params=_params(("arbitrary", "arbitrary")),
        name="adaln_mod",
    )(c.reshape(d, 1), w_ada, b_ada.reshape(depth, 1, n))


IN_PROJ_CHUNK = 512


def _in_proj_kernel(h_ref, g_ref, mod_ref, w_ref, z_ref, zc1_ref, zc2_ref, tmp_ref):
    x = h_ref[...]
    u = _rmsnorm(x, g_ref[...]) * (1.0 + mod_ref[1:2, :]) + mod_ref[0:1, :]
    u = u.astype(jnp.bfloat16)
    for c0 in range(0, Z_WIDTH, IN_PROJ_CHUNK):
        z_ref[:, c0:c0 + IN_PROJ_CHUNK] = _dot(u, w_ref[:, c0:c0 + IN_PROJ_CHUNK]).astype(z_ref.dtype)
    for g, out_ref in ((1, zc1_ref), (2, zc2_ref)):
        c0 = Z_WIDTH + (g - 1) * C_WIDTH
        res = _dot(u, w_ref[:, c0:c0 + C_WIDTH])
        dil = C_GROUPS[g][1]
        rows = x.shape[0] // dil
        ln = LANES_V7X
        for c in range(C_WIDTH // ln):
            tmp_ref[c] = res[:, c * ln:(c + 1) * ln]
        for r in range(dil):
            for c in range(C_WIDTH // ln):
                out_ref[r, :, c * ln:(c + 1) * ln] = tmp_ref[c, pl.ds(r, rows, stride=dil), :].astype(out_ref.dtype)


def _in_proj_call(h, g, mod, w, layer):
    s, d = h.shape
    tm = ROW_TILE
    d1, d2 = C_GROUPS[1][1], C_GROUPS[2][1]
    assert tm % (d2 * BF16_SUBLANES_V7X) == 0
    return pl.pallas_call(
        _in_proj_kernel,
        grid=(s // tm,),
        in_specs=[
            pl.BlockSpec((tm, d), lambda i: (i, 0)),
            pl.BlockSpec((1, d), lambda i: (0, 0)),
            pl.BlockSpec((6, d), lambda i: (0, 0)),
            pl.BlockSpec((None, d, IN_WIDTH), lambda i: (layer, 0, 0), pipeline_mode=pl.Buffered(1)),
        ],
        out_specs=[
            pl.BlockSpec((tm, Z_WIDTH), lambda i: (i, 0)),
            pl.BlockSpec((d1, tm // d1, C_WIDTH), lambda i: (0, i, 0)),
            pl.BlockSpec((d2, tm // d2, C_WIDTH), lambda i: (0, i, 0)),
        ],
        out_shape=[
            jax.ShapeDtypeStruct((s, Z_WIDTH), jnp.bfloat16),
            jax.ShapeDtypeStruct((d1, s // d1, C_WIDTH), jnp.bfloat16),
            jax.ShapeDtypeStruct((d2, s // d2, C_WIDTH), jnp.bfloat16),
        ],
        scratch_shapes=[pltpu.VMEM((C_WIDTH // LANES_V7X, tm, LANES_V7X), jnp.float32)],
        compiler_params=_params(("arbitrary",)),
        name="in_proj",
    )(h, g, mod, w)


SUBLANES_V7X = 8
MOBA_V_ROWS = HEAD_DIM + BF16_SUBLANES_V7X
MOBA_GROUPS = MOBA_BLOCK // SUBLANES_V7X
MOBA_UNROLL_LOG2 = 0
MOBA_UNROLL = 1 << MOBA_UNROLL_LOG2
MOBA_TILES = 4


def _all_sublanes_max(x):
    for shift in (4, 2, 1):
        x = jnp.maximum(x, pltpu.roll(x, shift, axis=0))
    return x


def _moba_lanes():
    lane = lax.broadcasted_iota(jnp.int32, (MOBA_BLOCK, LANES_V7X), 1)
    own = [lane < HEAD_DIM, lane >= HEAD_DIM]
    spare = [HEAD_DIM, 0]
    bias_lanes = [(lane == spare[h]) | (lane == spare[h] + 1) for h in (0, 1)]
    return lane, own, spare, bias_lanes


def _moba_kernel(slopes_ref, q_ref, k_ref, v_ref, o_ref, vt_ref, km_ref, ka_ref, sel_ref, s_ref, e_ref, st_ref, acc_ref):
    p = pl.program_id(0)
    nblk = vt_ref.shape[0]
    blk = MOBA_BLOCK
    half = HEAD_DIM
    _, own, _, bias_lanes = _moba_lanes()
    ones = jnp.ones((BF16_SUBLANES_V7X, blk), jnp.bfloat16)

    def prepare(j, carry):
        rows = pl.ds(pl.multiple_of(j * blk, blk), blk)
        vt = v_ref[rows, :].astype(jnp.float32).T.astype(jnp.bfloat16)
        k2 = k_ref[rows, :]
        km_ref[pl.ds(j, 1), :] = jnp.mean(k2.astype(jnp.float32), axis=0, keepdims=True)
        k2f = k2.astype(jnp.float32)
        pos = lax.broadcasted_iota(jnp.int32, k2.shape, 0).astype(jnp.float32)
        for h in (0, 1):
            vt_ref[j, h, 0:half, :] = vt[h * half:(h + 1) * half, :]
            vt_ref[j, h, half:, :] = ones
            ka = jnp.where(own[h], k2f, jnp.where(bias_lanes[h], pos, 0.0))
            ka_ref[h, rows, :] = ka.astype(jnp.bfloat16)
        return carry

    lax.fori_loop(0, nblk, prepare, 0)
    for w in range(MOBA_TILES):
        _moba_select(w, w, q_ref, km_ref, sel_ref)

    def tiles(t, carry):
        _moba_tiles(t, p, slopes_ref, q_ref, o_ref, vt_ref, km_ref, ka_ref, sel_ref, s_ref, e_ref, st_ref, acc_ref)
        return carry

    lax.fori_loop(0, nblk // MOBA_TILES, tiles, 0)


def _moba_select(i, w, q_ref, km_ref, sel_ref):
    nblk, blk = sel_ref.shape[1], MOBA_BLOCK
    _, own, _, _ = _moba_lanes()
    q2 = q_ref[pl.ds(pl.multiple_of(i * blk, blk), blk), :]
    blk_id = lax.broadcasted_iota(jnp.int32, (nblk, blk), 0)
    km = km_ref[...]
    km_hi = km.astype(jnp.bfloat16)
    km_lo = (km - km_hi.astype(jnp.float32)).astype(jnp.bfloat16)
    for h in (0, 1):
        qz = jnp.where(own[h], q2, jnp.zeros_like(q2))
        gate = _dot_nt(km_hi, qz) + _dot_nt(km_lo, qz)
        gate = jnp.where(blk_id < i, gate, -jnp.inf)
        sel = jnp.full((nblk, blk), NEG_BIG, jnp.float32)
        for _ in range(MOBA_TOPK):
            mx = jnp.max(gate, axis=0, keepdims=True)
            cand = (gate == mx) & (mx > -jnp.inf)
            idx = jnp.min(jnp.where(cand, blk_id, nblk), axis=0, keepdims=True)
            chosen = blk_id == idx
            sel = jnp.where(chosen, 0.0, sel)
            gate = jnp.where(chosen, -jnp.inf, gate)
        sel_ref[2 * w + h] = sel


def _moba_tiles(t, p, slopes_ref, q_ref, o_ref, vt_ref, km_ref, ka_ref, sel_ref, s_ref, e_ref, st_ref, acc_ref):
    nblk = vt_ref.shape[0]
    blk = MOBA_BLOCK
    half = HEAD_DIM
    sub = SUBLANES_V7X
    tile3 = (MOBA_GROUPS, sub, blk)
    key_pos = lax.broadcasted_iota(jnp.int32, tile3, 0) * sub + lax.broadcasted_iota(jnp.int32, tile3, 1)
    qry_pos = lax.broadcasted_iota(jnp.int32, tile3, 2)
    lane, own, spare, _ = _moba_lanes()
    streams = range(2 * MOBA_TILES)
    tile_of = [t * MOBA_TILES + st // 2 for st in streams]
    head_of = [st % 2 for st in streams]
    last_tile = tile_of[-1]

    qh = []
    for st in streams:
        h = head_of[st]
        q2f = q_ref[pl.ds(pl.multiple_of(tile_of[st] * blk, blk), blk), :].astype(jnp.float32)
        hi = slopes_ref[A_HEADS + 2 * p + h]
        lo = slopes_ref[2 * A_HEADS + 2 * p + h]
        extra = jnp.where(lane == spare[h], hi, jnp.where(lane == spare[h] + 1, lo, 0.0))
        qh.append(jnp.where(own[h], q2f, extra).astype(jnp.bfloat16))

    unroll = MOBA_UNROLL
    acc_groups = MOBA_V_ROWS // sub

    def past_block(n):
        return jnp.clip(n - 1, 0, nblk - 1)

    def item_block(n, st, own_first):
        return tile_of[st] if own_first else past_block(n)

    def issue_scores(items, own_first):
        out = []
        for u, n in enumerate(items):
            per_stream = []
            for st in streams:
                b = item_block(n, st, own_first and u == 0)
                rows = pl.ds(pl.multiple_of(b * blk, blk), blk)
                per_stream.append(_dot_nt(ka_ref[head_of[st], rows, :], qh[st]).reshape(tile3))
            out.append(per_stream)
        return out

    def stage_scores(raws, slot, items, own_first):
        for u, n in enumerate(items):
            for st in streams:
                s = raws[u][st]
                if own_first and u == 0:
                    s = jnp.where(qry_pos >= key_pos, s, NEG_BIG)
                    row = jnp.zeros((sub, blk), jnp.float32)
                else:
                    b = past_block(n)
                    gap = ((tile_of[st] - b) * blk).astype(jnp.float32)
                    row = sel_ref[st, pl.ds(b, 1), :] - slopes_ref[2 * p + head_of[st]] * gap
                    row = jnp.broadcast_to(jnp.where(n <= tile_of[st], row, NEG_BIG), (sub, blk))
                s_ref[slot, u, st] = s
                st_ref[slot, u, st, 0] = _all_sublanes_max(jnp.max(s, axis=0)) + row
                st_ref[slot, u, st, 1] = row

    def exponentiate(group_slot, ms):
        new_ms, alphas = [], []
        for st in streams:
            m_new = ms[st]
            for u in range(unroll):
                m_new = jnp.maximum(m_new, st_ref[group_slot, u, st, 0])
            alphas.append(jnp.exp2(ms[st] - m_new))
            new_ms.append(m_new)
            for u in range(unroll):
                e = jnp.exp2(s_ref[group_slot, u, st] - (m_new - st_ref[group_slot, u, st, 1])[None])
                e_ref[group_slot, u, st] = e.reshape(blk, blk).astype(jnp.bfloat16)
        return new_ms, alphas

    def accumulate(group, slot, alphas):
        for st in streams:
            acc = alphas[st][None] * acc_ref[st]
            for u in range(unroll):
                n = group * unroll + u
                vblock = jnp.where(n == 0, tile_of[st], past_block(n))
                acc = acc + _dot(vt_ref[vblock, head_of[st]], e_ref[slot, u, st]).reshape(acc_groups, sub, blk)
            acc_ref[st] = acc

    def score_group(group, slot, own_first):
        items = [group * unroll + u for u in range(unroll)]
        raws = issue_scores(items, own_first)
        return lambda: stage_scores(raws, slot, items, own_first)

    def tick(g, slot, ms, alphas):
        accumulate(g - 2, slot, alphas)
        finish_scores = score_group(g, slot, False)
        ms, alphas = exponentiate(1 - slot, ms)
        finish_scores()
        return ms, alphas

    n_st = len(streams)
    pairs = jnp.maximum(lax.shift_right_logical(last_tile + 2 * unroll, MOBA_UNROLL_LOG2 + 1), 1)
    ms = [jnp.full((sub, blk), -jnp.inf, jnp.float32) for _ in streams]
    acc_ref[...] = jnp.zeros(acc_ref.shape, jnp.float32)
    score_group(0, 0, True)()
    finish_scores = score_group(1, 1, False)
    ms, alphas = exponentiate(0, ms)
    finish_scores()

    def two_ticks(k, carry):
        ms, alphas = tick(2 * k, 0, list(carry[:n_st]), list(carry[n_st:]))
        ms, alphas = tick(2 * k + 1, 1, ms, alphas)
        return tuple(ms + alphas)

    carry = lax.fori_loop(1, pairs, two_ticks, tuple(ms + alphas))
    ms, alphas = list(carry[:n_st]), list(carry[n_st:])
    accumulate(2 * pairs - 2, 0, alphas)
    ms, alphas = exponentiate(1, ms)
    for w in range(MOBA_TILES):
        _moba_select(jnp.minimum(tile_of[2 * w] + MOBA_TILES, nblk - 1), w, q_ref, km_ref, sel_ref)
    accumulate(2 * pairs - 1, 1, alphas)
    for w in range(MOBA_TILES):
        accs = [acc_ref[2 * w + h] for h in (0, 1)]
        o_t = jnp.concatenate([(acc[0:half // sub] / acc[half // sub][None]).reshape(half, blk) for acc in accs],
                              axis=0)
        o_ref[pl.ds(pl.multiple_of(tile_of[2 * w] * blk, blk), blk), :] = o_t.T.astype(o_ref.dtype)


def _moba_slopes():
    full = np.asarray([sl * LOG2E for sl in SLOPES_A], np.float32)
    hi = full.astype(jnp.bfloat16).astype(np.float32)
    lo = (full - hi).astype(jnp.bfloat16).astype(np.float32)
    return jnp.asarray(np.concatenate([full, hi, lo]))


def _moba_call(z):
    s = z.shape[0]
    nblk = s // MOBA_BLOCK
    pairs = A_HEADS // 2
    ln = LANES_V7X
    qb, kb, vb = Z_A // ln, Z_A // ln + pairs, Z_A // ln + 2 * pairs
    return pl.pallas_call(
        _moba_kernel,
        grid=(pairs,),
        in_specs=[
            pl.BlockSpec(memory_space=pltpu.SMEM),
            pl.BlockSpec((s, ln), lambda p: (0, qb + p)),
            pl.BlockSpec((s, ln), lambda p: (0, kb + p), pipeline_mode=pl.Buffered(1)),
            pl.BlockSpec((s, ln), lambda p: (0, vb + p), pipeline_mode=pl.Buffered(1)),
        ],
        out_specs=pl.BlockSpec((s, ln), lambda p: (0, p)),
        out_shape=jax.ShapeDtypeStruct((s, A_HEADS * HEAD_DIM), jnp.bfloat16),
        scratch_shapes=[
            pltpu.VMEM((nblk, 2, MOBA_V_ROWS, MOBA_BLOCK), jnp.bfloat16),
            pltpu.VMEM((nblk, ln), jnp.float32),
            pltpu.VMEM((2, s, ln), jnp.bfloat16),
            pltpu.VMEM((2 * MOBA_TILES, nblk, MOBA_BLOCK), jnp.float32),
            pltpu.VMEM((2, MOBA_UNROLL, 2 * MOBA_TILES, MOBA_GROUPS, SUBLANES_V7X, MOBA_BLOCK), jnp.float32),
            pltpu.VMEM((2, MOBA_UNROLL, 2 * MOBA_TILES, MOBA_BLOCK, MOBA_BLOCK), jnp.bfloat16),
            pltpu.VMEM((2, MOBA_UNROLL, 2 * MOBA_TILES, 2, SUBLANES_V7X, MOBA_BLOCK), jnp.float32),
            pltpu.VMEM((2 * MOBA_TILES, MOBA_V_ROWS // SUBLANES_V7X, SUBLANES_V7X, MOBA_BLOCK), jnp.float32),
        ],
        compiler_params=_params(("arbitrary",)),
        name="moba_attn",
    )(_moba_slopes(), z, z, z)


def _blocks_per_step(nb):
    return max(c for c in (4, 2, 1) if nb % c == 0)


def _band_penalty_table(slope_dils, max_steps):
    steps = np.arange(BAND)[:, None] + BAND - np.arange(2 * BAND)[None, :]
    inside = (steps >= 0) & (steps <= max_steps)
    table = [np.where(inside, np.float32(sd) * steps.astype(np.float32), np.float32(-NEG_BIG)) for sd in slope_dils]
    return jnp.asarray(np.stack(table).astype(np.float32))


def _band_no_prev(first_step):
    kj = lax.broadcasted_iota(jnp.int32, (BAND, 2 * BAND), 1)
    return jnp.logical_and(first_step, kj < BAND)


def _band_units(units, lookahead):
    raws = {j: _dot_nt(units[j][0], units[j][1]) for j in range(min(lookahead, len(units)))}
    outs, lses = [], []
    for j, (_, _, vv, penalty, drop, sink) in enumerate(units):
        ahead = j + lookahead
        if ahead < len(units):
            raws[ahead] = _dot_nt(units[ahead][0], units[ahead][1])
        s = raws.pop(j) - penalty
        if drop is not None:
            s = jnp.where(drop, NEG_BIG, s)
        m = jnp.max(s, axis=1, keepdims=True)
        if sink is not None:
            m = jnp.maximum(m, sink)
        e = jnp.exp(s - m)
        denom = jnp.sum(e, axis=1, keepdims=True)
        if sink is not None:
            denom = denom + jnp.exp(sink - m)
        outs.append(_dot(e.astype(jnp.bfloat16), vv) / denom)
        lses.append(m + jnp.log(denom))
    return outs, lses


def _swa_kernel(sinks_ref, bias_ref, q_ref, kp_ref, ko_ref, vp_ref, vo_ref, o_ref):
    n = pl.program_id(0)
    nbq = q_ref.shape[0] // BAND
    no_prev = _band_no_prev(n == 0)
    k_all = jnp.concatenate([kp_ref[...], ko_ref[...]], axis=0)
    v_all = jnp.concatenate([vp_ref[...], vo_ref[...]], axis=0)
    ln = LANES_V7X
    lane = lax.broadcasted_iota(jnp.int32, (BAND, ln), 1)
    low = lane < HEAD_DIM
    units = []
    for qb in range(nbq):
        kk = k_all[qb * BAND:(qb + 2) * BAND]
        vv = v_all[qb * BAND:(qb + 2) * BAND]
        drop = no_prev if qb == 0 else None
        for b in range(B_HEADS // 2):
            q2 = q_ref[qb * BAND:(qb + 1) * BAND, b * ln:(b + 1) * ln]
            for c in range(2):
                head = B_HEAD_ORDER[2 * b + c]
                qm = jnp.where(low if c == 0 else jnp.logical_not(low), q2, jnp.zeros_like(q2))
                units.append((qm, kk, vv, bias_ref[2 * b + c], drop, sinks_ref[head]))
    outs, _ = _band_units(units, B_HEADS)
    for qb in range(nbq):
        for b in range(B_HEADS // 2):
            j = qb * B_HEADS + 2 * b
            o_ref[qb * BAND:(qb + 1) * BAND, b * ln:(b + 1) * ln] = (
                jnp.where(low, outs[j], outs[j + 1]).astype(o_ref.dtype))


def _swa_call(z, sinks):
    s = z.shape[0]
    nbq = _blocks_per_step(s // BAND)
    rows = nbq * BAND
    ln = LANES_V7X
    qw = B_HEADS * HEAD_DIM
    prev = lambda n: jnp.maximum(n * nbq - 1, 0)
    bias = _band_penalty_table([SLOPES_B[head] for head in B_HEAD_ORDER], B_WINDOW - 1)
    return pl.pallas_call(
        _swa_kernel,
        grid=(s // rows,),
        in_specs=[
            pl.BlockSpec(memory_space=pltpu.SMEM),
            pl.BlockSpec(bias.shape, lambda n: (0, 0, 0), pipeline_mode=pl.Buffered(1)),
            pl.BlockSpec((rows, qw), lambda n: (n, Z_BQ // qw)),
            pl.BlockSpec((BAND, ln), lambda n: (prev(n), Z_BK // ln)),
            pl.BlockSpec((rows, ln), lambda n: (n, Z_BK // ln)),
            pl.BlockSpec((BAND, ln), lambda n: (prev(n), Z_BV // ln)),
            pl.BlockSpec((rows, ln), lambda n: (n, Z_BV // ln)),
        ],
        out_specs=pl.BlockSpec((rows, qw), lambda n: (n, 0)),
        out_shape=jax.ShapeDtypeStruct((s, qw), jnp.bfloat16),
        compiler_params=_params(("arbitrary",)),
        name="swa_attn",
    )(sinks, bias, z, z, z, z, z)


def _dilated_kernel(bias_ref, q_ref, kp_ref, ko_ref, vp_ref, vo_ref, o_ref, lse_ref):
    n = pl.program_id(1)
    nbq = q_ref.shape[0] // BAND
    no_prev = _band_no_prev(n == 0)
    k_all = jnp.concatenate([kp_ref[...], ko_ref[...]], axis=0)
    v_all = jnp.concatenate([vp_ref[...], vo_ref[...]], axis=0)
    cw = q_ref.shape[1]
    lane = lax.broadcasted_iota(jnp.int32, (BAND, cw), 1)
    heads = range(C_HEADS_PER_GROUP)
    mine = [(lane >= h * HEAD_DIM) & (lane < (h + 1) * HEAD_DIM) for h in heads]
    biases = [bias_ref[h] for h in heads]
    units = []
    for qb in range(nbq):
        q4 = q_ref[qb * BAND:(qb + 1) * BAND, :]
        kk = k_all[qb * BAND:(qb + 2) * BAND]
        vv = v_all[qb * BAND:(qb + 2) * BAND]
        drop = no_prev if qb == 0 else None
        for h in heads:
            units.append((jnp.where(mine[h], q4, jnp.zeros_like(q4)), kk, vv, biases[h], drop, None))
    outs, lses = _band_units(units, C_HEADS_PER_GROUP)
    for qb in range(nbq):
        o_all = jnp.zeros((BAND, cw), jnp.float32)
        lse_all = jnp.zeros((BAND, cw), jnp.float32)
        for h in heads:
            j = qb * C_HEADS_PER_GROUP + h
            o_all = jnp.where(mine[h], outs[j], o_all)
            lse_all = jnp.where(mine[h], lses[j], lse_all)
        o_ref[qb * BAND:(qb + 1) * BAND, :] = o_all
        lse_ref[qb * BAND:(qb + 1) * BAND, :] = lse_all


def _dilated_call(zc, group, col0):
    window, dil = C_GROUPS[group]
    assert zc.shape[0] == dil
    length = zc.shape[1]
    nbq = _blocks_per_step(length // BAND)
    rows = nbq * BAND
    cw = C_HEADS_PER_GROUP * HEAD_DIM
    base = col0 // cw
    slopes = [SLOPES_C[group * C_HEADS_PER_GROUP + h] * dil for h in range(C_HEADS_PER_GROUP)]
    bias = _band_penalty_table(slopes, window // dil)
    own = lambda col: pl.BlockSpec((None, rows, cw), lambda r, n: (r, n, base + col))
    prev = lambda col: pl.BlockSpec((None, BAND, cw), lambda r, n: (r, jnp.maximum(n * nbq - 1, 0), base + col))
    out_blk = pl.BlockSpec((None, rows, cw), lambda r, n: (r, n, 0))
    return pl.pallas_call(
        _dilated_kernel,
        grid=(dil, length // rows),
        in_specs=[pl.BlockSpec(bias.shape, lambda r, n: (0, 0, 0), pipeline_mode=pl.Buffered(1)),
                  own(0), prev(1), own(1), prev(2), own(2)],
        out_specs=[out_blk, out_blk],
        out_shape=[jax.ShapeDtypeStruct((dil, length, cw), jnp.float32)] * 2,
        compiler_params=_params(("arbitrary", "arbitrary")),
        name=f"dilated_attn_g{group}",
    )(bias, zc, zc, zc, zc, zc)


def _merge_kernel(h_ref, zg_ref, oa_ref, ob_ref, o0_ref, l0_ref, o1_ref, l1_ref, o2_ref, l2_ref,
                  wa_ref, wb_ref, wc_ref, wo_ref, g_ref, mod_ref, out_ref, nat_ref):
    tm = h_ref.shape[0]

    def natural(ref, slot):
        dil = ref.shape[0]
        if dil == 1:
            return ref[0]
        ln = LANES_V7X
        chunks = ref.shape[2] // ln
        for r in range(dil):
            for c in range(chunks):
                nat_ref[slot * chunks + c, pl.ds(r, tm // dil, stride=dil), :] = ref[r, :, c * ln:(c + 1) * ln]
        return jnp.concatenate([nat_ref[slot * chunks + c] for c in range(chunks)], axis=1)

    l0, l1, l2 = natural(l0_ref, 0), natural(l1_ref, 0), natural(l2_ref, 1)
    o0, o1, o2 = natural(o0_ref, 0), natural(o1_ref, 2), natural(o2_ref, 3)
    mx = jnp.maximum(jnp.maximum(l0, l1), l2)
    e0, e1, e2 = jnp.exp(l0 - mx), jnp.exp(l1 - mx), jnp.exp(l2 - mx)
    oc = ((e0 * o0 + e1 * o1 + e2 * o2) / (e0 + e1 + e2)).astype(jnp.bfloat16)
    d = D_MODEL
    subs = range(0, tm, MERGE_SUB_ROWS)
    branches = [(_dot(oa_ref[pl.ds(r0, MERGE_SUB_ROWS), :], wa_ref[...]),
                 _dot(ob_ref[pl.ds(r0, MERGE_SUB_ROWS), :], wb_ref[...]),
                 _dot(oc[r0:r0 + MERGE_SUB_ROWS], wc_ref[...])) for r0 in subs]
    ys = []
    for r0, (ya, yb, yc) in zip(subs, branches):
        gates = jax.nn.sigmoid(zg_ref[pl.ds(r0, MERGE_SUB_ROWS), :].astype(jnp.float32))
        merged = gates[:, 0:d] * ya + gates[:, d:2 * d] * yb + gates[:, 2 * d:3 * d] * yc
        ys.append(_dot(merged.astype(jnp.bfloat16), wo_ref[...]))
    for r0, y in zip(subs, ys):
        rs = pl.ds(r0, MERGE_SUB_ROWS)
        out_ref[rs, :] = h_ref[rs, :] + mod_ref[2:3, :] * _rmsnorm(y, g_ref[...])


MERGE_SUB_ROWS = 256


def _merge_call(h, z, oa, ob, oc_lse, wa, wb, wc, wo, g, mod, layer):
    s, d = h.shape
    tm = 2 * MERGE_SUB_ROWS
    cw = C_HEADS_PER_GROUP * HEAD_DIM
    row = lambda w: pl.BlockSpec((tm, w), lambda i: (i, 0))
    res = lambda a: pl.BlockSpec((a.shape[0], tm // a.shape[0], cw), lambda i: (0, i, 0))
    full = lambda a: pl.BlockSpec(a.shape, lambda i: (0, 0), pipeline_mode=pl.Buffered(1))
    stacked = lambda a: pl.BlockSpec((None,) + a.shape[1:], lambda i: (layer, 0, 0), pipeline_mode=pl.Buffered(1))
    return pl.pallas_call(
        _merge_kernel,
        grid=(s // tm,),
        in_specs=[row(d), row(3 * d), row(oa.shape[1]), row(ob.shape[1])] + [res(a) for a in oc_lse]
        + [stacked(wa), stacked(wb), stacked(wc), stacked(wo), full(g), full(mod)],
        out_specs=row(d),
        out_shape=jax.ShapeDtypeStruct((s, d), jnp.float32),
        scratch_shapes=[pltpu.VMEM((4 * cw // LANES_V7X, tm, LANES_V7X), jnp.float32)],
        input_output_aliases={0: 0} if layer > 0 else {},
        compiler_params=_params(("arbitrary",)),
        name="merge_out_proj",
    )(h, z, oa, ob, *oc_lse, wa, wb, wc, wo, g, mod)


FFN_CHUNK = 1408


def _ffn_kernel(h_ref, g_pre_ref, g_post_ref, mod_ref, wg_ref, wu_ref, wd_ref, out_ref):
    x = h_ref[...]
    u = _rmsnorm(x, g_pre_ref[...]) * (1.0 + mod_ref[4:5, :]) + mod_ref[3:4, :]
    u = u.astype(jnp.bfloat16)
    y = jnp.zeros(x.shape, jnp.float32)
    for c0 in range(0, D_FF, FFN_CHUNK):
        gate = _dot(u, wg_ref[:, c0:c0 + FFN_CHUNK])
        up = _dot(u, wu_ref[:, c0:c0 + FFN_CHUNK])
        act = (gate * jax.nn.sigmoid(gate) * up).astype(jnp.bfloat16)
        y = y + _dot(act, wd_ref[c0:c0 + FFN_CHUNK, :])
    out_ref[...] = x + mod_ref[5:6, :] * _rmsnorm(y, g_post_ref[...])


def _ffn_call(h, g_pre, g_post, mod, wg, wu, wd, layer):
    s, d = h.shape
    tm = ROW_TILE
    row = pl.BlockSpec((tm, d), lambda i: (i, 0))
    full = lambda a: pl.BlockSpec(a.shape, lambda i: (0, 0), pipeline_mode=pl.Buffered(1))
    stacked = lambda a: pl.BlockSpec((None,) + a.shape[1:], lambda i: (layer, 0, 0), pipeline_mode=pl.Buffered(1))
    return pl.pallas_call(
        _ffn_kernel,
        grid=(s // tm,),
        in_specs=[row, full(g_pre), full(g_post), full(mod), stacked(wg), stacked(wu), stacked(wd)],
        out_specs=row,
        out_shape=jax.ShapeDtypeStruct((s, d), jnp.float32),
        input_output_aliases={0: 0},
        compiler_params=_params(("arbitrary",)),
        name="swiglu_ffn",
    )(h, g_pre, g_post, mod, wg, wu, wd)


def _permute_in_proj(w_in):
    hd = HEAD_DIM
    o_bq = 3 * A_HEADS * hd
    o_bkv = o_bq + B_HEADS * hd
    o_c = o_bkv + 2 * B_KV_HEADS * hd
    o_g = o_c + 3 * C_HEADS * hd
    cw = C_HEADS_PER_GROUP * hd
    sl = lambda a, b: w_in[:, :, a:b]
    parts = [sl(o_g, o_g + 3 * D_MODEL), sl(0, A_HEADS * hd) * (QK_SCALE * LOG2E), sl(A_HEADS * hd, o_bq)]
    parts += [sl(o_bq + hh * hd, o_bq + (hh + 1) * hd) * QK_SCALE for hh in B_HEAD_ORDER]
    parts += [sl(o_bkv, o_c)]
    for g in range(len(C_GROUPS)):
        parts += [sl(o_c + g * 3 * cw, o_c + g * 3 * cw + cw) * QK_SCALE,
                  sl(o_c + g * 3 * cw + cw, o_c + (g + 1) * 3 * cw)]
    out = jnp.concatenate([part.astype(jnp.bfloat16) for part in parts], axis=2)
    assert out.shape[2] == IN_WIDTH
    return out


def kernel(x, c, w_ada, b_ada, g_pre_mix, g_post_mix, w_in, sinks, w_br_a, w_br_b, w_br_c,
           w_out, g_pre_ffn, g_post_ffn, w_gate, w_up, w_down):
    bn, s, d = x.shape
    assert bn == 1 and d == D_MODEL and s % SEQ_MULTIPLE == 0
    bf = jnp.bfloat16
    hd = HEAD_DIM
    w_in_p = _permute_in_proj(w_in)
    w_br_b_p = jnp.concatenate([w_br_b[:, hh * hd:(hh + 1) * hd] for hh in B_HEAD_ORDER], axis=1).astype(bf)
    w_br_a_b, w_br_c_b, w_out_b = w_br_a.astype(bf), w_br_c.astype(bf), w_out.astype(bf)
    w_gate_b, w_up_b, w_down_b = w_gate.astype(bf), w_up.astype(bf), w_down.astype(bf)

    mod_all = _ada_call(c, w_ada, b_ada).reshape(DEPTH, 6, d)
    h = x.reshape(s, d)
    for l in range(DEPTH):
        mod = mod_all[l]
        z, zc1, zc2 = _in_proj_call(h, g_pre_mix[l].reshape(1, d), mod, w_in_p, l)
        oa = _moba_call(z)
        ob = _swa_call(z, sinks[l])
        oc_lse = (_dilated_call(z.reshape(1, s, Z_WIDTH), 0, Z_C0)
                  + _dilated_call(zc1, 1, 0) + _dilated_call(zc2, 2, 0))
        h = _merge_call(h, z, oa, ob, oc_lse, w_br_a_b, w_br_b_p, w_br_c_b, w_out_b,
                        g_post_mix[l].reshape(1, d), mod, l)
        h = _ffn_call(h, g_pre_ffn[l].reshape(1, d), g_post_ffn[l].reshape(1, d), mod,
                      w_gate_b, w_up_b, w_down_b, l)
    return h.reshape(bn, s, d)
```

```python
import numpy as np
import jax
import jax.numpy as jnp
from jax import lax
from jax.experimental import pallas as pl
from jax.experimental.pallas import tpu as pltpu

D_MODEL = 1024
DEPTH = 4
HEAD_DIM = 64
A_HEADS = 8
MOBA_BLOCK = 256
MOBA_TOPK = 3
B_HEADS = 8
B_KV_HEADS = 2
B_WINDOW = 128
C_GROUPS = ((128, 1), (512, 4), (2048, 16))
C_HEADS_PER_GROUP = 4
C_HEADS = len(C_GROUPS) * C_HEADS_PER_GROUP
BAND = 128
D_FF = 2816
N_ALIBI_HEADS = A_HEADS + B_HEADS + C_HEADS
SEQ_MULTIPLE = 2048
RMS_EPS = 1e-6

LANES_V7X = 128
BF16_SUBLANES_V7X = 16
VMEM_LIMIT_BYTES_V7X = 56 * 1024 * 1024

ROW_TILE = 512
C_WIDTH = 3 * C_HEADS_PER_GROUP * HEAD_DIM

Z_GATES = 0
Z_A = 3 * D_MODEL
Z_BQ = Z_A + 3 * A_HEADS * HEAD_DIM
Z_BK = Z_BQ + B_HEADS * HEAD_DIM
Z_BV = Z_BK + B_KV_HEADS * HEAD_DIM
Z_C0 = Z_BV + B_KV_HEADS * HEAD_DIM
Z_WIDTH = Z_C0 + C_WIDTH
IN_WIDTH = Z_WIDTH + (len(C_GROUPS) - 1) * C_WIDTH
B_HEAD_ORDER = (0, 4, 1, 5, 2, 6, 3, 7)

NEG_BIG = -1e30
QK_SCALE = HEAD_DIM ** -0.5
LOG2E = 1.4426950408889634


def _alibi_slopes():
    n = N_ALIBI_HEADS
    return [float(2.0 ** (-8.0 * (i + 1) / n)) for i in range(n)]


_SLOPES = _alibi_slopes()
SLOPES_B = _SLOPES[:B_HEADS]
SLOPES_C = _SLOPES[B_HEADS:B_HEADS + C_HEADS]
SLOPES_A = _SLOPES[B_HEADS + C_HEADS:]


def _dot(a, b):
    return jnp.dot(a, b, preferred_element_type=jnp.float32)


def _dot_nt(a, b):
    return lax.dot_general(a, b, (((1,), (1,)), ((), ())), preferred_element_type=jnp.float32)


def _params(semantics):
    return pltpu.CompilerParams(dimension_semantics=semantics, vmem_limit_bytes=VMEM_LIMIT_BYTES_V7X)


def _rmsnorm(x, g):
    return x * lax.rsqrt(jnp.mean(x * x, axis=-1, keepdims=True) + RMS_EPS) * g


def _ada_kernel(c_ref, w_ref, b_ref, o_ref):
    c = c_ref[...]
    sc = c * jax.nn.sigmoid(c)
    o_ref[...] = jnp.sum(w_ref[...] * sc, axis=0, keepdims=True) + b_ref[...]


def _ada_call(c, w_ada, b_ada):
    depth, d, n = w_ada.shape
    tn = 1536
    return pl.pallas_call(
        _ada_kernel,
        grid=(depth, n // tn),
        in_specs=[
            pl.BlockSpec((d, 1), lambda l, j: (0, 0)),
            pl.BlockSpec((None, d, tn), lambda l, j: (l, 0, j)),
            pl.BlockSpec((None, 1, tn), lambda l, j: (l, 0, j)),
        ],
        out_specs=pl.BlockSpec((None, 1, tn), lambda l, j: (l, 0, j)),
        out_shape=jax.ShapeDtypeStruct((depth, 1, n), jnp.float32),
        compiler_params=_params(("arbitrary", "arbitrary")),
        name="adaln_mod",
    )(c.reshape(d, 1), w_ada, b_ada.reshape(depth, 1, n))


IN_PROJ_CHUNK = 512


def _in_proj_kernel(h_ref, g_ref, mod_ref, w_ref, z_ref, zc1_ref, zc2_ref, tmp_ref):
    x = h_ref[...]
    u = _rmsnorm(x, g_ref[...]) * (1.0 + mod_ref[1:2, :]) + mod_ref[0:1, :]
    u = u.astype(jnp.bfloat16)
    for c0 in range(0, Z_WIDTH, IN_PROJ_CHUNK):
        z_ref[:, c0:c0 + IN_PROJ_CHUNK] = _dot(u, w_ref[:, c0:c0 + IN_PROJ_CHUNK]).astype(z_ref.dtype)
    for g, out_ref in ((1, zc1_ref), (2, zc2_ref)):
        c0 = Z_WIDTH + (g - 1) * C_WIDTH
        res = _dot(u, w_ref[:, c0:c0 + C_WIDTH])
        dil = C_GROUPS[g][1]
        rows = x.shape[0] // dil
        ln = LANES_V7X
        for c in range(C_WIDTH // ln):
            tmp_ref[c] = res[:, c * ln:(c + 1) * ln]
        for r in range(dil):
            for c in range(C_WIDTH // ln):
                out_ref[r, :, c * ln:(c + 1) * ln] = tmp_ref[c, pl.ds(r, rows, stride=dil), :].astype(out_ref.dtype)


def _in_proj_call(h, g, mod, w, layer):
    s, d = h.shape
    tm = ROW_TILE
    d1, d2 = C_GROUPS[1][1], C_GROUPS[2][1]
    assert tm % (d2 * BF16_SUBLANES_V7X) == 0
    return pl.pallas_call(
        _in_proj_kernel,
        grid=(s // tm,),
        in_specs=[
            pl.BlockSpec((tm, d), lambda i: (i, 0)),
            pl.BlockSpec((1, d), lambda i: (0, 0)),
            pl.BlockSpec((6, d), lambda i: (0, 0)),
            pl.BlockSpec((None, d, IN_WIDTH), lambda i: (layer, 0, 0), pipeline_mode=pl.Buffered(1)),
        ],
        out_specs=[
            pl.BlockSpec((tm, Z_WIDTH), lambda i: (i, 0)),
            pl.BlockSpec((d1, tm // d1, C_WIDTH), lambda i: (0, i, 0)),
            pl.BlockSpec((d2, tm // d2, C_WIDTH), lambda i: (0, i, 0)),
        ],
        out_shape=[
            jax.ShapeDtypeStruct((s, Z_WIDTH), jnp.bfloat16),
            jax.ShapeDtypeStruct((d1, s // d1, C_WIDTH), jnp.bfloat16),
            jax.ShapeDtypeStruct((d2, s // d2, C_WIDTH), jnp.bfloat16),
        ],
        scratch_shapes=[pltpu.VMEM((C_WIDTH // LANES_V7X, tm, LANES_V7X), jnp.float32)],
        compiler_params=_params(("arbitrary",)),
        name="in_proj",
    )(h, g, mod, w)


SUBLANES_V7X = 8
MOBA_V_ROWS = HEAD_DIM + BF16_SUBLANES_V7X
MOBA_GROUPS = MOBA_BLOCK // SUBLANES_V7X
MOBA_UNROLL_LOG2 = 1
MOBA_UNROLL = 1 << MOBA_UNROLL_LOG2
MOBA_TILES = 2


def _all_sublanes_max(x):
    for shift in (4, 2, 1):
        x = jnp.maximum(x, pltpu.roll(x, shift, axis=0))
    return x


def _moba_lanes():
    lane = lax.broadcasted_iota(jnp.int32, (MOBA_BLOCK, LANES_V7X), 1)
    own = [lane < HEAD_DIM, lane >= HEAD_DIM]
    spare = [HEAD_DIM, 0]
    bias_lanes = [(lane == spare[h]) | (lane == spare[h] + 1) for h in (0, 1)]
    return lane, own, spare, bias_lanes


def _moba_kernel(slopes_ref, q_ref, k_ref, v_ref, o_ref, vt_ref, km_ref, ka_ref, sel_ref, s_ref, e_ref, st_ref, acc_ref):
    p = pl.program_id(0)
    nblk = vt_ref.shape[0]
    blk = MOBA_BLOCK
    half = HEAD_DIM
    _, own, _, bias_lanes = _moba_lanes()
    ones = jnp.ones((BF16_SUBLANES_V7X, blk), jnp.bfloat16)

    def prepare(j, carry):
        rows = pl.ds(pl.multiple_of(j * blk, blk), blk)
        vt = v_ref[rows, :].astype(jnp.float32).T.astype(jnp.bfloat16)
        k2 = k_ref[rows, :]
        km_ref[pl.ds(j, 1), :] = jnp.mean(k2.astype(jnp.float32), axis=0, keepdims=True)
        k2f = k2.astype(jnp.float32)
        pos = lax.broadcasted_iota(jnp.int32, k2.shape, 0).astype(jnp.float32)
        for h in (0, 1):
            vt_ref[j, h, 0:half, :] = vt[h * half:(h + 1) * half, :]
            vt_ref[j, h, half:, :] = ones
            ka = jnp.where(own[h], k2f, jnp.where(bias_lanes[h], pos, 0.0))
            ka_ref[h, rows, :] = ka.astype(jnp.bfloat16)
        return carry

    lax.fori_loop(0, nblk, prepare, 0)
    for w in range(MOBA_TILES):
        _moba_select(w, w, q_ref, km_ref, sel_ref)

    def tiles(t, carry):
        _moba_tiles(t, p, slopes_ref, q_ref, o_ref, vt_ref, km_ref, ka_ref, sel_ref, s_ref, e_ref, st_ref, acc_ref)
        return carry

    lax.fori_loop(0, nblk // MOBA_TILES, tiles, 0)


def _moba_select(i, w, q_ref, km_ref, sel_ref):
    nblk, blk = sel_ref.shape[1], MOBA_BLOCK
    _, own, _, _ = _moba_lanes()
    q2 = q_ref[pl.ds(pl.multiple_of(i * blk, blk), blk), :]
    blk_id = lax.broadcasted_iota(jnp.int32, (nblk, blk), 0)
    km = km_ref[...]
    km_hi = km.astype(jnp.bfloat16)
    km_lo = (km - km_hi.astype(jnp.float32)).astype(jnp.bfloat16)
    for h in (0, 1):
        qz = jnp.where(own[h], q2, jnp.zeros_like(q2))
        gate = _dot_nt(km_hi, qz) + _dot_nt(km_lo, qz)
        gate = jnp.where(blk_id < i, gate, -jnp.inf)
        sel = jnp.full((nblk, blk), NEG_BIG, jnp.float32)
        for _ in range(MOBA_TOPK):
            mx = jnp.max(gate, axis=0, keepdims=True)
            cand = (gate == mx) & (mx > -jnp.inf)
            idx = jnp.min(jnp.where(cand, blk_id, nblk), axis=0, keepdims=True)
            chosen = blk_id == idx
            sel = jnp.where(chosen, 0.0, sel)
            gate = jnp.where(chosen, -jnp.inf, gate)
        sel_ref[2 * w + h] = sel


def _moba_tiles(t, p, slopes_ref, q_ref, o_ref, vt_ref, km_ref, ka_ref, sel_ref, s_ref, e_ref, st_ref, acc_ref):
    nblk = vt_ref.shape[0]
    blk = MOBA_BLOCK
    half = HEAD_DIM
    sub = SUBLANES_V7X
    tile3 = (MOBA_GROUPS, sub, blk)
    key_pos = lax.broadcasted_iota(jnp.int32, tile3, 0) * sub + lax.broadcasted_iota(jnp.int32, tile3, 1)
    qry_pos = lax.broadcasted_iota(jnp.int32, tile3, 2)
    lane, own, spare, _ = _moba_lanes()
    streams = range(2 * MOBA_TILES)
    tile_of = [t * MOBA_TILES + st // 2 for st in streams]
    head_of = [st % 2 for st in streams]
    last_tile = tile_of[-1]

    qh = []
    for st in streams:
        h = head_of[st]
        q2f = q_ref[pl.ds(pl.multiple_of(tile_of[st] * blk, blk), blk), :].astype(jnp.float32)
        hi = slopes_ref[A_HEADS + 2 * p + h]
        lo = slopes_ref[2 * A_HEADS + 2 * p + h]
        extra = jnp.where(lane == spare[h], hi, jnp.where(lane == spare[h] + 1, lo, 0.0))
        qh.append(jnp.where(own[h], q2f, extra).astype(jnp.bfloat16))

    unroll = MOBA_UNROLL
    acc_groups = MOBA_V_ROWS // sub

    def past_block(n):
        return jnp.clip(n - 1, 0, nblk - 1)

    def item_block(n, st, own_first):
        return tile_of[st] if own_first else past_block(n)

    def issue_scores(items, own_first):
        out = []
        for u, n in enumerate(items):
            per_stream = []
            for st in streams:
                b = item_block(n, st, own_first and u == 0)
                rows = pl.ds(pl.multiple_of(b * blk, blk), blk)
                per_stream.append(_dot_nt(ka_ref[head_of[st], rows, :], qh[st]).reshape(tile3))
            out.append(per_stream)
        return out

    def stage_scores(raws, slot, items, own_first):
        for u, n in enumerate(items):
            for st in streams:
                s = raws[u][st]
                if own_first and u == 0:
                    s = jnp.where(qry_pos >= key_pos, s, NEG_BIG)
                    row = jnp.zeros((sub, blk), jnp.float32)
                else:
                    b = past_block(n)
                    gap = ((tile_of[st] - b) * blk).astype(jnp.float32)
                    row = sel_ref[st, pl.ds(b, 1), :] - slopes_ref[2 * p + head_of[st]] * gap
                    row = jnp.broadcast_to(jnp.where(n <= tile_of[st], row, NEG_BIG), (sub, blk))
                s_ref[slot, u, st] = s
                st_ref[slot, u, st, 0] = _all_sublanes_max(jnp.max(s, axis=0)) + row
                st_ref[slot, u, st, 1] = row

    def exponentiate(group_slot, ms):
        new_ms, alphas = [], []
        for st in streams:
            m_new = ms[st]
            for u in range(unroll):
                m_new = jnp.maximum(m_new, st_ref[group_slot, u, st, 0])
            alphas.append(jnp.exp2(ms[st] - m_new))
            new_ms.append(m_new)
            for u in range(unroll):
                e = jnp.exp2(s_ref[group_slot, u, st] - (m_new - st_ref[group_slot, u, st, 1])[None])
                e_ref[group_slot, u, st] = e.reshape(blk, blk).astype(jnp.bfloat16)
        return new_ms, alphas

    def accumulate(group, slot, alphas):
        for st in streams:
            acc = alphas[st][None] * acc_ref[st]
            for u in range(unroll):
                n = group * unroll + u
                vblock = jnp.where(n == 0, tile_of[st], past_block(n))
                acc = acc + _dot(vt_ref[vblock, head_of[st]], e_ref[slot, u, st]).reshape(acc_groups, sub, blk)
            acc_ref[st] = acc

    def score_group(group, slot, own_first):
        items = [group * unroll + u for u in range(unroll)]
        raws = issue_scores(items, own_first)
        return lambda: stage_scores(raws, slot, items, own_first)

    def tick(g, slot, ms, alphas):
        accumulate(g - 2, slot, alphas)
        finish_scores = score_group(g, slot, False)
        ms, alphas = exponentiate(1 - slot, ms)
        finish_scores()
        return ms, alphas

    n_st = len(streams)
    pairs = jnp.maximum(lax.shift_right_logical(last_tile + 2 * unroll, MOBA_UNROLL_LOG2 + 1), 1)
    ms = [jnp.full((sub, blk), -jnp.inf, jnp.float32) for _ in streams]
    acc_ref[...] = jnp.zeros(acc_ref.shape, jnp.float32)
    score_group(0, 0, True)()
    finish_scores = score_group(1, 1, False)
    ms, alphas = exponentiate(0, ms)
    finish_scores()

    def two_ticks(k, carry):
        ms, alphas = tick(2 * k, 0, list(carry[:n_st]), list(carry[n_st:]))
        ms, alphas = tick(2 * k + 1, 1, ms, alphas)
        return tuple(ms + alphas)

    carry = lax.fori_loop(1, pairs, two_ticks, tuple(ms + alphas))
    ms, alphas = list(carry[:n_st]), list(carry[n_st:])
    accumulate(2 * pairs - 2, 0, alphas)
    ms, alphas = exponentiate(1, ms)
    for w in range(MOBA_TILES):
        _moba_select(jnp.minimum(tile_of[2 * w] + MOBA_TILES, nblk - 1), w, q_ref, km_ref, sel_ref)
    accumulate(2 * pairs - 1, 1, alphas)
    for w in range(MOBA_TILES):
        accs = [acc_ref[2 * w + h] for h in (0, 1)]
        o_t = jnp.concatenate([(acc[0:half // sub] / acc[half // sub][None]).reshape(half, blk) for acc in accs],
                              axis=0)
        o_ref[pl.ds(pl.multiple_of(tile_of[2 * w] * blk, blk), blk), :] = o_t.T.astype(o_ref.dtype)


def _moba_slopes():
    full = np.asarray([sl * LOG2E for sl in SLOPES_A], np.float32)
    hi = full.astype(jnp.bfloat16).astype(np.float32)
    lo = (full - hi).astype(jnp.bfloat16).astype(np.float32)
    return jnp.asarray(np.concatenate([full, hi, lo]))


def _moba_call(z):
    s = z.shape[0]
    nblk = s // MOBA_BLOCK
    pairs = A_HEADS // 2
    ln = LANES_V7X
    qb, kb, vb = Z_A // ln, Z_A // ln + pairs, Z_A // ln + 2 * pairs
    return pl.pallas_call(
        _moba_kernel,
        grid=(pairs,),
        in_specs=[
            pl.BlockSpec(memory_space=pltpu.SMEM),
            pl.BlockSpec((s, ln), lambda p: (0, qb + p)),
            pl.BlockSpec((s, ln), lambda p: (0, kb + p), pipeline_mode=pl.Buffered(1)),
            pl.BlockSpec((s, ln), lambda p: (0, vb + p), pipeline_mode=pl.Buffered(1)),
        ],
        out_specs=pl.BlockSpec((s, ln), lambda p: (0, p)),
        out_shape=jax.ShapeDtypeStruct((s, A_HEADS * HEAD_DIM), jnp.bfloat16),
        scratch_shapes=[
            pltpu.VMEM((nblk, 2, MOBA_V_ROWS, MOBA_BLOCK), jnp.bfloat16),
            pltpu.VMEM((nblk, ln), jnp.float32),
            pltpu.VMEM((2, s, ln), jnp.bfloat16),
            pltpu.VMEM((2 * MOBA_TILES, nblk, MOBA_BLOCK), jnp.float32),
            pltpu.VMEM((2, MOBA_UNROLL, 2 * MOBA_TILES, MOBA_GROUPS, SUBLANES_V7X, MOBA_BLOCK), jnp.float32),
            pltpu.VMEM((2, MOBA_UNROLL, 2 * MOBA_TILES, MOBA_BLOCK, MOBA_BLOCK), jnp.bfloat16),
            pltpu.VMEM((2, MOBA_UNROLL, 2 * MOBA_TILES, 2, SUBLANES_V7X, MOBA_BLOCK), jnp.float32),
            pltpu.VMEM((2 * MOBA_TILES, MOBA_V_ROWS // SUBLANES_V7X, SUBLANES_V7X, MOBA_BLOCK), jnp.float32),
        ],
        compiler_params=_params(("arbitrary",)),
        name="moba_attn",
    )(_moba_slopes(), z, z, z)


def _blocks_per_step(nb):
    return max(c for c in (4, 2, 1) if nb % c == 0)


def _band_penalty_table(slope_dils, max_steps):
    steps = np.arange(BAND)[:, None] + BAND - np.arange(2 * BAND)[None, :]
    inside = (steps >= 0) & (steps <= max_steps)
    table = [np.where(inside, np.float32(sd) * steps.astype(np.float32), np.float32(-NEG_BIG)) for sd in slope_dils]
    return jnp.asarray(np.stack(table).astype(np.float32))


def _band_no_prev(first_step):
    kj = lax.broadcasted_iota(jnp.int32, (BAND, 2 * BAND), 1)
    return jnp.logical_and(first_step, kj < BAND)


def _band_units(units, lookahead):
    raws = {j: _dot_nt(units[j][0], units[j][1]) for j in range(min(lookahead, len(units)))}
    outs, lses = [], []
    for j, (_, _, vv, penalty, drop, sink) in enumerate(units):
        ahead = j + lookahead
        if ahead < len(units):
            raws[ahead] = _dot_nt(units[ahead][0], units[ahead][1])
        s = raws.pop(j) - penalty
        if drop is not None:
            s = jnp.where(drop, NEG_BIG, s)
        m = jnp.max(s, axis=1, keepdims=True)
        if sink is not None:
            m = jnp.maximum(m, sink)
        e = jnp.exp(s - m)
        denom = jnp.sum(e, axis=1, keepdims=True)
        if sink is not None:
            denom = denom + jnp.exp(sink - m)
        outs.append(_dot(e.astype(jnp.bfloat16), vv) / denom)
        lses.append(m + jnp.log(denom))
    return outs, lses


def _swa_kernel(sinks_ref, bias_ref, q_ref, kp_ref, ko_ref, vp_ref, vo_ref, o_ref):
    n = pl.program_id(0)
    nbq = q_ref.shape[0] // BAND
    no_prev = _band_no_prev(n == 0)
    k_all = jnp.concatenate([kp_ref[...], ko_ref[...]], axis=0)
    v_all = jnp.concatenate([vp_ref[...], vo_ref[...]], axis=0)
    ln = LANES_V7X
    lane = lax.broadcasted_iota(jnp.int32, (BAND, ln), 1)
    low = lane < HEAD_DIM
    units = []
    for qb in range(nbq):
        kk = k_all[qb * BAND:(qb + 2) * BAND]
        vv = v_all[qb * BAND:(qb + 2) * BAND]
        drop = no_prev if qb == 0 else None
        for b in range(B_HEADS // 2):
            q2 = q_ref[qb * BAND:(qb + 1) * BAND, b * ln:(b + 1) * ln]
            for c in range(2):
                head = B_HEAD_ORDER[2 * b + c]
                qm = jnp.where(low if c == 0 else jnp.logical_not(low), q2, jnp.zeros_like(q2))
                units.append((qm, kk, vv, bias_ref[2 * b + c], drop, sinks_ref[head]))
    outs, _ = _band_units(units, B_HEADS)
    for qb in range(nbq):
        for b in range(B_HEADS // 2):
            j = qb * B_HEADS + 2 * b
            o_ref[qb * BAND:(qb + 1) * BAND, b * ln:(b + 1) * ln] = (
                jnp.where(low, outs[j], outs[j + 1]).astype(o_ref.dtype))


def _swa_call(z, sinks):
    s = z.shape[0]
    nbq = _blocks_per_step(s // BAND)
    rows = nbq * BAND
    ln = LANES_V7X
    qw = B_HEADS * HEAD_DIM
    prev = lambda n: jnp.maximum(n * nbq - 1, 0)
    bias = _band_penalty_table([SLOPES_B[head] for head in B_HEAD_ORDER], B_WINDOW - 1)
    return pl.pallas_call(
        _swa_kernel,
        grid=(s // rows,),
        in_specs=[
            pl.BlockSpec(memory_space=pltpu.SMEM),
            pl.BlockSpec(bias.shape, lambda n: (0, 0, 0), pipeline_mode=pl.Buffered(1)),
            pl.BlockSpec((rows, qw), lambda n: (n, Z_BQ // qw)),
            pl.BlockSpec((BAND, ln), lambda n: (prev(n), Z_BK // ln)),
            pl.BlockSpec((rows, ln), lambda n: (n, Z_BK // ln)),
            pl.BlockSpec((BAND, ln), lambda n: (prev(n), Z_BV // ln)),
            pl.BlockSpec((rows, ln), lambda n: (n, Z_BV // ln)),
        ],
        out_specs=pl.BlockSpec((rows, qw), lambda n: (n, 0)),
        out_shape=jax.ShapeDtypeStruct((s, qw), jnp.bfloat16),
        compiler_params=_params(("arbitrary",)),
        name="swa_attn",
    )(sinks, bias, z, z, z, z, z)


def _dilated_kernel(bias_ref, q_ref, kp_ref, ko_ref, vp_ref, vo_ref, o_ref, lse_ref):
    n = pl.program_id(1)
    nbq = q_ref.shape[0] // BAND
    no_prev = _band_no_prev(n == 0)
    k_all = jnp.concatenate([kp_ref[...], ko_ref[...]], axis=0)
    v_all = jnp.concatenate([vp_ref[...], vo_ref[...]], axis=0)
    cw = q_ref.shape[1]
    lane = lax.broadcasted_iota(jnp.int32, (BAND, cw), 1)
    heads = range(C_HEADS_PER_GROUP)
    mine = [(lane >= h * HEAD_DIM) & (lane < (h + 1) * HEAD_DIM) for h in heads]
    biases = [bias_ref[h] for h in heads]
    units = []
    for qb in range(nbq):
        q4 = q_ref[qb * BAND:(qb + 1) * BAND, :]
        kk = k_all[qb * BAND:(qb + 2) * BAND]
        vv = v_all[qb * BAND:(qb + 2) * BAND]
        drop = no_prev if qb == 0 else None
        for h in heads:
            units.append((jnp.where(mine[h], q4, jnp.zeros_like(q4)), kk, vv, biases[h], drop, None))
    outs, lses = _band_units(units, C_HEADS_PER_GROUP)
    for qb in range(nbq):
        o_all = jnp.zeros((BAND, cw), jnp.float32)
        lse_all = jnp.zeros((BAND, cw), jnp.float32)
        for h in heads:
            j = qb * C_HEADS_PER_GROUP + h
            o_all = jnp.where(mine[h], outs[j], o_all)
            lse_all = jnp.where(mine[h], lses[j], lse_all)
        o_ref[qb * BAND:(qb + 1) * BAND, :] = o_all
        lse_ref[qb * BAND:(qb + 1) * BAND, :] = lse_all


def _dilated_call(zc, group, col0):
    window, dil = C_GROUPS[group]
    assert zc.shape[0] == dil
    length = zc.shape[1]
    nbq = _blocks_per_step(length // BAND)
    rows = nbq * BAND
    cw = C_HEADS_PER_GROUP * HEAD_DIM
    base = col0 // cw
    slopes = [SLOPES_C[group * C_HEADS_PER_GROUP + h] * dil for h in range(C_HEADS_PER_GROUP)]
    bias = _band_penalty_table(slopes, window // dil)
    own = lambda col: pl.BlockSpec((None, rows, cw), lambda r, n: (r, n, base + col))
    prev = lambda col: pl.BlockSpec((None, BAND, cw), lambda r, n: (r, jnp.maximum(n * nbq - 1, 0), base + col))
    out_blk = pl.BlockSpec((None, rows, cw), lambda r, n: (r, n, 0))
    return pl.pallas_call(
        _dilated_kernel,
        grid=(dil, length // rows),
        in_specs=[pl.BlockSpec(bias.shape, lambda r, n: (0, 0, 0), pipeline_mode=pl.Buffered(1)),
                  own(0), prev(1), own(1), prev(2), own(2)],
        out_specs=[out_blk, out_blk],
        out_shape=[jax.ShapeDtypeStruct((dil, length, cw), jnp.float32)] * 2,
        compiler_params=_params(("arbitrary", "arbitrary")),
        name=f"dilated_attn_g{group}",
    )(bias, zc, zc, zc, zc, zc)


def _merge_kernel(h_ref, zg_ref, oa_ref, ob_ref, o0_ref, l0_ref, o1_ref, l1_ref, o2_ref, l2_ref,
                  wa_ref, wb_ref, wc_ref, wo_ref, g_ref, mod_ref, out_ref, nat_ref):
    tm = h_ref.shape[0]

    def natural(ref, slot):
        dil = ref.shape[0]
        if dil == 1:
            return ref[0]
        ln = LANES_V7X
        chunks = ref.shape[2] // ln
        for r in range(dil):
            for c in range(chunks):
                nat_ref[slot * chunks + c, pl.ds(r, tm // dil, stride=dil), :] = ref[r, :, c * ln:(c + 1) * ln]
        return jnp.concatenate([nat_ref[slot * chunks + c] for c in range(chunks)], axis=1)

    l0, l1, l2 = natural(l0_ref, 0), natural(l1_ref, 0), natural(l2_ref, 1)
    o0, o1, o2 = natural(o0_ref, 0), natural(o1_ref, 2), natural(o2_ref, 3)
    mx = jnp.maximum(jnp.maximum(l0, l1), l2)
    e0, e1, e2 = jnp.exp(l0 - mx), jnp.exp(l1 - mx), jnp.exp(l2 - mx)
    oc = ((e0 * o0 + e1 * o1 + e2 * o2) / (e0 + e1 + e2)).astype(jnp.bfloat16)
    d = D_MODEL
    subs = range(0, tm, MERGE_SUB_ROWS)
    wa, wb, wc, wo = [ref[...].astype(jnp.bfloat16) for ref in (wa_ref, wb_ref, wc_ref, wo_ref)]
    branches = [(_dot(oa_ref[pl.ds(r0, MERGE_SUB_ROWS), :], wa),
                 _dot(ob_ref[pl.ds(r0, MERGE_SUB_ROWS), :], wb),
                 _dot(oc[r0:r0 + MERGE_SUB_ROWS], wc)) for r0 in subs]
    ys = []
    for r0, (ya, yb, yc) in zip(subs, branches):
        gates = jax.nn.sigmoid(zg_ref[pl.ds(r0, MERGE_SUB_ROWS), :].astype(jnp.float32))
        merged = gates[:, 0:d] * ya + gates[:, d:2 * d] * yb + gates[:, 2 * d:3 * d] * yc
        ys.append(_dot(merged.astype(jnp.bfloat16), wo))
    for r0, y in zip(subs, ys):
        rs = pl.ds(r0, MERGE_SUB_ROWS)
        out_ref[rs, :] = h_ref[rs, :] + mod_ref[2:3, :] * _rmsnorm(y, g_ref[...])


MERGE_SUB_ROWS = 256


def _merge_call(h, z, oa, ob, oc_lse, wa, wb, wc, wo, g, mod, layer):
    s, d = h.shape
    tm = 2 * MERGE_SUB_ROWS
    cw = C_HEADS_PER_GROUP * HEAD_DIM
    row = lambda w: pl.BlockSpec((tm, w), lambda i: (i, 0))
    res = lambda a: pl.BlockSpec((a.shape[0], tm // a.shape[0], cw), lambda i: (0, i, 0))
    full = lambda a: pl.BlockSpec(a.shape, lambda i: (0, 0), pipeline_mode=pl.Buffered(1))
    stacked = lambda a: pl.BlockSpec((None,) + a.shape[1:], lambda i: (layer, 0, 0), pipeline_mode=pl.Buffered(1))
    return pl.pallas_call(
        _merge_kernel,
        grid=(s // tm,),
        in_specs=[row(d), row(3 * d), row(oa.shape[1]), row(ob.shape[1])] + [res(a) for a in oc_lse]
        + [stacked(wa), stacked(wb), stacked(wc), stacked(wo), full(g), full(mod)],
        out_specs=row(d),
        out_shape=jax.ShapeDtypeStruct((s, d), jnp.float32),
        scratch_shapes=[pltpu.VMEM((4 * cw // LANES_V7X, tm, LANES_V7X), jnp.float32)],
        input_output_aliases={0: 0} if layer > 0 else {},
        compiler_params=_params(("arbitrary",)),
        name="merge_out_proj",
    )(h, z, oa, ob, *oc_lse, wa, wb, wc, wo, g, mod)


FFN_CHUNK = 1408


def _ffn_kernel(h_ref, g_pre_ref, g_post_ref, mod_ref, wg_ref, wu_ref, wd_ref, out_ref):
    x = h_ref[...]
    u = _rmsnorm(x, g_pre_ref[...]) * (1.0 + mod_ref[4:5, :]) + mod_ref[3:4, :]
    u = u.astype(jnp.bfloat16)
    y = jnp.zeros(x.shape, jnp.float32)
    for c0 in range(0, D_FF, FFN_CHUNK):
        gate = _dot(u, wg_ref[:, c0:c0 + FFN_CHUNK].astype(jnp.bfloat16))
        up = _dot(u, wu_ref[:, c0:c0 + FFN_CHUNK].astype(jnp.bfloat16))
        act = (gate * jax.nn.sigmoid(gate) * up).astype(jnp.bfloat16)
        y = y + _dot(act, wd_ref[c0:c0 + FFN_CHUNK, :].astype(jnp.bfloat16))
    out_ref[...] = x + mod_ref[5:6, :] * _rmsnorm(y, g_post_ref[...])


def _ffn_call(h, g_pre, g_post, mod, wg, wu, wd, layer):
    s, d = h.shape
    tm = ROW_TILE
    row = pl.BlockSpec((tm, d), lambda i: (i, 0))
    full = lambda a: pl.BlockSpec(a.shape, lambda i: (0, 0), pipeline_mode=pl.Buffered(1))
    stacked = lambda a: pl.BlockSpec((None,) + a.shape[1:], lambda i: (layer, 0, 0), pipeline_mode=pl.Buffered(1))
    return pl.pallas_call(
        _ffn_kernel,
        grid=(s // tm,),
        in_specs=[row, full(g_pre), full(g_post), full(mod), stacked(wg), stacked(wu), stacked(wd)],
        out_specs=row,
        out_shape=jax.ShapeDtypeStruct((s, d), jnp.float32),
        input_output_aliases={0: 0},
        compiler_params=_params(("arbitrary",)),
        name="swiglu_ffn",
    )(h, g_pre, g_post, mod, wg, wu, wd)


def _permute_in_proj(w_in):
    hd = HEAD_DIM
    o_bq = 3 * A_HEADS * hd
    o_bkv = o_bq + B_HEADS * hd
    o_c = o_bkv + 2 * B_KV_HEADS * hd
    o_g = o_c + 3 * C_HEADS * hd
    cw = C_HEADS_PER_GROUP * hd
    sl = lambda a, b: w_in[:, :, a:b]
    parts = [sl(o_g, o_g + 3 * D_MODEL), sl(0, A_HEADS * hd) * (QK_SCALE * LOG2E), sl(A_HEADS * hd, o_bq)]
    parts += [sl(o_bq + hh * hd, o_bq + (hh + 1) * hd) * QK_SCALE for hh in B_HEAD_ORDER]
    parts += [sl(o_bkv, o_c)]
    for g in range(len(C_GROUPS)):
        parts += [sl(o_c + g * 3 * cw, o_c + g * 3 * cw + cw) * QK_SCALE,
                  sl(o_c + g * 3 * cw + cw, o_c + (g + 1) * 3 * cw)]
    out = jnp.concatenate([part.astype(jnp.bfloat16) for part in parts], axis=2)
    assert out.shape[2] == IN_WIDTH
    return out


def kernel(x, c, w_ada, b_ada, g_pre_mix, g_post_mix, w_in, sinks, w_br_a, w_br_b, w_br_c,
           w_out, g_pre_ffn, g_post_ffn, w_gate, w_up, w_down):
    bn, s, d = x.shape
    assert bn == 1 and d == D_MODEL and s % SEQ_MULTIPLE == 0
    hd = HEAD_DIM
    w_in_p = _permute_in_proj(w_in)
    w_br_b_p = jnp.concatenate([w_br_b[:, hh * hd:(hh + 1) * hd] for hh in B_HEAD_ORDER], axis=1)

    mod_all = _ada_call(c, w_ada, b_ada).reshape(DEPTH, 6, d)
    h = x.reshape(s, d)
    for l in range(DEPTH):
        mod = mod_all[l]
        z, zc1, zc2 = _in_proj_call(h, g_pre_mix[l].reshape(1, d), mod, w_in_p, l)
        oa = _moba_call(z)
        ob = _swa_call(z, sinks[l])
        oc_lse = (_dilated_call(z.reshape(1, s, Z_WIDTH), 0, Z_C0)
                  + _dilated_call(zc1, 1, 0) + _dilated_call(zc2, 2, 0))
        h = _merge_call(h, z, oa, ob, oc_lse, w_br_a, w_br_b_p, w_br_c, w_out,
                        g_post_mix[l].reshape(1, d), mod, l)
        h = _ffn_call(h, g_pre_ffn[l].reshape(1, d), g_post_ffn[l].reshape(1, d), mod,
                      w_gate, w_up, w_down, l)
    return h.reshape(bn, s, d)
```

```python
import numpy as np
import jax
import jax.numpy as jnp
from jax import lax
from jax.experimental import pallas as pl
from jax.experimental.pallas import tpu as pltpu

D_MODEL = 1024
DEPTH = 4
HEAD_DIM = 64
A_HEADS = 8
MOBA_BLOCK = 256
MOBA_TOPK = 3
B_HEADS = 8
B_KV_HEADS = 2
B_WINDOW = 128
C_GROUPS = ((128, 1), (512, 4), (2048, 16))
C_HEADS_PER_GROUP = 4
C_HEADS = len(C_GROUPS) * C_HEADS_PER_GROUP
BAND = 128
D_FF = 2816
N_ALIBI_HEADS = A_HEADS + B_HEADS + C_HEADS
SEQ_MULTIPLE = 2048
RMS_EPS = 1e-6

LANES_V7X = 128
BF16_SUBLANES_V7X = 16
VMEM_LIMIT_BYTES_V7X = 56 * 1024 * 1024

ROW_TILE = 512
C_WIDTH = 3 * C_HEADS_PER_GROUP * HEAD_DIM

Z_GATES = 0
Z_A = 3 * D_MODEL
Z_BQ = Z_A + 3 * A_HEADS * HEAD_DIM
Z_BK = Z_BQ + B_HEADS * HEAD_DIM
Z_BV = Z_BK + B_KV_HEADS * HEAD_DIM
Z_C0 = Z_BV + B_KV_HEADS * HEAD_DIM
Z_WIDTH = Z_C0 + C_WIDTH
IN_WIDTH = Z_WIDTH + (len(C_GROUPS) - 1) * C_WIDTH
B_HEAD_ORDER = (0, 4, 1, 5, 2, 6, 3, 7)

NEG_BIG = -1e30
QK_SCALE = HEAD_DIM ** -0.5
LOG2E = 1.4426950408889634


def _alibi_slopes():
    n = N_ALIBI_HEADS
    return [float(2.0 ** (-8.0 * (i + 1) / n)) for i in range(n)]


_SLOPES = _alibi_slopes()
SLOPES_B = _SLOPES[:B_HEADS]
SLOPES_C = _SLOPES[B_HEADS:B_HEADS + C_HEADS]
SLOPES_A = _SLOPES[B_HEADS + C_HEADS:]


def _dot(a, b):
    return jnp.dot(a, b, preferred_element_type=jnp.float32)


def _dot_nt(a, b):
    return lax.dot_general(a, b, (((1,), (1,)), ((), ())), preferred_element_type=jnp.float32)


def _params(semantics):
    return pltpu.CompilerParams(dimension_semantics=semantics, vmem_limit_bytes=VMEM_LIMIT_BYTES_V7X)


def _rmsnorm(x, g):
    return x * lax.rsqrt(jnp.mean(x * x, axis=-1, keepdims=True) + RMS_EPS) * g


def _ada_kernel(c_ref, w_ref, b_ref, o_ref):
    c = c_ref[...]
    sc = c * jax.nn.sigmoid(c)
    o_ref[...] = jnp.sum(w_ref[...] * sc, axis=0, keepdims=True) + b_ref[...]


def _ada_call(c, w_ada, b_ada):
    depth, d, n = w_ada.shape
    tn = 1536
    return pl.pallas_call(
        _ada_kernel,
        grid=(depth, n // tn),
        in_specs=[
            pl.BlockSpec((d, 1), lambda l, j: (0, 0)),
            pl.BlockSpec((None, d, tn), lambda l, j: (l, 0, j)),
            pl.BlockSpec((None, 1, tn), lambda l, j: (l, 0, j)),
        ],
        out_specs=pl.BlockSpec((None, 1, tn), lambda l, j: (l, 0, j)),
        out_shape=jax.ShapeDtypeStruct((depth, 1, n), jnp.float32),
        compiler_params=_params(("arbitrary", "arbitrary")),
        name="adaln_mod",
    )(c.reshape(d, 1), w_ada, b_ada.reshape(depth, 1, n))


IN_PROJ_CHUNK = 512


def _in_proj_kernel(h_ref, g_ref, mod_ref, w_ref, z_ref, zc1_ref, zc2_ref, tmp_ref):
    x = h_ref[...]
    u = _rmsnorm(x, g_ref[...]) * (1.0 + mod_ref[1:2, :]) + mod_ref[0:1, :]
    u = u.astype(jnp.bfloat16)
    for c0 in range(0, Z_WIDTH, IN_PROJ_CHUNK):
        z_ref[:, c0:c0 + IN_PROJ_CHUNK] = _dot(u, w_ref[:, c0:c0 + IN_PROJ_CHUNK]).astype(z_ref.dtype)
    for g, out_ref in ((1, zc1_ref), (2, zc2_ref)):
        c0 = Z_WIDTH + (g - 1) * C_WIDTH
        res = _dot(u, w_ref[:, c0:c0 + C_WIDTH])
        dil = C_GROUPS[g][1]
        rows = x.shape[0] // dil
        ln = LANES_V7X
        for c in range(C_WIDTH // ln):
            tmp_ref[c] = res[:, c * ln:(c + 1) * ln]
        for r in range(dil):
            for c in range(C_WIDTH // ln):
                out_ref[r, :, c * ln:(c + 1) * ln] = tmp_ref[c, pl.ds(r, rows, stride=dil), :].astype(out_ref.dtype)


def _in_proj_call(h, g, mod, w, layer):
    s, d = h.shape
    tm = ROW_TILE
    d1, d2 = C_GROUPS[1][1], C_GROUPS[2][1]
    assert tm % (d2 * BF16_SUBLANES_V7X) == 0
    return pl.pallas_call(
        _in_proj_kernel,
        grid=(s // tm,),
        in_specs=[
            pl.BlockSpec((tm, d), lambda i: (i, 0)),
            pl.BlockSpec((1, d), lambda i: (0, 0)),
            pl.BlockSpec((6, d), lambda i: (0, 0)),
            pl.BlockSpec((None, d, IN_WIDTH), lambda i: (layer, 0, 0), pipeline_mode=pl.Buffered(1)),
        ],
        out_specs=[
            pl.BlockSpec((tm, Z_WIDTH), lambda i: (i, 0)),
            pl.BlockSpec((d1, tm // d1, C_WIDTH), lambda i: (0, i, 0)),
            pl.BlockSpec((d2, tm // d2, C_WIDTH), lambda i: (0, i, 0)),
        ],
        out_shape=[
            jax.ShapeDtypeStruct((s, Z_WIDTH), jnp.bfloat16),
            jax.ShapeDtypeStruct((d1, s // d1, C_WIDTH), jnp.bfloat16),
            jax.ShapeDtypeStruct((d2, s // d2, C_WIDTH), jnp.bfloat16),
        ],
        scratch_shapes=[pltpu.VMEM((C_WIDTH // LANES_V7X, tm, LANES_V7X), jnp.float32)],
        compiler_params=_params(("arbitrary",)),
        name="in_proj",
    )(h, g, mod, w)


SUBLANES_V7X = 8
MOBA_V_ROWS = HEAD_DIM + BF16_SUBLANES_V7X
MOBA_GROUPS = MOBA_BLOCK // SUBLANES_V7X
MOBA_UNROLL_LOG2 = 1
MOBA_UNROLL = 1 << MOBA_UNROLL_LOG2
MOBA_TILES = 2


def _all_sublanes_max(x):
    for shift in (4, 2, 1):
        x = jnp.maximum(x, pltpu.roll(x, shift, axis=0))
    return x


def _moba_lanes():
    lane = lax.broadcasted_iota(jnp.int32, (MOBA_BLOCK, LANES_V7X), 1)
    own = [lane < HEAD_DIM, lane >= HEAD_DIM]
    spare = [HEAD_DIM, 0]
    bias_lanes = [(lane == spare[h]) | (lane == spare[h] + 1) for h in (0, 1)]
    return lane, own, spare, bias_lanes


def _moba_kernel(slopes_ref, q_ref, k_ref, v_ref, o_ref, vt_ref, km_ref, ka_ref, sel_ref, s_ref, e_ref, st_ref, acc_ref):
    p = pl.program_id(0)
    nblk = vt_ref.shape[0]
    blk = MOBA_BLOCK
    half = HEAD_DIM
    _, own, _, bias_lanes = _moba_lanes()
    ones = jnp.ones((BF16_SUBLANES_V7X, blk), jnp.bfloat16)

    def prepare(j, carry):
        rows = pl.ds(pl.multiple_of(j * blk, blk), blk)
        vt = v_ref[rows, :].astype(jnp.float32).T.astype(jnp.bfloat16)
        k2 = k_ref[rows, :]
        km_ref[pl.ds(j, 1), :] = jnp.mean(k2.astype(jnp.float32), axis=0, keepdims=True)
        k2f = k2.astype(jnp.float32)
        pos = lax.broadcasted_iota(jnp.int32, k2.shape, 0).astype(jnp.float32)
        for h in (0, 1):
            vt_ref[j, h, 0:half, :] = vt[h * half:(h + 1) * half, :]
            vt_ref[j, h, half:, :] = ones
            ka = jnp.where(own[h], k2f, jnp.where(bias_lanes[h], pos, 0.0))
            ka_ref[h, rows, :] = ka.astype(jnp.bfloat16)
        return carry

    lax.fori_loop(0, nblk, prepare, 0)
    for w in range(MOBA_TILES):
        _moba_select(w, w, q_ref, km_ref, sel_ref)

    def tiles(t, carry):
        _moba_tiles(t, p, slopes_ref, q_ref, o_ref, vt_ref, km_ref, ka_ref, sel_ref, s_ref, e_ref, st_ref, acc_ref)
        return carry

    lax.fori_loop(0, nblk // MOBA_TILES, tiles, 0)


def _moba_select(i, w, q_ref, km_ref, sel_ref):
    nblk, blk = sel_ref.shape[1], MOBA_BLOCK
    _, own, _, _ = _moba_lanes()
    q2 = q_ref[pl.ds(pl.multiple_of(i * blk, blk), blk), :]
    blk_id = lax.broadcasted_iota(jnp.int32, (nblk, blk), 0)
    km = km_ref[...]
    km_hi = km.astype(jnp.bfloat16)
    km_lo = (km - km_hi.astype(jnp.float32)).astype(jnp.bfloat16)
    for h in (0, 1):
        qz = jnp.where(own[h], q2, jnp.zeros_like(q2))
        gate = _dot_nt(km_hi, qz) + _dot_nt(km_lo, qz)
        gate = jnp.where(blk_id < i, gate, -jnp.inf)
        sel = jnp.full((nblk, blk), NEG_BIG, jnp.float32)
        for _ in range(MOBA_TOPK):
            mx = jnp.max(gate, axis=0, keepdims=True)
            cand = (gate == mx) & (mx > -jnp.inf)
            idx = jnp.min(jnp.where(cand, blk_id, nblk), axis=0, keepdims=True)
            chosen = blk_id == idx
            sel = jnp.where(chosen, 0.0, sel)
            gate = jnp.where(chosen, -jnp.inf, gate)
        sel_ref[2 * w + h] = sel


def _moba_tiles(t, p, slopes_ref, q_ref, o_ref, vt_ref, km_ref, ka_ref, sel_ref, s_ref, e_ref, st_ref, acc_ref):
    nblk = vt_ref.shape[0]
    blk = MOBA_BLOCK
    half = HEAD_DIM
    sub = SUBLANES_V7X
    tile3 = (MOBA_GROUPS, sub, blk)
    key_pos = lax.broadcasted_iota(jnp.int32, tile3, 0) * sub + lax.broadcasted_iota(jnp.int32, tile3, 1)
    qry_pos = lax.broadcasted_iota(jnp.int32, tile3, 2)
    lane, own, spare, _ = _moba_lanes()
    streams = range(2 * MOBA_TILES)
    tile_of = [t * MOBA_TILES + st // 2 for st in streams]
    head_of = [st % 2 for st in streams]
    last_tile = tile_of[-1]

    qh = []
    for st in streams:
        h = head_of[st]
        q2f = q_ref[pl.ds(pl.multiple_of(tile_of[st] * blk, blk), blk), :].astype(jnp.float32)
        hi = slopes_ref[A_HEADS + 2 * p + h]
        lo = slopes_ref[2 * A_HEADS + 2 * p + h]
        extra = jnp.where(lane == spare[h], hi, jnp.where(lane == spare[h] + 1, lo, 0.0))
        qh.append(jnp.where(own[h], q2f, extra).astype(jnp.bfloat16))

    unroll = MOBA_UNROLL
    acc_groups = MOBA_V_ROWS // sub

    def past_block(n):
        return jnp.clip(n - 1, 0, nblk - 1)

    def item_block(n, st, own_first):
        return tile_of[st] if own_first else past_block(n)

    def issue_scores(items, own_first):
        out = []
        for u, n in enumerate(items):
            per_stream = []
            for st in streams:
                b = item_block(n, st, own_first and u == 0)
                rows = pl.ds(pl.multiple_of(b * blk, blk), blk)
                per_stream.append(_dot_nt(ka_ref[head_of[st], rows, :], qh[st]).reshape(tile3))
            out.append(per_stream)
        return out

    def stage_scores(raws, slot, items, own_first):
        for u, n in enumerate(items):
            for st in streams:
                s = raws[u][st]
                if own_first and u == 0:
                    s = jnp.where(qry_pos >= key_pos, s, NEG_BIG)
                    row = jnp.zeros((sub, blk), jnp.float32)
                else:
                    b = past_block(n)
                    gap = ((tile_of[st] - b) * blk).astype(jnp.float32)
                    row = sel_ref[st, pl.ds(b, 1), :] - slopes_ref[2 * p + head_of[st]] * gap
                    row = jnp.broadcast_to(jnp.where(n <= tile_of[st], row, NEG_BIG), (sub, blk))
                s_ref[slot, u, st] = s
                st_ref[slot, u, st, 0] = _all_sublanes_max(jnp.max(s, axis=0)) + row
                st_ref[slot, u, st, 1] = row

    def exponentiate(group_slot, ms):
        new_ms, alphas = [], []
        for st in streams:
            m_new = ms[st]
            for u in range(unroll):
                m_new = jnp.maximum(m_new, st_ref[group_slot, u, st, 0])
            alphas.append(jnp.exp2(ms[st] - m_new))
            new_ms.append(m_new)
            for u in range(unroll):
                e = jnp.exp2(s_ref[group_slot, u, st] - (m_new - st_ref[group_slot, u, st, 1])[None])
                e_ref[group_slot, u, st] = e.reshape(blk, blk).astype(jnp.bfloat16)
        return new_ms, alphas

    def accumulate(group, slot, alphas):
        for st in streams:
            acc = alphas[st][None] * acc_ref[st]
            for u in range(unroll):
                n = group * unroll + u
                vblock = jnp.where(n == 0, tile_of[st], past_block(n))
                acc = acc + _dot(vt_ref[vblock, head_of[st]], e_ref[slot, u, st]).reshape(acc_groups, sub, blk)
            acc_ref[st] = acc

    def score_group(group, slot, own_first):
        items = [group * unroll + u for u in range(unroll)]
        raws = issue_scores(items, own_first)
        return lambda: stage_scores(raws, slot, items, own_first)

    def tick(g, slot, ms, alphas):
        items = [g * unroll + u for u in range(unroll)]
        raws = [[None] * len(streams) for _ in items]
        for st in streams:
            for u, n in enumerate(items):
                rows = pl.ds(pl.multiple_of(past_block(n) * blk, blk), blk)
                raws[u][st] = _dot_nt(ka_ref[head_of[st], rows, :], qh[st]).reshape(tile3)
            acc = alphas[st][None] * acc_ref[st]
            for u in range(unroll):
                n = (g - 2) * unroll + u
                vblock = jnp.where(n == 0, tile_of[st], past_block(n))
                acc = acc + _dot(vt_ref[vblock, head_of[st]], e_ref[slot, u, st]).reshape(acc_groups, sub, blk)
            acc_ref[st] = acc
        ms, alphas = exponentiate(1 - slot, ms)
        stage_scores(raws, slot, items, False)
        return ms, alphas

    n_st = len(streams)
    pairs = jnp.maximum(lax.shift_right_logical(last_tile + 2 * unroll, MOBA_UNROLL_LOG2 + 1), 1)
    ms = [jnp.full((sub, blk), -jnp.inf, jnp.float32) for _ in streams]
    acc_ref[...] = jnp.zeros(acc_ref.shape, jnp.float32)
    score_group(0, 0, True)()
    finish_scores = score_group(1, 1, False)
    ms, alphas = exponentiate(0, ms)
    finish_scores()

    def two_ticks(k, carry):
        ms, alphas = tick(2 * k, 0, list(carry[:n_st]), list(carry[n_st:]))
        ms, alphas = tick(2 * k + 1, 1, ms, alphas)
        return tuple(ms + alphas)

    carry = lax.fori_loop(1, pairs, two_ticks, tuple(ms + alphas))
    ms, alphas = list(carry[:n_st]), list(carry[n_st:])
    accumulate(2 * pairs - 2, 0, alphas)
    ms, alphas = exponentiate(1, ms)
    for w in range(MOBA_TILES):
        _moba_select(jnp.minimum(tile_of[2 * w] + MOBA_TILES, nblk - 1), w, q_ref, km_ref, sel_ref)
    accumulate(2 * pairs - 1, 1, alphas)
    for w in range(MOBA_TILES):
        accs = [acc_ref[2 * w + h] for h in (0, 1)]
        o_t = jnp.concatenate([(acc[0:half // sub] / acc[half // sub][None]).reshape(half, blk) for acc in accs],
                              axis=0)
        o_ref[pl.ds(pl.multiple_of(tile_of[2 * w] * blk, blk), blk), :] = o_t.T.astype(o_ref.dtype)


def _moba_slopes():
    full = np.asarray([sl * LOG2E for sl in SLOPES_A], np.float32)
    hi = full.astype(jnp.bfloat16).astype(np.float32)
    lo = (full - hi).astype(jnp.bfloat16).astype(np.float32)
    return jnp.asarray(np.concatenate([full, hi, lo]))


def _moba_call(z):
    s = z.shape[0]
    nblk = s // MOBA_BLOCK
    pairs = A_HEADS // 2
    ln = LANES_V7X
    qb, kb, vb = Z_A // ln, Z_A // ln + pairs, Z_A // ln + 2 * pairs
    return pl.pallas_call(
        _moba_kernel,
        grid=(pairs,),
        in_specs=[
            pl.BlockSpec(memory_space=pltpu.SMEM),
            pl.BlockSpec((s, ln), lambda p: (0, qb + p)),
            pl.BlockSpec((s, ln), lambda p: (0, kb + p), pipeline_mode=pl.Buffered(1)),
            pl.BlockSpec((s, ln), lambda p: (0, vb + p), pipeline_mode=pl.Buffered(1)),
        ],
        out_specs=pl.BlockSpec((s, ln), lambda p: (0, p)),
        out_shape=jax.ShapeDtypeStruct((s, A_HEADS * HEAD_DIM), jnp.bfloat16),
        scratch_shapes=[
            pltpu.VMEM((nblk, 2, MOBA_V_ROWS, MOBA_BLOCK), jnp.bfloat16),
            pltpu.VMEM((nblk, ln), jnp.float32),
            pltpu.VMEM((2, s, ln), jnp.bfloat16),
            pltpu.VMEM((2 * MOBA_TILES, nblk, MOBA_BLOCK), jnp.float32),
            pltpu.VMEM((2, MOBA_UNROLL, 2 * MOBA_TILES, MOBA_GROUPS, SUBLANES_V7X, MOBA_BLOCK), jnp.float32),
            pltpu.VMEM((2, MOBA_UNROLL, 2 * MOBA_TILES, MOBA_BLOCK, MOBA_BLOCK), jnp.bfloat16),
            pltpu.VMEM((2, MOBA_UNROLL, 2 * MOBA_TILES, 2, SUBLANES_V7X, MOBA_BLOCK), jnp.float32),
            pltpu.VMEM((2 * MOBA_TILES, MOBA_V_ROWS // SUBLANES_V7X, SUBLANES_V7X, MOBA_BLOCK), jnp.float32),
        ],
        compiler_params=_params(("arbitrary",)),
        name="moba_attn",
    )(_moba_slopes(), z, z, z)


def _blocks_per_step(nb):
    return max(c for c in (4, 2, 1) if nb % c == 0)


def _band_penalty_table(slope_dils, max_steps):
    steps = np.arange(BAND)[:, None] + BAND - np.arange(2 * BAND)[None, :]
    inside = (steps >= 0) & (steps <= max_steps)
    table = [np.where(inside, np.float32(sd) * steps.astype(np.float32), np.float32(-NEG_BIG)) for sd in slope_dils]
    return jnp.asarray(np.stack(table).astype(np.float32))


def _band_no_prev(first_step):
    kj = lax.broadcasted_iota(jnp.int32, (BAND, 2 * BAND), 1)
    return jnp.logical_and(first_step, kj < BAND)


def _band_units(units, lookahead):
    raws = {j: _dot_nt(units[j][0], units[j][1]) for j in range(min(lookahead, len(units)))}
    outs, lses = [], []
    for j, (_, _, vv, penalty, drop, sink) in enumerate(units):
        ahead = j + lookahead
        if ahead < len(units):
            raws[ahead] = _dot_nt(units[ahead][0], units[ahead][1])
        s = raws.pop(j) - penalty
        if drop is not None:
            s = jnp.where(drop, NEG_BIG, s)
        m = jnp.max(s, axis=1, keepdims=True)
        if sink is not None:
            m = jnp.maximum(m, sink)
        e = jnp.exp(s - m)
        denom = jnp.sum(e, axis=1, keepdims=True)
        if sink is not None:
            denom = denom + jnp.exp(sink - m)
        outs.append(_dot(e.astype(jnp.bfloat16), vv) / denom)
        lses.append(m + jnp.log(denom))
    return outs, lses


def _swa_kernel(sinks_ref, bias_ref, q_ref, kp_ref, ko_ref, vp_ref, vo_ref, o_ref):
    n = pl.program_id(0)
    nbq = q_ref.shape[0] // BAND
    no_prev = _band_no_prev(n == 0)
    k_all = jnp.concatenate([kp_ref[...], ko_ref[...]], axis=0)
    v_all = jnp.concatenate([vp_ref[...], vo_ref[...]], axis=0)
    ln = LANES_V7X
    lane = lax.broadcasted_iota(jnp.int32, (BAND, ln), 1)
    low = lane < HEAD_DIM
    units = []
    for qb in range(nbq):
        kk = k_all[qb * BAND:(qb + 2) * BAND]
        vv = v_all[qb * BAND:(qb + 2) * BAND]
        drop = no_prev if qb == 0 else None
        for b in range(B_HEADS // 2):
            q2 = q_ref[qb * BAND:(qb + 1) * BAND, b * ln:(b + 1) * ln]
            for c in range(2):
                head = B_HEAD_ORDER[2 * b + c]
                qm = jnp.where(low if c == 0 else jnp.logical_not(low), q2, jnp.zeros_like(q2))
                units.append((qm, kk, vv, bias_ref[2 * b + c], drop, sinks_ref[head]))
    outs, _ = _band_units(units, B_HEADS)
    for qb in range(nbq):
        for b in range(B_HEADS // 2):
            j = qb * B_HEADS + 2 * b
            o_ref[qb * BAND:(qb + 1) * BAND, b * ln:(b + 1) * ln] = (
                jnp.where(low, outs[j], outs[j + 1]).astype(o_ref.dtype))


def _swa_call(z, sinks):
    s = z.shape[0]
    nbq = _blocks_per_step(s // BAND)
    rows = nbq * BAND
    ln = LANES_V7X
    qw = B_HEADS * HEAD_DIM
    prev = lambda n: jnp.maximum(n * nbq - 1, 0)
    bias = _band_penalty_table([SLOPES_B[head] for head in B_HEAD_ORDER], B_WINDOW - 1)
    return pl.pallas_call(
        _swa_kernel,
        grid=(s // rows,),
        in_specs=[
            pl.BlockSpec(memory_space=pltpu.SMEM),
            pl.BlockSpec(bias.shape, lambda n: (0, 0, 0), pipeline_mode=pl.Buffered(1)),
            pl.BlockSpec((rows, qw), lambda n: (n, Z_BQ // qw)),
            pl.BlockSpec((BAND, ln), lambda n: (prev(n), Z_BK // ln)),
            pl.BlockSpec((rows, ln), lambda n: (n, Z_BK // ln)),
            pl.BlockSpec((BAND, ln), lambda n: (prev(n), Z_BV // ln)),
            pl.BlockSpec((rows, ln), lambda n: (n, Z_BV // ln)),
        ],
        out_specs=pl.BlockSpec((rows, qw), lambda n: (n, 0)),
        out_shape=jax.ShapeDtypeStruct((s, qw), jnp.bfloat16),
        compiler_params=_params(("arbitrary",)),
        name="swa_attn",
    )(sinks, bias, z, z, z, z, z)


def _dilated_kernel(bias_ref, q_ref, kp_ref, ko_ref, vp_ref, vo_ref, o_ref, lse_ref):
    n = pl.program_id(1)
    nbq = q_ref.shape[0] // BAND
    no_prev = _band_no_prev(n == 0)
    k_all = jnp.concatenate([kp_ref[...], ko_ref[...]], axis=0)
    v_all = jnp.concatenate([vp_ref[...], vo_ref[...]], axis=0)
    cw = q_ref.shape[1]
    lane = lax.broadcasted_iota(jnp.int32, (BAND, cw), 1)
    heads = range(C_HEADS_PER_GROUP)
    mine = [(lane >= h * HEAD_DIM) & (lane < (h + 1) * HEAD_DIM) for h in heads]
    biases = [bias_ref[h] for h in heads]
    units = []
    for qb in range(nbq):
        q4 = q_ref[qb * BAND:(qb + 1) * BAND, :]
        kk = k_all[qb * BAND:(qb + 2) * BAND]
        vv = v_all[qb * BAND:(qb + 2) * BAND]
        drop = no_prev if qb == 0 else None
        for h in heads:
            units.append((jnp.where(mine[h], q4, jnp.zeros_like(q4)), kk, vv, biases[h], drop, None))
    outs, lses = _band_units(units, C_HEADS_PER_GROUP)
    for qb in range(nbq):
        o_all = jnp.zeros((BAND, cw), jnp.float32)
        lse_all = jnp.zeros((BAND, cw), jnp.float32)
        for h in heads:
            j = qb * C_HEADS_PER_GROUP + h
            o_all = jnp.where(mine[h], outs[j], o_all)
            lse_all = jnp.where(mine[h], lses[j], lse_all)
        o_ref[qb * BAND:(qb + 1) * BAND, :] = o_all
        lse_ref[qb * BAND:(qb + 1) * BAND, :] = lse_all


def _dilated_call(zc, group, col0):
    window, dil = C_GROUPS[group]
    assert zc.shape[0] == dil
    length = zc.shape[1]
    nbq = _blocks_per_step(length // BAND)
    rows = nbq * BAND
    cw = C_HEADS_PER_GROUP * HEAD_DIM
    base = col0 // cw
    slopes = [SLOPES_C[group * C_HEADS_PER_GROUP + h] * dil for h in range(C_HEADS_PER_GROUP)]
    bias = _band_penalty_table(slopes, window // dil)
    own = lambda col: pl.BlockSpec((None, rows, cw), lambda r, n: (r, n, base + col))
    prev = lambda col: pl.BlockSpec((None, BAND, cw), lambda r, n: (r, jnp.maximum(n * nbq - 1, 0), base + col))
    out_blk = pl.BlockSpec((None, rows, cw), lambda r, n: (r, n, 0))
    return pl.pallas_call(
        _dilated_kernel,
        grid=(dil, length // rows),
        in_specs=[pl.BlockSpec(bias.shape, lambda r, n: (0, 0, 0), pipeline_mode=pl.Buffered(1)),
                  own(0), prev(1), own(1), prev(2), own(2)],
        out_specs=[out_blk, out_blk],
        out_shape=[jax.ShapeDtypeStruct((dil, length, cw), jnp.float32)] * 2,
        compiler_params=_params(("arbitrary", "arbitrary")),
        name=f"dilated_attn_g{group}",
    )(bias, zc, zc, zc, zc, zc)


def _merge_kernel(h_ref, zg_ref, oa_ref, ob_ref, o0_ref, l0_ref, o1_ref, l1_ref, o2_ref, l2_ref,
                  wa_ref, wb_ref, wc_ref, wo_ref, g_ref, mod_ref, out_ref, nat_ref):
    tm = h_ref.shape[0]

    def natural(ref, slot):
        dil = ref.shape[0]
        if dil == 1:
            return ref[0]
        ln = LANES_V7X
        chunks = ref.shape[2] // ln
        for r in range(dil):
            for c in range(chunks):
                nat_ref[slot * chunks + c, pl.ds(r, tm // dil, stride=dil), :] = ref[r, :, c * ln:(c + 1) * ln]
        return jnp.concatenate([nat_ref[slot * chunks + c] for c in range(chunks)], axis=1)

    l0, l1, l2 = natural(l0_ref, 0), natural(l1_ref, 0), natural(l2_ref, 1)
    o0, o1, o2 = natural(o0_ref, 0), natural(o1_ref, 2), natural(o2_ref, 3)
    mx = jnp.maximum(jnp.maximum(l0, l1), l2)
    e0, e1, e2 = jnp.exp(l0 - mx), jnp.exp(l1 - mx), jnp.exp(l2 - mx)
    oc = ((e0 * o0 + e1 * o1 + e2 * o2) / (e0 + e1 + e2)).astype(jnp.bfloat16)
    d = D_MODEL
    subs = range(0, tm, MERGE_SUB_ROWS)
    wa, wb, wc, wo = [ref[...].astype(jnp.bfloat16) for ref in (wa_ref, wb_ref, wc_ref, wo_ref)]
    branches = [(_dot(oa_ref[pl.ds(r0, MERGE_SUB_ROWS), :], wa),
                 _dot(ob_ref[pl.ds(r0, MERGE_SUB_ROWS), :], wb),
                 _dot(oc[r0:r0 + MERGE_SUB_ROWS], wc)) for r0 in subs]
    ys = []
    for r0, (ya, yb, yc) in zip(subs, branches):
        gates = jax.nn.sigmoid(zg_ref[pl.ds(r0, MERGE_SUB_ROWS), :].astype(jnp.float32))
        merged = gates[:, 0:d] * ya + gates[:, d:2 * d] * yb + gates[:, 2 * d:3 * d] * yc
        ys.append(_dot(merged.astype(jnp.bfloat16), wo))
    for r0, y in zip(subs, ys):
        rs = pl.ds(r0, MERGE_SUB_ROWS)
        out_ref[rs, :] = h_ref[rs, :] + mod_ref[2:3, :] * _rmsnorm(y, g_ref[...])


MERGE_SUB_ROWS = 256


def _merge_call(h, z, oa, ob, oc_lse, wa, wb, wc, wo, g, mod, layer):
    s, d = h.shape
    tm = 2 * MERGE_SUB_ROWS
    cw = C_HEADS_PER_GROUP * HEAD_DIM
    row = lambda w: pl.BlockSpec((tm, w), lambda i: (i, 0))
    res = lambda a: pl.BlockSpec((a.shape[0], tm // a.shape[0], cw), lambda i: (0, i, 0))
    full = lambda a: pl.BlockSpec(a.shape, lambda i: (0, 0), pipeline_mode=pl.Buffered(1))
    stacked = lambda a: pl.BlockSpec((None,) + a.shape[1:], lambda i: (layer, 0, 0), pipeline_mode=pl.Buffered(1))
    return pl.pallas_call(
        _merge_kernel,
        grid=(s // tm,),
        in_specs=[row(d), row(3 * d), row(oa.shape[1]), row(ob.shape[1])] + [res(a) for a in oc_lse]
        + [stacked(wa), stacked(wb), stacked(wc), stacked(wo), full(g), full(mod)],
        out_specs=row(d),
        out_shape=jax.ShapeDtypeStruct((s, d), jnp.float32),
        scratch_shapes=[pltpu.VMEM((4 * cw // LANES_V7X, tm, LANES_V7X), jnp.float32)],
        input_output_aliases={0: 0} if layer > 0 else {},
        compiler_params=_params(("arbitrary",)),
        name="merge_out_proj",
    )(h, z, oa, ob, *oc_lse, wa, wb, wc, wo, g, mod)


FFN_CHUNK = 1408


def _ffn_kernel(h_ref, g_pre_ref, g_post_ref, mod_ref, wg_ref, wu_ref, wd_ref, out_ref):
    x = h_ref[...]
    u = _rmsnorm(x, g_pre_ref[...]) * (1.0 + mod_ref[4:5, :]) + mod_ref[3:4, :]
    u = u.astype(jnp.bfloat16)
    y = jnp.zeros(x.shape, jnp.float32)
    for c0 in range(0, D_FF, FFN_CHUNK):
        gate = _dot(u, wg_ref[:, c0:c0 + FFN_CHUNK].astype(jnp.bfloat16))
        up = _dot(u, wu_ref[:, c0:c0 + FFN_CHUNK].astype(jnp.bfloat16))
        act = (gate * jax.nn.sigmoid(gate) * up).astype(jnp.bfloat16)
        y = y + _dot(act, wd_ref[c0:c0 + FFN_CHUNK, :].astype(jnp.bfloat16))
    out_ref[...] = x + mod_ref[5:6, :] * _rmsnorm(y, g_post_ref[...])


def _ffn_call(h, g_pre, g_post, mod, wg, wu, wd, layer):
    s, d = h.shape
    tm = ROW_TILE
    row = pl.BlockSpec((tm, d), lambda i: (i, 0))
    full = lambda a: pl.BlockSpec(a.shape, lambda i: (0, 0), pipeline_mode=pl.Buffered(1))
    stacked = lambda a: pl.BlockSpec((None,) + a.shape[1:], lambda i: (layer, 0, 0), pipeline_mode=pl.Buffered(1))
    return pl.pallas_call(
        _ffn_kernel,
        grid=(s // tm,),
        in_specs=[row, full(g_pre), full(g_post), full(mod), stacked(wg), stacked(wu), stacked(wd)],
        out_specs=row,
        out_shape=jax.ShapeDtypeStruct((s, d), jnp.float32),
        input_output_aliases={0: 0},
        compiler_params=_params(("arbitrary",)),
        name="swiglu_ffn",
    )(h, g_pre, g_post, mod, wg, wu, wd)


def _permute_in_proj(w_in):
    hd = HEAD_DIM
    o_bq = 3 * A_HEADS * hd
    o_bkv = o_bq + B_HEADS * hd
    o_c = o_bkv + 2 * B_KV_HEADS * hd
    o_g = o_c + 3 * C_HEADS * hd
    cw = C_HEADS_PER_GROUP * hd
    sl = lambda a, b: w_in[:, :, a:b]
    parts = [sl(o_g, o_g + 3 * D_MODEL), sl(0, A_HEADS * hd) * (QK_SCALE * LOG2E), sl(A_HEADS * hd, o_bq)]
    parts += [sl(o_bq + hh * hd, o_bq + (hh + 1) * hd) * QK_SCALE for hh in B_HEAD_ORDER]
    parts += [sl(o_bkv, o_c)]
    for g in range(len(C_GROUPS)):
        parts += [sl(o_c + g * 3 * cw, o_c + g * 3 * cw + cw) * QK_SCALE,
                  sl(o_c + g * 3 * cw + cw, o_c + (g + 1) * 3 * cw)]
    out = jnp.concatenate([part.astype(jnp.bfloat16) for part in parts], axis=2)
    assert out.shape[2] == IN_WIDTH
    return out


def kernel(x, c, w_ada, b_ada, g_pre_mix, g_post_mix, w_in, sinks, w_br_a, w_br_b, w_br_c,
           w_out, g_pre_ffn, g_post_ffn, w_gate, w_up, w_down):
    bn, s, d = x.shape
    assert bn == 1 and d == D_MODEL and s % SEQ_MULTIPLE == 0
    hd = HEAD_DIM
    w_in_p = _permute_in_proj(w_in)
    w_br_b_p = jnp.concatenate([w_br_b[:, hh * hd:(hh + 1) * hd] for hh in B_HEAD_ORDER], axis=1)

    mod_all = _ada_call(c, w_ada, b_ada).reshape(DEPTH, 6, d)
    h = x.reshape(s, d)
    for l in range(DEPTH):
        mod = mod_all[l]
        z, zc1, zc2 = _in_proj_call(h, g_pre_mix[l].reshape(1, d), mod, w_in_p, l)
        oa = _moba_call(z)
        ob = _swa_call(z, sinks[l])
        oc_lse = (_dilated_call(z.reshape(1, s, Z_WIDTH), 0, Z_C0)
                  + _dilated_call(zc1, 1, 0) + _dilated_call(zc2, 2, 0))
        h = _merge_call(h, z, oa, ob, oc_lse, w_br_a, w_br_b_p, w_br_c, w_out,
                        g_post_mix[l].reshape(1, d), mod, l)
        h = _ffn_call(h, g_pre_ffn[l].reshape(1, d), g_post_ffn[l].reshape(1, d), mod,
                      w_gate, w_up, w_down, l)
    return h.reshape(bn, s, d)
```

```python
import numpy as np
import jax
import jax.numpy as jnp
from jax import lax
from jax.experimental import pallas as pl
from jax.experimental.pallas import tpu as pltpu

D_MODEL = 1024
DEPTH = 4
HEAD_DIM = 64
A_HEADS = 8
MOBA_BLOCK = 256
MOBA_TOPK = 3
B_HEADS = 8
B_KV_HEADS = 2
B_WINDOW = 128
C_GROUPS = ((128, 1), (512, 4), (2048, 16))
C_HEADS_PER_GROUP = 4
C_HEADS = len(C_GROUPS) * C_HEADS_PER_GROUP
BAND = 128
D_FF = 2816
N_ALIBI_HEADS = A_HEADS + B_HEADS + C_HEADS
SEQ_MULTIPLE = 2048
RMS_EPS = 1e-6

LANES_V7X = 128
BF16_SUBLANES_V7X = 16
VMEM_LIMIT_BYTES_V7X = 56 * 1024 * 1024

ROW_TILE = 512
C_WIDTH = 3 * C_HEADS_PER_GROUP * HEAD_DIM

Z_GATES = 0
Z_A = 3 * D_MODEL
Z_BQ = Z_A + 3 * A_HEADS * HEAD_DIM
Z_BK = Z_BQ + B_HEADS * HEAD_DIM
Z_BV = Z_BK + B_KV_HEADS * HEAD_DIM
Z_C0 = Z_BV + B_KV_HEADS * HEAD_DIM
Z_WIDTH = Z_C0 + C_WIDTH
IN_WIDTH = Z_WIDTH + (len(C_GROUPS) - 1) * C_WIDTH
B_HEAD_ORDER = (0, 4, 1, 5, 2, 6, 3, 7)

NEG_BIG = -1e30
QK_SCALE = HEAD_DIM ** -0.5
LOG2E = 1.4426950408889634


def _alibi_slopes():
    n = N_ALIBI_HEADS
    return [float(2.0 ** (-8.0 * (i + 1) / n)) for i in range(n)]


_SLOPES = _alibi_slopes()
SLOPES_B = _SLOPES[:B_HEADS]
SLOPES_C = _SLOPES[B_HEADS:B_HEADS + C_HEADS]
SLOPES_A = _SLOPES[B_HEADS + C_HEADS:]


def _dot(a, b):
    return jnp.dot(a, b, preferred_element_type=jnp.float32)


def _dot_nt(a, b):
    return lax.dot_general(a, b, (((1,), (1,)), ((), ())), preferred_element_type=jnp.float32)


def _params(semantics):
    return pltpu.CompilerParams(dimension_semantics=semantics, vmem_limit_bytes=VMEM_LIMIT_BYTES_V7X)


def _rmsnorm(x, g):
    return x * lax.rsqrt(jnp.mean(x * x, axis=-1, keepdims=True) + RMS_EPS) * g


def _ada_kernel(c_ref, w_ref, b_ref, o_ref):
    c = c_ref[...]
    sc = c * jax.nn.sigmoid(c)
    o_ref[...] = jnp.sum(w_ref[...] * sc, axis=0, keepdims=True) + b_ref[...]


def _ada_call(c, w_ada, b_ada):
    depth, d, n = w_ada.shape
    tn = 1536
    return pl.pallas_call(
        _ada_kernel,
        grid=(depth, n // tn),
        in_specs=[
            pl.BlockSpec((d, 1), lambda l, j: (0, 0)),
            pl.BlockSpec((None, d, tn), lambda l, j: (l, 0, j)),
            pl.BlockSpec((None, 1, tn), lambda l, j: (l, 0, j)),
        ],
        out_specs=pl.BlockSpec((None, 1, tn), lambda l, j: (l, 0, j)),
        out_shape=jax.ShapeDtypeStruct((depth, 1, n), jnp.float32),
        compiler_params=_params(("arbitrary", "arbitrary")),
        name="adaln_mod",
    )(c.reshape(d, 1), w_ada, b_ada.reshape(depth, 1, n))


IN_PROJ_CHUNK = 512


def _in_proj_kernel(h_ref, g_ref, mod_ref, w_ref, z_ref, zc1_ref, zc2_ref, tmp_ref):
    x = h_ref[...]
    u = _rmsnorm(x, g_ref[...]) * (1.0 + mod_ref[1:2, :]) + mod_ref[0:1, :]
    u = u.astype(jnp.bfloat16)
    for c0 in range(0, Z_WIDTH, IN_PROJ_CHUNK):
        z_ref[:, c0:c0 + IN_PROJ_CHUNK] = _dot(u, w_ref[:, c0:c0 + IN_PROJ_CHUNK]).astype(z_ref.dtype)
    for g, out_ref in ((1, zc1_ref), (2, zc2_ref)):
        c0 = Z_WIDTH + (g - 1) * C_WIDTH
        res = _dot(u, w_ref[:, c0:c0 + C_WIDTH])
        dil = C_GROUPS[g][1]
        rows = x.shape[0] // dil
        ln = LANES_V7X
        for c in range(C_WIDTH // ln):
            tmp_ref[c] = res[:, c * ln:(c + 1) * ln]
        for r in range(dil):
            for c in range(C_WIDTH // ln):
                out_ref[r, :, c * ln:(c + 1) * ln] = tmp_ref[c, pl.ds(r, rows, stride=dil), :].astype(out_ref.dtype)


def _in_proj_call(h, g, mod, w, layer):
    s, d = h.shape
    tm = ROW_TILE
    d1, d2 = C_GROUPS[1][1], C_GROUPS[2][1]
    assert tm % (d2 * BF16_SUBLANES_V7X) == 0
    return pl.pallas_call(
        _in_proj_kernel,
        grid=(s // tm,),
        in_specs=[
            pl.BlockSpec((tm, d), lambda i: (i, 0)),
            pl.BlockSpec((1, d), lambda i: (0, 0)),
            pl.BlockSpec((6, d), lambda i: (0, 0)),
            pl.BlockSpec((None, d, IN_WIDTH), lambda i: (layer, 0, 0), pipeline_mode=pl.Buffered(1)),
        ],
        out_specs=[
            pl.BlockSpec((tm, Z_WIDTH), lambda i: (i, 0)),
            pl.BlockSpec((d1, tm // d1, C_WIDTH), lambda i: (0, i, 0)),
            pl.BlockSpec((d2, tm // d2, C_WIDTH), lambda i: (0, i, 0)),
        ],
        out_shape=[
            jax.ShapeDtypeStruct((s, Z_WIDTH), jnp.bfloat16),
            jax.ShapeDtypeStruct((d1, s // d1, C_WIDTH), jnp.bfloat16),
            jax.ShapeDtypeStruct((d2, s // d2, C_WIDTH), jnp.bfloat16),
        ],
        scratch_shapes=[pltpu.VMEM((C_WIDTH // LANES_V7X, tm, LANES_V7X), jnp.float32)],
        compiler_params=_params(("arbitrary",)),
        name="in_proj",
    )(h, g, mod, w)


SUBLANES_V7X = 8
MOBA_V_ROWS = HEAD_DIM + BF16_SUBLANES_V7X
MOBA_GROUPS = MOBA_BLOCK // SUBLANES_V7X
MOBA_UNROLL_LOG2 = 1
MOBA_UNROLL = 1 << MOBA_UNROLL_LOG2
MOBA_TILES = 4


def _all_sublanes_max(x):
    for shift in (4, 2, 1):
        x = jnp.maximum(x, pltpu.roll(x, shift, axis=0))
    return x


def _moba_lanes():
    lane = lax.broadcasted_iota(jnp.int32, (MOBA_BLOCK, LANES_V7X), 1)
    own = [lane < HEAD_DIM, lane >= HEAD_DIM]
    spare = [HEAD_DIM, 0]
    bias_lanes = [(lane == spare[h]) | (lane == spare[h] + 1) for h in (0, 1)]
    return lane, own, spare, bias_lanes


def _moba_kernel(slopes_ref, q_ref, k_ref, v_ref, o_ref, vt_ref, km_ref, ka_ref, sel_ref, s_ref, e_ref, st_ref, acc_ref):
    p = pl.program_id(0)
    nblk = vt_ref.shape[0]
    blk = MOBA_BLOCK
    half = HEAD_DIM
    _, own, _, bias_lanes = _moba_lanes()
    ones = jnp.ones((BF16_SUBLANES_V7X, blk), jnp.bfloat16)

    def prepare(j, carry):
        rows = pl.ds(pl.multiple_of(j * blk, blk), blk)
        vt = v_ref[rows, :].astype(jnp.float32).T.astype(jnp.bfloat16)
        k2 = k_ref[rows, :]
        km_ref[pl.ds(j, 1), :] = jnp.mean(k2.astype(jnp.float32), axis=0, keepdims=True)
        k2f = k2.astype(jnp.float32)
        pos = lax.broadcasted_iota(jnp.int32, k2.shape, 0).astype(jnp.float32)
        for h in (0, 1):
            vt_ref[j, h, 0:half, :] = vt[h * half:(h + 1) * half, :]
            vt_ref[j, h, half:, :] = ones
            ka = jnp.where(own[h], k2f, jnp.where(bias_lanes[h], pos, 0.0))
            ka_ref[h, rows, :] = ka.astype(jnp.bfloat16)
        return carry

    lax.fori_loop(0, nblk, prepare, 0)
    for w in range(MOBA_TILES):
        _moba_select(w, w, q_ref, km_ref, sel_ref)

    def tiles(t, carry):
        _moba_tiles(t, p, slopes_ref, q_ref, o_ref, vt_ref, km_ref, ka_ref, sel_ref, s_ref, e_ref, st_ref, acc_ref)
        return carry

    lax.fori_loop(0, nblk // MOBA_TILES, tiles, 0)


def _moba_select(i, w, q_ref, km_ref, sel_ref):
    nblk, blk = sel_ref.shape[1], MOBA_BLOCK
    _, own, _, _ = _moba_lanes()
    q2 = q_ref[pl.ds(pl.multiple_of(i * blk, blk), blk), :]
    blk_id = lax.broadcasted_iota(jnp.int32, (nblk, blk), 0)
    km = km_ref[...]
    km_hi = km.astype(jnp.bfloat16)
    km_lo = (km - km_hi.astype(jnp.float32)).astype(jnp.bfloat16)
    for h in (0, 1):
        qz = jnp.where(own[h], q2, jnp.zeros_like(q2))
        gate = _dot_nt(km_hi, qz) + _dot_nt(km_lo, qz)
        gate = jnp.where(blk_id < i, gate, -jnp.inf)
        sel = jnp.full((nblk, blk), NEG_BIG, jnp.float32)
        for _ in range(MOBA_TOPK):
            mx = jnp.max(gate, axis=0, keepdims=True)
            cand = (gate == mx) & (mx > -jnp.inf)
            idx = jnp.min(jnp.where(cand, blk_id, nblk), axis=0, keepdims=True)
            chosen = blk_id == idx
            sel = jnp.where(chosen, 0.0, sel)
            gate = jnp.where(chosen, -jnp.inf, gate)
        sel_ref[2 * w + h] = sel


def _moba_tiles(t, p, slopes_ref, q_ref, o_ref, vt_ref, km_ref, ka_ref, sel_ref, s_ref, e_ref, st_ref, acc_ref):
    nblk = vt_ref.shape[0]
    blk = MOBA_BLOCK
    half = HEAD_DIM
    sub = SUBLANES_V7X
    tile3 = (MOBA_GROUPS, sub, blk)
    key_pos = lax.broadcasted_iota(jnp.int32, tile3, 0) * sub + lax.broadcasted_iota(jnp.int32, tile3, 1)
    qry_pos = lax.broadcasted_iota(jnp.int32, tile3, 2)
    lane, own, spare, _ = _moba_lanes()
    streams = range(2 * MOBA_TILES)
    tile_of = [t * MOBA_TILES + st // 2 for st in streams]
    head_of = [st % 2 for st in streams]
    last_tile = tile_of[-1]

    qh = []
    for st in streams:
        h = head_of[st]
        q2f = q_ref[pl.ds(pl.multiple_of(tile_of[st] * blk, blk), blk), :].astype(jnp.float32)
        hi = slopes_ref[A_HEADS + 2 * p + h]
        lo = slopes_ref[2 * A_HEADS + 2 * p + h]
        extra = jnp.where(lane == spare[h], hi, jnp.where(lane == spare[h] + 1, lo, 0.0))
        qh.append(jnp.where(own[h], q2f, extra).astype(jnp.bfloat16))

    unroll = MOBA_UNROLL
    acc_groups = MOBA_V_ROWS // sub

    def past_block(n):
        return jnp.clip(n - 1, 0, nblk - 1)

    def item_block(n, st, own_first):
        return tile_of[st] if own_first else past_block(n)

    def issue_scores(items, own_first):
        out = []
        for u, n in enumerate(items):
            per_stream = []
            for st in streams:
                b = item_block(n, st, own_first and u == 0)
                rows = pl.ds(pl.multiple_of(b * blk, blk), blk)
                per_stream.append(_dot_nt(ka_ref[head_of[st], rows, :], qh[st]).reshape(tile3))
            out.append(per_stream)
        return out

    def stage_scores(raws, slot, items, own_first):
        for u, n in enumerate(items):
            for st in streams:
                s = raws[u][st]
                if own_first and u == 0:
                    s = jnp.where(qry_pos >= key_pos, s, NEG_BIG)
                    row = jnp.zeros((sub, blk), jnp.float32)
                else:
                    b = past_block(n)
                    gap = ((tile_of[st] - b) * blk).astype(jnp.float32)
                    row = sel_ref[st, pl.ds(b, 1), :] - slopes_ref[2 * p + head_of[st]] * gap
                    row = jnp.broadcast_to(jnp.where(n <= tile_of[st], row, NEG_BIG), (sub, blk))
                s_ref[slot, u, st] = s
                st_ref[slot, u, st, 0] = _all_sublanes_max(jnp.max(s, axis=0)) + row
                st_ref[slot, u, st, 1] = row

    def exponentiate(group_slot, ms):
        new_ms, alphas = [], []
        for st in streams:
            m_new = ms[st]
            for u in range(unroll):
                m_new = jnp.maximum(m_new, st_ref[group_slot, u, st, 0])
            alphas.append(jnp.exp2(ms[st] - m_new))
            new_ms.append(m_new)
            for u in range(unroll):
                e = jnp.exp2(s_ref[group_slot, u, st] - (m_new - st_ref[group_slot, u, st, 1])[None])
                e_ref[group_slot, u, st] = e.reshape(blk, blk).astype(jnp.bfloat16)
        return new_ms, alphas

    def accumulate(group, slot, alphas):
        for st in streams:
            acc = alphas[st][None] * acc_ref[st]
            for u in range(unroll):
                n = group * unroll + u
                vblock = jnp.where(n == 0, tile_of[st], past_block(n))
                acc = acc + _dot(vt_ref[vblock, head_of[st]], e_ref[slot, u, st]).reshape(acc_groups, sub, blk)
            acc_ref[st] = acc

    def score_group(group, slot, own_first):
        items = [group * unroll + u for u in range(unroll)]
        raws = issue_scores(items, own_first)
        return lambda: stage_scores(raws, slot, items, own_first)

    def tick(g, slot, ms, alphas):
        items = [g * unroll + u for u in range(unroll)]
        raws = [[None] * len(streams) for _ in items]
        for st in streams:
            for u, n in enumerate(items):
                rows = pl.ds(pl.multiple_of(past_block(n) * blk, blk), blk)
                raws[u][st] = _dot_nt(ka_ref[head_of[st], rows, :], qh[st]).reshape(tile3)
            acc = alphas[st][None] * acc_ref[st]
            for u in range(unroll):
                n = (g - 2) * unroll + u
                vblock = jnp.where(n == 0, tile_of[st], past_block(n))
                acc = acc + _dot(vt_ref[vblock, head_of[st]], e_ref[slot, u, st]).reshape(acc_groups, sub, blk)
            acc_ref[st] = acc
        ms, alphas = exponentiate(1 - slot, ms)
        stage_scores(raws, slot, items, False)
        return ms, alphas

    n_st = len(streams)
    pairs = jnp.maximum(lax.shift_right_logical(last_tile + 2 * unroll, MOBA_UNROLL_LOG2 + 1), 1)
    ms = [jnp.full((sub, blk), -jnp.inf, jnp.float32) for _ in streams]
    acc_ref[...] = jnp.zeros(acc_ref.shape, jnp.float32)
    finish_scores0 = score_group(0, 0, True)
    finish_scores1 = score_group(1, 1, False)
    finish_scores0()
    ms, alphas = exponentiate(0, ms)
    finish_scores1()

    def two_ticks(k, carry):
        ms, alphas = tick(2 * k, 0, list(carry[:n_st]), list(carry[n_st:]))
        ms, alphas = tick(2 * k + 1, 1, ms, alphas)
        return tuple(ms + alphas)

    carry = lax.fori_loop(1, pairs, two_ticks, tuple(ms + alphas))
    ms, alphas = list(carry[:n_st]), list(carry[n_st:])
    accumulate(2 * pairs - 2, 0, alphas)
    ms, alphas = exponentiate(1, ms)
    for w in range(MOBA_TILES):
        _moba_select(jnp.minimum(tile_of[2 * w] + MOBA_TILES, nblk - 1), w, q_ref, km_ref, sel_ref)
    accumulate(2 * pairs - 1, 1, alphas)
    for w in range(MOBA_TILES):
        accs = [acc_ref[2 * w + h] for h in (0, 1)]
        o_t = jnp.concatenate([(acc[0:half // sub] / acc[half // sub][None]).reshape(half, blk) for acc in accs],
                              axis=0)
        o_ref[pl.ds(pl.multiple_of(tile_of[2 * w] * blk, blk), blk), :] = o_t.T.astype(o_ref.dtype)


def _moba_slopes():
    full = np.asarray([sl * LOG2E for sl in SLOPES_A], np.float32)
    hi = full.astype(jnp.bfloat16).astype(np.float32)
    lo = (full - hi).astype(jnp.bfloat16).astype(np.float32)
    return jnp.asarray(np.concatenate([full, hi, lo]))


def _moba_call(z):
    s = z.shape[0]
    nblk = s // MOBA_BLOCK
    pairs = A_HEADS // 2
    ln = LANES_V7X
    qb, kb, vb = Z_A // ln, Z_A // ln + pairs, Z_A // ln + 2 * pairs
    return pl.pallas_call(
        _moba_kernel,
        grid=(pairs,),
        in_specs=[
            pl.BlockSpec(memory_space=pltpu.SMEM),
            pl.BlockSpec((s, ln), lambda p: (0, qb + p), pipeline_mode=pl.Buffered(1)),
            pl.BlockSpec((s, ln), lambda p: (0, kb + p), pipeline_mode=pl.Buffered(1)),
            pl.BlockSpec((s, ln), lambda p: (0, vb + p), pipeline_mode=pl.Buffered(1)),
        ],
        out_specs=pl.BlockSpec((s, ln), lambda p: (0, p)),
        out_shape=jax.ShapeDtypeStruct((s, A_HEADS * HEAD_DIM), jnp.bfloat16),
        scratch_shapes=[
            pltpu.VMEM((nblk, 2, MOBA_V_ROWS, MOBA_BLOCK), jnp.bfloat16),
            pltpu.VMEM((nblk, ln), jnp.float32),
            pltpu.VMEM((2, s, ln), jnp.bfloat16),
            pltpu.VMEM((2 * MOBA_TILES, nblk, MOBA_BLOCK), jnp.float32),
            pltpu.VMEM((2, MOBA_UNROLL, 2 * MOBA_TILES, MOBA_GROUPS, SUBLANES_V7X, MOBA_BLOCK), jnp.float32),
            pltpu.VMEM((2, MOBA_UNROLL, 2 * MOBA_TILES, MOBA_BLOCK, MOBA_BLOCK), jnp.bfloat16),
            pltpu.VMEM((2, MOBA_UNROLL, 2 * MOBA_TILES, 2, SUBLANES_V7X, MOBA_BLOCK), jnp.float32),
            pltpu.VMEM((2 * MOBA_TILES, MOBA_V_ROWS // SUBLANES_V7X, SUBLANES_V7X, MOBA_BLOCK), jnp.float32),
        ],
        compiler_params=_params(("arbitrary",)),
        name="moba_attn",
    )(_moba_slopes(), z, z, z)


def _blocks_per_step(nb):
    return max(c for c in (4, 2, 1) if nb % c == 0)


def _band_penalty_table(slope_dils, max_steps):
    steps = np.arange(BAND)[:, None] + BAND - np.arange(2 * BAND)[None, :]
    inside = (steps >= 0) & (steps <= max_steps)
    table = [np.where(inside, np.float32(sd) * steps.astype(np.float32), np.float32(-NEG_BIG)) for sd in slope_dils]
    return jnp.asarray(np.stack(table).astype(np.float32))


def _band_no_prev(first_step):
    kj = lax.broadcasted_iota(jnp.int32, (BAND, 2 * BAND), 1)
    return jnp.logical_and(first_step, kj < BAND)


def _band_units(units, lookahead):
    raws = {j: _dot_nt(units[j][0], units[j][1]) for j in range(min(lookahead, len(units)))}
    outs, lses = [], []
    for j, (_, _, vv, penalty, drop, sink) in enumerate(units):
        ahead = j + lookahead
        if ahead < len(units):
            raws[ahead] = _dot_nt(units[ahead][0], units[ahead][1])
        s = raws.pop(j) - penalty
        if drop is not None:
            s = jnp.where(drop, NEG_BIG, s)
        m = jnp.max(s, axis=1, keepdims=True)
        if sink is not None:
            m = jnp.maximum(m, sink)
        e = jnp.exp(s - m)
        denom = jnp.sum(e, axis=1, keepdims=True)
        if sink is not None:
            denom = denom + jnp.exp(sink - m)
        outs.append(_dot(e.astype(jnp.bfloat16), vv) / denom)
        lses.append(m + jnp.log(denom))
    return outs, lses


def _swa_kernel(sinks_ref, bias_ref, q_ref, kp_ref, ko_ref, vp_ref, vo_ref, o_ref):
    n = pl.program_id(0)
    nbq = q_ref.shape[0] // BAND
    no_prev = _band_no_prev(n == 0)
    k_all = jnp.concatenate([kp_ref[...], ko_ref[...]], axis=0)
    v_all = jnp.concatenate([vp_ref[...], vo_ref[...]], axis=0)
    ln = LANES_V7X
    lane = lax.broadcasted_iota(jnp.int32, (BAND, ln), 1)
    low = lane < HEAD_DIM
    units = []
    for qb in range(nbq):
        kk = k_all[qb * BAND:(qb + 2) * BAND]
        vv = v_all[qb * BAND:(qb + 2) * BAND]
        drop = no_prev if qb == 0 else None
        for b in range(B_HEADS // 2):
            q2 = q_ref[qb * BAND:(qb + 1) * BAND, b * ln:(b + 1) * ln]
            for c in range(2):
                head = B_HEAD_ORDER[2 * b + c]
                qm = jnp.where(low if c == 0 else jnp.logical_not(low), q2, jnp.zeros_like(q2))
                units.append((qm, kk, vv, bias_ref[2 * b + c], drop, sinks_ref[head]))
    outs, _ = _band_units(units, B_HEADS)
    for qb in range(nbq):
        for b in range(B_HEADS // 2):
            j = qb * B_HEADS + 2 * b
            o_ref[qb * BAND:(qb + 1) * BAND, b * ln:(b + 1) * ln] = (
                jnp.where(low, outs[j], outs[j + 1]).astype(o_ref.dtype))


def _swa_call(z, sinks):
    s = z.shape[0]
    nbq = _blocks_per_step(s // BAND)
    rows = nbq * BAND
    ln = LANES_V7X
    qw = B_HEADS * HEAD_DIM
    prev = lambda n: jnp.maximum(n * nbq - 1, 0)
    bias = _band_penalty_table([SLOPES_B[head] for head in B_HEAD_ORDER], B_WINDOW - 1)
    return pl.pallas_call(
        _swa_kernel,
        grid=(s // rows,),
        in_specs=[
            pl.BlockSpec(memory_space=pltpu.SMEM),
            pl.BlockSpec(bias.shape, lambda n: (0, 0, 0), pipeline_mode=pl.Buffered(1)),
            pl.BlockSpec((rows, qw), lambda n: (n, Z_BQ // qw)),
            pl.BlockSpec((BAND, ln), lambda n: (prev(n), Z_BK // ln)),
            pl.BlockSpec((rows, ln), lambda n: (n, Z_BK // ln)),
            pl.BlockSpec((BAND, ln), lambda n: (prev(n), Z_BV // ln)),
            pl.BlockSpec((rows, ln), lambda n: (n, Z_BV // ln)),
        ],
        out_specs=pl.BlockSpec((rows, qw), lambda n: (n, 0)),
        out_shape=jax.ShapeDtypeStruct((s, qw), jnp.bfloat16),
        compiler_params=_params(("arbitrary",)),
        name="swa_attn",
    )(sinks, bias, z, z, z, z, z)


def _dilated_kernel(bias_ref, q_ref, kp_ref, ko_ref, vp_ref, vo_ref, o_ref, lse_ref):
    n = pl.program_id(1)
    nbq = q_ref.shape[0] // BAND
    no_prev = _band_no_prev(n == 0)
    k_all = jnp.concatenate([kp_ref[...], ko_ref[...]], axis=0)
    v_all = jnp.concatenate([vp_ref[...], vo_ref[...]], axis=0)
    cw = q_ref.shape[1]
    lane = lax.broadcasted_iota(jnp.int32, (BAND, cw), 1)
    heads = range(C_HEADS_PER_GROUP)
    mine = [(lane >= h * HEAD_DIM) & (lane < (h + 1) * HEAD_DIM) for h in heads]
    biases = [bias_ref[h] for h in heads]
    units = []
    for qb in range(nbq):
        q4 = q_ref[qb * BAND:(qb + 1) * BAND, :]
        kk = k_all[qb * BAND:(qb + 2) * BAND]
        vv = v_all[qb * BAND:(qb + 2) * BAND]
        drop = no_prev if qb == 0 else None
        for h in heads:
            units.append((jnp.where(mine[h], q4, jnp.zeros_like(q4)), kk, vv, biases[h], drop, None))
    outs, lses = _band_units(units, C_HEADS_PER_GROUP)
    for qb in range(nbq):
        o_all = jnp.zeros((BAND, cw), jnp.float32)
        lse_all = jnp.zeros((BAND, cw), jnp.float32)
        for h in heads:
            j = qb * C_HEADS_PER_GROUP + h
            o_all = jnp.where(mine[h], outs[j], o_all)
            lse_all = jnp.where(mine[h], lses[j], lse_all)
        o_ref[qb * BAND:(qb + 1) * BAND, :] = o_all
        lse_ref[qb * BAND:(qb + 1) * BAND, :] = lse_all


def _dilated_call(zc, group, col0):
    window, dil = C_GROUPS[group]
    assert zc.shape[0] == dil
    length = zc.shape[1]
    nbq = _blocks_per_step(length // BAND)
    rows = nbq * BAND
    cw = C_HEADS_PER_GROUP * HEAD_DIM
    base = col0 // cw
    slopes = [SLOPES_C[group * C_HEADS_PER_GROUP + h] * dil for h in range(C_HEADS_PER_GROUP)]
    bias = _band_penalty_table(slopes, window // dil)
    own = lambda col: pl.BlockSpec((None, rows, cw), lambda r, n: (r, n, base + col))
    prev = lambda col: pl.BlockSpec((None, BAND, cw), lambda r, n: (r, jnp.maximum(n * nbq - 1, 0), base + col))
    out_blk = pl.BlockSpec((None, rows, cw), lambda r, n: (r, n, 0))
    return pl.pallas_call(
        _dilated_kernel,
        grid=(dil, length // rows),
        in_specs=[pl.BlockSpec(bias.shape, lambda r, n: (0, 0, 0), pipeline_mode=pl.Buffered(1)),
                  own(0), prev(1), own(1), prev(2), own(2)],
        out_specs=[out_blk, out_blk],
        out_shape=[jax.ShapeDtypeStruct((dil, length, cw), jnp.float32)] * 2,
        compiler_params=_params(("arbitrary", "arbitrary")),
        name=f"dilated_attn_g{group}",
    )(bias, zc, zc, zc, zc, zc)


def _merge_kernel(h_ref, zg_ref, oa_ref, ob_ref, o0_ref, l0_ref, o1_ref, l1_ref, o2_ref, l2_ref,
                  wa_ref, wb_ref, wc_ref, wo_ref, g_ref, mod_ref, out_ref, nat_ref):
    tm = h_ref.shape[0]

    def natural(ref, slot):
        dil = ref.shape[0]
        if dil == 1:
            return ref[0]
        ln = LANES_V7X
        chunks = ref.shape[2] // ln
        for r in range(dil):
            for c in range(chunks):
                nat_ref[slot * chunks + c, pl.ds(r, tm // dil, stride=dil), :] = ref[r, :, c * ln:(c + 1) * ln]
        return jnp.concatenate([nat_ref[slot * chunks + c] for c in range(chunks)], axis=1)

    l0, l1, l2 = natural(l0_ref, 0), natural(l1_ref, 0), natural(l2_ref, 1)
    o0, o1, o2 = natural(o0_ref, 0), natural(o1_ref, 2), natural(o2_ref, 3)
    mx = jnp.maximum(jnp.maximum(l0, l1), l2)
    e0, e1, e2 = jnp.exp(l0 - mx), jnp.exp(l1 - mx), jnp.exp(l2 - mx)
    oc = ((e0 * o0 + e1 * o1 + e2 * o2) / (e0 + e1 + e2)).astype(jnp.bfloat16)
    d = D_MODEL
    subs = range(0, tm, MERGE_SUB_ROWS)
    wa, wb, wc, wo = [ref[...].astype(jnp.bfloat16) for ref in (wa_ref, wb_ref, wc_ref, wo_ref)]
    branches = [(_dot(oa_ref[pl.ds(r0, MERGE_SUB_ROWS), :], wa),
                 _dot(ob_ref[pl.ds(r0, MERGE_SUB_ROWS), :], wb),
                 _dot(oc[r0:r0 + MERGE_SUB_ROWS], wc)) for r0 in subs]
    ys = []
    for r0, (ya, yb, yc) in zip(subs, branches):
        gates = jax.nn.sigmoid(zg_ref[pl.ds(r0, MERGE_SUB_ROWS), :].astype(jnp.float32))
        merged = gates[:, 0:d] * ya + gates[:, d:2 * d] * yb + gates[:, 2 * d:3 * d] * yc
        ys.append(_dot(merged.astype(jnp.bfloat16), wo))
    for r0, y in zip(subs, ys):
        rs = pl.ds(r0, MERGE_SUB_ROWS)
        out_ref[rs, :] = h_ref[rs, :] + mod_ref[2:3, :] * _rmsnorm(y, g_ref[...])


MERGE_SUB_ROWS = 256


def _merge_call(h, z, oa, ob, oc_lse, wa, wb, wc, wo, g, mod, layer):
    s, d = h.shape
    tm = 2 * MERGE_SUB_ROWS
    cw = C_HEADS_PER_GROUP * HEAD_DIM
    row = lambda w: pl.BlockSpec((tm, w), lambda i: (i, 0))
    res = lambda a: pl.BlockSpec((a.shape[0], tm // a.shape[0], cw), lambda i: (0, i, 0))
    full = lambda a: pl.BlockSpec(a.shape, lambda i: (0, 0), pipeline_mode=pl.Buffered(1))
    stacked = lambda a: pl.BlockSpec((None,) + a.shape[1:], lambda i: (layer, 0, 0), pipeline_mode=pl.Buffered(1))
    return pl.pallas_call(
        _merge_kernel,
        grid=(s // tm,),
        in_specs=[row(d), row(3 * d), row(oa.shape[1]), row(ob.shape[1])] + [res(a) for a in oc_lse]
        + [stacked(wa), stacked(wb), stacked(wc), stacked(wo), full(g), full(mod)],
        out_specs=row(d),
        out_shape=jax.ShapeDtypeStruct((s, d), jnp.float32),
        scratch_shapes=[pltpu.VMEM((4 * cw // LANES_V7X, tm, LANES_V7X), jnp.float32)],
        input_output_aliases={0: 0} if layer > 0 else {},
        compiler_params=_params(("arbitrary",)),
        name="merge_out_proj",
    )(h, z, oa, ob, *oc_lse, wa, wb, wc, wo, g, mod)


FFN_CHUNK = 1408


def _ffn_kernel(h_ref, g_pre_ref, g_post_ref, mod_ref, wg_ref, wu_ref, wd_ref, out_ref):
    x = h_ref[...]
    u = _rmsnorm(x, g_pre_ref[...]) * (1.0 + mod_ref[4:5, :]) + mod_ref[3:4, :]
    u = u.astype(jnp.bfloat16)
    y = jnp.zeros(x.shape, jnp.float32)
    for c0 in range(0, D_FF, FFN_CHUNK):
        gate = _dot(u, wg_ref[:, c0:c0 + FFN_CHUNK].astype(jnp.bfloat16))
        up = _dot(u, wu_ref[:, c0:c0 + FFN_CHUNK].astype(jnp.bfloat16))
        act = (gate * jax.nn.sigmoid(gate) * up).astype(jnp.bfloat16)
        y = y + _dot(act, wd_ref[c0:c0 + FFN_CHUNK, :].astype(jnp.bfloat16))
    out_ref[...] = x + mod_ref[5:6, :] * _rmsnorm(y, g_post_ref[...])


def _ffn_call(h, g_pre, g_post, mod, wg, wu, wd, layer):
    s, d = h.shape
    tm = ROW_TILE
    row = pl.BlockSpec((tm, d), lambda i: (i, 0))
    full = lambda a: pl.BlockSpec(a.shape, lambda i: (0, 0), pipeline_mode=pl.Buffered(1))
    stacked = lambda a: pl.BlockSpec((None,) + a.shape[1:], lambda i: (layer, 0, 0), pipeline_mode=pl.Buffered(1))
    return pl.pallas_call(
        _ffn_kernel,
        grid=(s // tm,),
        in_specs=[row, full(g_pre), full(g_post), full(mod), stacked(wg), stacked(wu), stacked(wd)],
        out_specs=row,
        out_shape=jax.ShapeDtypeStruct((s, d), jnp.float32),
        input_output_aliases={0: 0},
        compiler_params=_params(("arbitrary",)),
        name="swiglu_ffn",
    )(h, g_pre, g_post, mod, wg, wu, wd)


def _permute_in_proj(w_in):
    hd = HEAD_DIM
    o_bq = 3 * A_HEADS * hd
    o_bkv = o_bq + B_HEADS * hd
    o_c = o_bkv + 2 * B_KV_HEADS * hd
    o_g = o_c + 3 * C_HEADS * hd
    cw = C_HEADS_PER_GROUP * hd
    sl = lambda a, b: w_in[:, :, a:b]
    parts = [sl(o_g, o_g + 3 * D_MODEL), sl(0, A_HEADS * hd) * (QK_SCALE * LOG2E), sl(A_HEADS * hd, o_bq)]
    parts += [sl(o_bq + hh * hd, o_bq + (hh + 1) * hd) * QK_SCALE for hh in B_HEAD_ORDER]
    parts += [sl(o_bkv, o_c)]
    for g in range(len(C_GROUPS)):
        parts += [sl(o_c + g * 3 * cw, o_c + g * 3 * cw + cw) * QK_SCALE,
                  sl(o_c + g * 3 * cw + cw, o_c + (g + 1) * 3 * cw)]
    out = jnp.concatenate([part.astype(jnp.bfloat16) for part in parts], axis=2)
    assert out.shape[2] == IN_WIDTH
    return out


def kernel(x, c, w_ada, b_ada, g_pre_mix, g_post_mix, w_in, sinks, w_br_a, w_br_b, w_br_c,
           w_out, g_pre_ffn, g_post_ffn, w_gate, w_up, w_down):
    bn, s, d = x.shape
    assert bn == 1 and d == D_MODEL and s % SEQ_MULTIPLE == 0
    hd = HEAD_DIM
    w_in_p = _permute_in_proj(w_in)
    w_br_b_p = jnp.concatenate([w_br_b[:, hh * hd:(hh + 1) * hd] for hh in B_HEAD_ORDER], axis=1)

    mod_all = _ada_call(c, w_ada, b_ada).reshape(DEPTH, 6, d)
    h = x.reshape(s, d)
    for l in range(DEPTH):
        mod = mod_all[l]
        z, zc1, zc2 = _in_proj_call(h, g_pre_mix[l].reshape(1, d), mod, w_in_p, l)
        oa = _moba_call(z)
        ob = _swa_call(z, sinks[l])
        oc_lse = (_dilated_call(z.reshape(1, s, Z_WIDTH), 0, Z_C0)
                  + _dilated_call(zc1, 1, 0) + _dilated_call(zc2, 2, 0))
        h = _merge_call(h, z, oa, ob, oc_lse, w_br_a, w_br_b_p, w_br_c, w_out,
                        g_post_mix[l].reshape(1, d), mod, l)
        h = _ffn_call(h, g_pre_ffn[l].reshape(1, d), g_post_ffn[l].reshape(1, d), mod,
                      w_gate, w_up, w_down, l)
    return h.reshape(bn, s, d)
```

```python
import numpy as np
import jax
import jax.numpy as jnp
from jax import lax
from jax.experimental import pallas as pl
from jax.experimental.pallas import tpu as pltpu

D_MODEL = 1024
DEPTH = 4
HEAD_DIM = 64
A_HEADS = 8
MOBA_BLOCK = 256
MOBA_TOPK = 3
B_HEADS = 8
B_KV_HEADS = 2
B_WINDOW = 128
C_GROUPS = ((128, 1), (512, 4), (2048, 16))
C_HEADS_PER_GROUP = 4
C_HEADS = len(C_GROUPS) * C_HEADS_PER_GROUP
BAND = 128
D_FF = 2816
N_ALIBI_HEADS = A_HEADS + B_HEADS + C_HEADS
SEQ_MULTIPLE = 2048
RMS_EPS = 1e-6

LANES_V7X = 128
BF16_SUBLANES_V7X = 16
VMEM_LIMIT_BYTES_V7X = 56 * 1024 * 1024

ROW_TILE = 512
C_WIDTH = 3 * C_HEADS_PER_GROUP * HEAD_DIM

Z_GATES = 0
Z_A = 3 * D_MODEL
Z_BQ = Z_A + 3 * A_HEADS * HEAD_DIM
Z_BK = Z_BQ + B_HEADS * HEAD_DIM
Z_BV = Z_BK + B_KV_HEADS * HEAD_DIM
Z_C0 = Z_BV + B_KV_HEADS * HEAD_DIM
Z_WIDTH = Z_C0 + C_WIDTH
IN_WIDTH = Z_WIDTH + (len(C_GROUPS) - 1) * C_WIDTH
B_HEAD_ORDER = (0, 4, 1, 5, 2, 6, 3, 7)

NEG_BIG = -1e30
QK_SCALE = HEAD_DIM ** -0.5
LOG2E = 1.4426950408889634


def _alibi_slopes():
    n = N_ALIBI_HEADS
    return [float(2.0 ** (-8.0 * (i + 1) / n)) for i in range(n)]


_SLOPES = _alibi_slopes()
SLOPES_B = _SLOPES[:B_HEADS]
SLOPES_C = _SLOPES[B_HEADS:B_HEADS + C_HEADS]
SLOPES_A = _SLOPES[B_HEADS + C_HEADS:]


def _dot(a, b):
    return jnp.dot(a, b, preferred_element_type=jnp.float32)


def _dot_nt(a, b):
    return lax.dot_general(a, b, (((1,), (1,)), ((), ())), preferred_element_type=jnp.float32)


def _params(semantics):
    return pltpu.CompilerParams(dimension_semantics=semantics, vmem_limit_bytes=VMEM_LIMIT_BYTES_V7X)


def _rmsnorm(x, g):
    return x * lax.rsqrt(jnp.mean(x * x, axis=-1, keepdims=True) + RMS_EPS) * g


def _ada_kernel(c_ref, w_ref, b_ref, o_ref):
    c = c_ref[...]
    sc = c * jax.nn.sigmoid(c)
    o_ref[...] = jnp.sum(w_ref[...] * sc, axis=0, keepdims=True) + b_ref[...]


def _ada_call(c, w_ada, b_ada):
    depth, d, n = w_ada.shape
    tn = 1536
    return pl.pallas_call(
        _ada_kernel,
        grid=(depth, n // tn),
        in_specs=[
            pl.BlockSpec((d, 1), lambda l, j: (0, 0)),
            pl.BlockSpec((None, d, tn), lambda l, j: (l, 0, j)),
            pl.BlockSpec((None, 1, tn), lambda l, j: (l, 0, j)),
        ],
        out_specs=pl.BlockSpec((None, 1, tn), lambda l, j: (l, 0, j)),
        out_shape=jax.ShapeDtypeStruct((depth, 1, n), jnp.float32),
        compiler_params=_params(("arbitrary", "arbitrary")),
        name="adaln_mod",
    )(c.reshape(d, 1), w_ada, b_ada.reshape(depth, 1, n))


IN_PROJ_CHUNK = 512


def _in_proj_kernel(h_ref, g_ref, mod_ref, w_ref, z_ref, zc1_ref, zc2_ref, tmp_ref):
    x = h_ref[...]
    u = _rmsnorm(x, g_ref[...]) * (1.0 + mod_ref[1:2, :]) + mod_ref[0:1, :]
    u = u.astype(jnp.bfloat16)
    for c0 in range(0, Z_WIDTH, IN_PROJ_CHUNK):
        z_ref[:, c0:c0 + IN_PROJ_CHUNK] = _dot(u, w_ref[:, c0:c0 + IN_PROJ_CHUNK]).astype(z_ref.dtype)
    for g, out_ref in ((1, zc1_ref), (2, zc2_ref)):
        c0 = Z_WIDTH + (g - 1) * C_WIDTH
        res = _dot(u, w_ref[:, c0:c0 + C_WIDTH])
        dil = C_GROUPS[g][1]
        rows = x.shape[0] // dil
        ln = LANES_V7X
        for c in range(C_WIDTH // ln):
            tmp_ref[c] = res[:, c * ln:(c + 1) * ln]
        for r in range(dil):
            for c in range(C_WIDTH // ln):
                out_ref[r, :, c * ln:(c + 1) * ln] = tmp_ref[c, pl.ds(r, rows, stride=dil), :].astype(out_ref.dtype)


def _in_proj_call(h, g, mod, w, layer):
    s, d = h.shape
    tm = ROW_TILE
    d1, d2 = C_GROUPS[1][1], C_GROUPS[2][1]
    assert tm % (d2 * BF16_SUBLANES_V7X) == 0
    return pl.pallas_call(
        _in_proj_kernel,
        grid=(s // tm,),
        in_specs=[
            pl.BlockSpec((tm, d), lambda i: (i, 0)),
            pl.BlockSpec((1, d), lambda i: (0, 0)),
            pl.BlockSpec((6, d), lambda i: (0, 0)),
            pl.BlockSpec((None, d, IN_WIDTH), lambda i: (layer, 0, 0), pipeline_mode=pl.Buffered(1)),
        ],
        out_specs=[
            pl.BlockSpec((tm, Z_WIDTH), lambda i: (i, 0)),
            pl.BlockSpec((d1, tm // d1, C_WIDTH), lambda i: (0, i, 0)),
            pl.BlockSpec((d2, tm // d2, C_WIDTH), lambda i: (0, i, 0)),
        ],
        out_shape=[
            jax.ShapeDtypeStruct((s, Z_WIDTH), jnp.bfloat16),
            jax.ShapeDtypeStruct((d1, s // d1, C_WIDTH), jnp.bfloat16),
            jax.ShapeDtypeStruct((d2, s // d2, C_WIDTH), jnp.bfloat16),
        ],
        scratch_shapes=[pltpu.VMEM((C_WIDTH // LANES_V7X, tm, LANES_V7X), jnp.float32)],
        compiler_params=_params(("arbitrary",)),
        name="in_proj",
    )(h, g, mod, w)


SUBLANES_V7X = 8
MOBA_V_ROWS = HEAD_DIM + BF16_SUBLANES_V7X
MOBA_GROUPS = MOBA_BLOCK // SUBLANES_V7X
MOBA_UNROLL_LOG2 = 1
MOBA_UNROLL = 1 << MOBA_UNROLL_LOG2
MOBA_TILES = 4


def _all_sublanes_max(x):
    for shift in (4, 2, 1):
        x = jnp.maximum(x, pltpu.roll(x, shift, axis=0))
    return x


def _moba_lanes():
    lane = lax.broadcasted_iota(jnp.int32, (MOBA_BLOCK, LANES_V7X), 1)
    own = [lane < HEAD_DIM, lane >= HEAD_DIM]
    spare = [HEAD_DIM, 0]
    bias_lanes = [(lane == spare[h]) | (lane == spare[h] + 1) for h in (0, 1)]
    return lane, own, spare, bias_lanes


def _moba_kernel(slopes_ref, q_ref, k_ref, v_ref, o_ref, vt_ref, km_ref, ka_ref, sel_ref, s_ref, e_ref, st_ref, acc_ref):
    p = pl.program_id(0)
    nblk = vt_ref.shape[0]
    blk = MOBA_BLOCK
    half = HEAD_DIM
    _, own, _, bias_lanes = _moba_lanes()
    ones = jnp.ones((BF16_SUBLANES_V7X, blk), jnp.bfloat16)

    def prepare(j, carry):
        rows = pl.ds(pl.multiple_of(j * blk, blk), blk)
        vt = v_ref[rows, :].astype(jnp.float32).T.astype(jnp.bfloat16)
        k2 = k_ref[rows, :]
        km_ref[pl.ds(j, 1), :] = jnp.mean(k2.astype(jnp.float32), axis=0, keepdims=True)
        k2f = k2.astype(jnp.float32)
        pos = lax.broadcasted_iota(jnp.int32, k2.shape, 0).astype(jnp.float32)
        for h in (0, 1):
            vt_ref[j, h, 0:half, :] = vt[h * half:(h + 1) * half, :]
            vt_ref[j, h, half:, :] = ones
            ka = jnp.where(own[h], k2f, jnp.where(bias_lanes[h], pos, 0.0))
            ka_ref[h, rows, :] = ka.astype(jnp.bfloat16)
        return carry

    lax.fori_loop(0, nblk, prepare, 0)
    for w in range(MOBA_TILES):
        _moba_select(w, w, q_ref, km_ref, sel_ref)

    def tiles(t, carry):
        _moba_tiles(t, p, slopes_ref, q_ref, o_ref, vt_ref, km_ref, ka_ref, sel_ref, s_ref, e_ref, st_ref, acc_ref)
        return carry

    lax.fori_loop(0, nblk // MOBA_TILES, tiles, 0)


def _moba_select(i, w, q_ref, km_ref, sel_ref):
    nblk, blk = sel_ref.shape[1], MOBA_BLOCK
    _, own, _, _ = _moba_lanes()
    q2 = q_ref[pl.ds(pl.multiple_of(i * blk, blk), blk), :]
    blk_id = lax.broadcasted_iota(jnp.int32, (nblk, blk), 0)
    km = km_ref[...]
    km_hi = km.astype(jnp.bfloat16)
    km_lo = (km - km_hi.astype(jnp.float32)).astype(jnp.bfloat16)
    for h in (0, 1):
        qz = jnp.where(own[h], q2, jnp.zeros_like(q2))
        gate = _dot_nt(km_hi, qz) + _dot_nt(km_lo, qz)
        gate = jnp.where(blk_id < i, gate, -jnp.inf)
        sel = jnp.full((nblk, blk), NEG_BIG, jnp.float32)
        for _ in range(MOBA_TOPK):
            mx = jnp.max(gate, axis=0, keepdims=True)
            cand = (gate == mx) & (mx > -jnp.inf)
            idx = jnp.min(jnp.where(cand, blk_id, nblk), axis=0, keepdims=True)
            chosen = blk_id == idx
            sel = jnp.where(chosen, 0.0, sel)
            gate = jnp.where(chosen, -jnp.inf, gate)
        sel_ref[2 * w + h] = sel


def _moba_tiles(t, p, slopes_ref, q_ref, o_ref, vt_ref, km_ref, ka_ref, sel_ref, s_ref, e_ref, st_ref, acc_ref):
    nblk = vt_ref.shape[0]
    blk = MOBA_BLOCK
    half = HEAD_DIM
    sub = SUBLANES_V7X
    tile3 = (MOBA_GROUPS, sub, blk)
    key_pos = lax.broadcasted_iota(jnp.int32, tile3, 0) * sub + lax.broadcasted_iota(jnp.int32, tile3, 1)
    qry_pos = lax.broadcasted_iota(jnp.int32, tile3, 2)
    lane, own, spare, _ = _moba_lanes()
    streams = range(2 * MOBA_TILES)
    tile_of = [t * MOBA_TILES + st // 2 for st in streams]
    head_of = [st % 2 for st in streams]
    last_tile = tile_of[-1]

    qh = []
    for st in streams:
        h = head_of[st]
        q2f = q_ref[pl.ds(pl.multiple_of(tile_of[st] * blk, blk), blk), :].astype(jnp.float32)
        hi = slopes_ref[A_HEADS + 2 * p + h]
        lo = slopes_ref[2 * A_HEADS + 2 * p + h]
        extra = jnp.where(lane == spare[h], hi, jnp.where(lane == spare[h] + 1, lo, 0.0))
        qh.append(jnp.where(own[h], q2f, extra).astype(jnp.bfloat16))

    unroll = MOBA_UNROLL
    acc_groups = MOBA_V_ROWS // sub

    def past_block(n):
        return jnp.clip(n - 1, 0, nblk - 1)

    def item_block(n, st, own_first):
        return tile_of[st] if own_first else past_block(n)

    def issue_scores(items, own_first):
        out = []
        for u, n in enumerate(items):
            per_stream = []
            for st in streams:
                b = item_block(n, st, own_first and u == 0)
                rows = pl.ds(pl.multiple_of(b * blk, blk), blk)
                per_stream.append(_dot_nt(ka_ref[head_of[st], rows, :], qh[st]).reshape(tile3))
            out.append(per_stream)
        return out

    def stage_scores(raws, slot, items, own_first):
        for u, n in enumerate(items):
            for st in streams:
                s = raws[u][st]
                if own_first and u == 0:
                    s = jnp.where(qry_pos >= key_pos, s, NEG_BIG)
                    row = jnp.zeros((sub, blk), jnp.float32)
                else:
                    b = past_block(n)
                    gap = ((tile_of[st] - b) * blk).astype(jnp.float32)
                    row = sel_ref[st, pl.ds(b, 1), :] - slopes_ref[2 * p + head_of[st]] * gap
                    row = jnp.broadcast_to(jnp.where(n <= tile_of[st], row, NEG_BIG), (sub, blk))
                s_ref[slot, u, st] = s
                st_ref[slot, u, st, 0] = _all_sublanes_max(jnp.max(s, axis=0)) + row
                st_ref[slot, u, st, 1] = row

    def exponentiate(group_slot, ms):
        new_ms, alphas = [], []
        for st in streams:
            m_new = ms[st]
            for u in range(unroll):
                m_new = jnp.maximum(m_new, st_ref[group_slot, u, st, 0])
            alphas.append(jnp.exp2(ms[st] - m_new))
            new_ms.append(m_new)
            for u in range(unroll):
                e = jnp.exp2(s_ref[group_slot, u, st] - (m_new - st_ref[group_slot, u, st, 1])[None])
                e_ref[group_slot, u, st] = e.reshape(blk, blk).astype(jnp.bfloat16)
        return new_ms, alphas

    def accumulate(group, slot, alphas):
        for st in streams:
            acc = alphas[st][None] * acc_ref[st]
            for u in range(unroll):
                n = group * unroll + u
                vblock = jnp.where(n == 0, tile_of[st], past_block(n))
                acc = acc + _dot(vt_ref[vblock, head_of[st]], e_ref[slot, u, st]).reshape(acc_groups, sub, blk)
            acc_ref[st] = acc

    def score_group(group, slot, own_first):
        items = [group * unroll + u for u in range(unroll)]
        raws = issue_scores(items, own_first)
        return lambda: stage_scores(raws, slot, items, own_first)

    def tick(g, slot, ms, alphas):
        items = [g * unroll + u for u in range(unroll)]
        raws = [[None] * len(streams) for _ in items]
        for st in streams:
            for u, n in enumerate(items):
                rows = pl.ds(pl.multiple_of(past_block(n) * blk, blk), blk)
                raws[u][st] = _dot_nt(ka_ref[head_of[st], rows, :], qh[st]).reshape(tile3)
            acc = alphas[st][None] * acc_ref[st]
            for u in range(unroll):
                n = (g - 2) * unroll + u
                vblock = jnp.where(n == 0, tile_of[st], past_block(n))
                acc = acc + _dot(vt_ref[vblock, head_of[st]], e_ref[slot, u, st]).reshape(acc_groups, sub, blk)
            acc_ref[st] = acc
        ms, alphas = exponentiate(1 - slot, ms)
        stage_scores(raws, slot, items, False)
        return ms, alphas

    n_st = len(streams)
    pairs = jnp.maximum(lax.shift_right_logical(last_tile + 2 * unroll, MOBA_UNROLL_LOG2 + 1), 1)
    ms = [jnp.full((sub, blk), -jnp.inf, jnp.float32) for _ in streams]
    acc_ref[...] = jnp.zeros(acc_ref.shape, jnp.float32)
    finish_scores0 = score_group(0, 0, True)
    finish_scores1 = score_group(1, 1, False)
    finish_scores0()
    ms, alphas = exponentiate(0, ms)
    finish_scores1()

    def two_ticks(k, carry):
        ms, alphas = tick(2 * k, 0, list(carry[:n_st]), list(carry[n_st:]))
        ms, alphas = tick(2 * k + 1, 1, ms, alphas)
        return tuple(ms + alphas)

    carry = lax.fori_loop(1, pairs, two_ticks, tuple(ms + alphas))
    ms, alphas = list(carry[:n_st]), list(carry[n_st:])
    accumulate(2 * pairs - 2, 0, alphas)
    ms, alphas = exponentiate(1, ms)
    for w in range(MOBA_TILES):
        _moba_select(jnp.minimum(tile_of[2 * w] + MOBA_TILES, nblk - 1), w, q_ref, km_ref, sel_ref)
    accumulate(2 * pairs - 1, 1, alphas)
    for w in range(MOBA_TILES):
        accs = [acc_ref[2 * w + h] for h in (0, 1)]
        o_t = jnp.concatenate([(acc[0:half // sub] / acc[half // sub][None]).reshape(half, blk) for acc in accs],
                              axis=0)
        o_ref[pl.ds(pl.multiple_of(tile_of[2 * w] * blk, blk), blk), :] = o_t.T.astype(o_ref.dtype)


def _moba_slopes():
    full = np.asarray([sl * LOG2E for sl in SLOPES_A], np.float32)
    hi = full.astype(jnp.bfloat16).astype(np.float32)
    lo = (full - hi).astype(jnp.bfloat16).astype(np.float32)
    return jnp.asarray(np.concatenate([full, hi, lo]))


def _moba_call(z):
    s = z.shape[0]
    nblk = s // MOBA_BLOCK
    pairs = A_HEADS // 2
    ln = LANES_V7X
    qb, kb, vb = Z_A // ln, Z_A // ln + pairs, Z_A // ln + 2 * pairs
    return pl.pallas_call(
        _moba_kernel,
        grid=(pairs,),
        in_specs=[
            pl.BlockSpec(memory_space=pltpu.SMEM),
            pl.BlockSpec((s, ln), lambda p: (0, qb + p), pipeline_mode=pl.Buffered(1)),
            pl.BlockSpec((s, ln), lambda p: (0, kb + p), pipeline_mode=pl.Buffered(1)),
            pl.BlockSpec((s, ln), lambda p: (0, vb + p), pipeline_mode=pl.Buffered(1)),
        ],
        out_specs=pl.BlockSpec((s, ln), lambda p: (0, p)),
        out_shape=jax.ShapeDtypeStruct((s, A_HEADS * HEAD_DIM), jnp.bfloat16),
        scratch_shapes=[
            pltpu.VMEM((nblk, 2, MOBA_V_ROWS, MOBA_BLOCK), jnp.bfloat16),
            pltpu.VMEM((nblk, ln), jnp.float32),
            pltpu.VMEM((2, s, ln), jnp.bfloat16),
            pltpu.VMEM((2 * MOBA_TILES, nblk, MOBA_BLOCK), jnp.float32),
            pltpu.VMEM((2, MOBA_UNROLL, 2 * MOBA_TILES, MOBA_GROUPS, SUBLANES_V7X, MOBA_BLOCK), jnp.float32),
            pltpu.VMEM((2, MOBA_UNROLL, 2 * MOBA_TILES, MOBA_BLOCK, MOBA_BLOCK), jnp.bfloat16),
            pltpu.VMEM((2, MOBA_UNROLL, 2 * MOBA_TILES, 2, SUBLANES_V7X, MOBA_BLOCK), jnp.float32),
            pltpu.VMEM((2 * MOBA_TILES, MOBA_V_ROWS // SUBLANES_V7X, SUBLANES_V7X, MOBA_BLOCK), jnp.float32),
        ],
        compiler_params=_params(("arbitrary",)),
        name="moba_attn",
    )(_moba_slopes(), z, z, z)


def _blocks_per_step(nb):
    return max(c for c in (4, 2, 1) if nb % c == 0)


def _band_penalty_table(slope_dils, max_steps):
    steps = np.arange(BAND)[:, None] + BAND - np.arange(2 * BAND)[None, :]
    inside = (steps >= 0) & (steps <= max_steps)
    table = [np.where(inside, np.float32(sd) * steps.astype(np.float32), np.float32(-NEG_BIG)) for sd in slope_dils]
    return jnp.asarray(np.stack(table).astype(np.float32))


def _band_no_prev(first_step):
    kj = lax.broadcasted_iota(jnp.int32, (BAND, 2 * BAND), 1)
    return jnp.logical_and(first_step, kj < BAND)


def _band_units(units, lookahead):
    raws = {j: _dot_nt(units[j][0], units[j][1]) for j in range(min(lookahead, len(units)))}
    outs, lses = [], []
    for j, (_, _, vv, penalty, drop, sink) in enumerate(units):
        ahead = j + lookahead
        if ahead < len(units):
            raws[ahead] = _dot_nt(units[ahead][0], units[ahead][1])
        s = raws.pop(j) - penalty
        if drop is not None:
            s = jnp.where(drop, NEG_BIG, s)
        m = jnp.max(s, axis=1, keepdims=True)
        if sink is not None:
            m = jnp.maximum(m, sink)
        e = jnp.exp(s - m)
        denom = jnp.sum(e, axis=1, keepdims=True)
        if sink is not None:
            denom = denom + jnp.exp(sink - m)
        outs.append(_dot(e.astype(jnp.bfloat16), vv) / denom)
        lses.append(m + jnp.log(denom))
    return outs, lses


def _swa_kernel(sinks_ref, bias_ref, q_ref, kp_ref, ko_ref, vp_ref, vo_ref, o_ref):
    n = pl.program_id(0)
    nbq = q_ref.shape[0] // BAND
    no_prev = _band_no_prev(n == 0)
    k_all = jnp.concatenate([kp_ref[...], ko_ref[...]], axis=0)
    v_all = jnp.concatenate([vp_ref[...], vo_ref[...]], axis=0)
    ln = LANES_V7X
    lane = lax.broadcasted_iota(jnp.int32, (BAND, ln), 1)
    low = lane < HEAD_DIM
    units = []
    for qb in range(nbq):
        kk = k_all[qb * BAND:(qb + 2) * BAND]
        vv = v_all[qb * BAND:(qb + 2) * BAND]
        drop = no_prev if qb == 0 else None
        for b in range(B_HEADS // 2):
            q2 = q_ref[qb * BAND:(qb + 1) * BAND, b * ln:(b + 1) * ln]
            for c in range(2):
                head = B_HEAD_ORDER[2 * b + c]
                qm = jnp.where(low if c == 0 else jnp.logical_not(low), q2, jnp.zeros_like(q2))
                units.append((qm, kk, vv, bias_ref[2 * b + c], drop, sinks_ref[head]))
    outs, _ = _band_units(units, B_HEADS // 2)
    for qb in range(nbq):
        for b in range(B_HEADS // 2):
            j = qb * B_HEADS + 2 * b
            o_ref[qb * BAND:(qb + 1) * BAND, b * ln:(b + 1) * ln] = (
                jnp.where(low, outs[j], outs[j + 1]).astype(o_ref.dtype))


def _swa_call(z, sinks):
    s = z.shape[0]
    nbq = _blocks_per_step(s // BAND)
    rows = nbq * BAND
    ln = LANES_V7X
    qw = B_HEADS * HEAD_DIM
    prev = lambda n: jnp.maximum(n * nbq - 1, 0)
    bias = _band_penalty_table([SLOPES_B[head] for head in B_HEAD_ORDER], B_WINDOW - 1)
    return pl.pallas_call(
        _swa_kernel,
        grid=(s // rows,),
        in_specs=[
            pl.BlockSpec(memory_space=pltpu.SMEM),
            pl.BlockSpec(bias.shape, lambda n: (0, 0, 0), pipeline_mode=pl.Buffered(1)),
            pl.BlockSpec((rows, qw), lambda n: (n, Z_BQ // qw)),
            pl.BlockSpec((BAND, ln), lambda n: (prev(n), Z_BK // ln)),
            pl.BlockSpec((rows, ln), lambda n: (n, Z_BK // ln)),
            pl.BlockSpec((BAND, ln), lambda n: (prev(n), Z_BV // ln)),
            pl.BlockSpec((rows, ln), lambda n: (n, Z_BV // ln)),
        ],
        out_specs=pl.BlockSpec((rows, qw), lambda n: (n, 0)),
        out_shape=jax.ShapeDtypeStruct((s, qw), jnp.bfloat16),
        compiler_params=_params(("arbitrary",)),
        name="swa_attn",
    )(sinks, bias, z, z, z, z, z)


def _dilated_kernel(bias_ref, q_ref, kp_ref, ko_ref, vp_ref, vo_ref, o_ref, lse_ref):
    n = pl.program_id(1)
    nbq = q_ref.shape[0] // BAND
    no_prev = _band_no_prev(n == 0)
    k_all = jnp.concatenate([kp_ref[...], ko_ref[...]], axis=0)
    v_all = jnp.concatenate([vp_ref[...], vo_ref[...]], axis=0)
    cw = q_ref.shape[1]
    lane = lax.broadcasted_iota(jnp.int32, (BAND, cw), 1)
    heads = range(C_HEADS_PER_GROUP)
    mine = [(lane >= h * HEAD_DIM) & (lane < (h + 1) * HEAD_DIM) for h in heads]
    biases = [bias_ref[h] for h in heads]
    units = []
    for qb in range(nbq):
        q4 = q_ref[qb * BAND:(qb + 1) * BAND, :]
        kk = k_all[qb * BAND:(qb + 2) * BAND]
        vv = v_all[qb * BAND:(qb + 2) * BAND]
        drop = no_prev if qb == 0 else None
        for h in heads:
            units.append((jnp.where(mine[h], q4, jnp.zeros_like(q4)), kk, vv, biases[h], drop, None))
    outs, lses = _band_units(units, 4 * C_HEADS_PER_GROUP)
    for qb in range(nbq):
        o_all = jnp.zeros((BAND, cw), jnp.float32)
        lse_all = jnp.zeros((BAND, cw), jnp.float32)
        for h in heads:
            j = qb * C_HEADS_PER_GROUP + h
            o_all = jnp.where(mine[h], outs[j], o_all)
            lse_all = jnp.where(mine[h], lses[j], lse_all)
        o_ref[qb * BAND:(qb + 1) * BAND, :] = o_all
        lse_ref[qb * BAND:(qb + 1) * BAND, :] = lse_all


def _dilated_call(zc, group, col0):
    window, dil = C_GROUPS[group]
    assert zc.shape[0] == dil
    length = zc.shape[1]
    nbq = _blocks_per_step(length // BAND)
    rows = nbq * BAND
    cw = C_HEADS_PER_GROUP * HEAD_DIM
    base = col0 // cw
    slopes = [SLOPES_C[group * C_HEADS_PER_GROUP + h] * dil for h in range(C_HEADS_PER_GROUP)]
    bias = _band_penalty_table(slopes, window // dil)
    own = lambda col: pl.BlockSpec((None, rows, cw), lambda r, n: (r, n, base + col))
    prev = lambda col: pl.BlockSpec((None, BAND, cw), lambda r, n: (r, jnp.maximum(n * nbq - 1, 0), base + col))
    out_blk = pl.BlockSpec((None, rows, cw), lambda r, n: (r, n, 0))
    return pl.pallas_call(
        _dilated_kernel,
        grid=(dil, length // rows),
        in_specs=[pl.BlockSpec(bias.shape, lambda r, n: (0, 0, 0), pipeline_mode=pl.Buffered(1)),
                  own(0), prev(1), own(1), prev(2), own(2)],
        out_specs=[out_blk, out_blk],
        out_shape=[jax.ShapeDtypeStruct((dil, length, cw), jnp.float32)] * 2,
        compiler_params=_params(("arbitrary", "arbitrary")),
        name=f"dilated_attn_g{group}",
    )(bias, zc, zc, zc, zc, zc)


def _merge_kernel(h_ref, zg_ref, oa_ref, ob_ref, o0_ref, l0_ref, o1_ref, l1_ref, o2_ref, l2_ref,
                  wa_ref, wb_ref, wc_ref, wo_ref, g_ref, mod_ref, out_ref, nat_ref):
    tm = h_ref.shape[0]

    def natural(ref, slot):
        dil = ref.shape[0]
        if dil == 1:
            return ref[0]
        ln = LANES_V7X
        chunks = ref.shape[2] // ln
        for r in range(dil):
            for c in range(chunks):
                nat_ref[slot * chunks + c, pl.ds(r, tm // dil, stride=dil), :] = ref[r, :, c * ln:(c + 1) * ln]
        return jnp.concatenate([nat_ref[slot * chunks + c] for c in range(chunks)], axis=1)

    l0, l1, l2 = natural(l0_ref, 0), natural(l1_ref, 0), natural(l2_ref, 1)
    o0, o1, o2 = natural(o0_ref, 0), natural(o1_ref, 2), natural(o2_ref, 3)
    mx = jnp.maximum(jnp.maximum(l0, l1), l2)
    e0, e1, e2 = jnp.exp(l0 - mx), jnp.exp(l1 - mx), jnp.exp(l2 - mx)
    oc = ((e0 * o0 + e1 * o1 + e2 * o2) / (e0 + e1 + e2)).astype(jnp.bfloat16)
    d = D_MODEL
    subs = range(0, tm, MERGE_SUB_ROWS)
    wa, wb, wc, wo = [ref[...].astype(jnp.bfloat16) for ref in (wa_ref, wb_ref, wc_ref, wo_ref)]
    branches = [(_dot(oa_ref[pl.ds(r0, MERGE_SUB_ROWS), :], wa),
                 _dot(ob_ref[pl.ds(r0, MERGE_SUB_ROWS), :], wb),
                 _dot(oc[r0:r0 + MERGE_SUB_ROWS], wc)) for r0 in subs]
    ys = []
    for r0, (ya, yb, yc) in zip(subs, branches):
        gates = jax.nn.sigmoid(zg_ref[pl.ds(r0, MERGE_SUB_ROWS), :].astype(jnp.float32))
        merged = gates[:, 0:d] * ya + gates[:, d:2 * d] * yb + gates[:, 2 * d:3 * d] * yc
        ys.append(_dot(merged.astype(jnp.bfloat16), wo))
    for r0, y in zip(subs, ys):
        rs = pl.ds(r0, MERGE_SUB_ROWS)
        out_ref[rs, :] = h_ref[rs, :] + mod_ref[2:3, :] * _rmsnorm(y, g_ref[...])


MERGE_SUB_ROWS = 256


def _merge_call(h, z, oa, ob, oc_lse, wa, wb, wc, wo, g, mod, layer):
    s, d = h.shape
    tm = 2 * MERGE_SUB_ROWS
    cw = C_HEADS_PER_GROUP * HEAD_DIM
    row = lambda w: pl.BlockSpec((tm, w), lambda i: (i, 0))
    res = lambda a: pl.BlockSpec((a.shape[0], tm // a.shape[0], cw), lambda i: (0, i, 0))
    full = lambda a: pl.BlockSpec(a.shape, lambda i: (0, 0), pipeline_mode=pl.Buffered(1))
    stacked = lambda a: pl.BlockSpec((None,) + a.shape[1:], lambda i: (layer, 0, 0), pipeline_mode=pl.Buffered(1))
    return pl.pallas_call(
        _merge_kernel,
        grid=(s // tm,),
        in_specs=[row(d), row(3 * d), row(oa.shape[1]), row(ob.shape[1])] + [res(a) for a in oc_lse]
        + [stacked(wa), stacked(wb), stacked(wc), stacked(wo), full(g), full(mod)],
        out_specs=row(d),
        out_shape=jax.ShapeDtypeStruct((s, d), jnp.float32),
        scratch_shapes=[pltpu.VMEM((4 * cw // LANES_V7X, tm, LANES_V7X), jnp.float32)],
        input_output_aliases={0: 0} if layer > 0 else {},
        compiler_params=_params(("arbitrary",)),
        name="merge_out_proj",
    )(h, z, oa, ob, *oc_lse, wa, wb, wc, wo, g, mod)


FFN_CHUNK = 1408


def _ffn_kernel(h_ref, g_pre_ref, g_post_ref, mod_ref, wg_ref, wu_ref, wd_ref, out_ref):
    x = h_ref[...]
    u = _rmsnorm(x, g_pre_ref[...]) * (1.0 + mod_ref[4:5, :]) + mod_ref[3:4, :]
    u = u.astype(jnp.bfloat16)
    y = jnp.zeros(x.shape, jnp.float32)
    for c0 in range(0, D_FF, FFN_CHUNK):
        gate = _dot(u, wg_ref[:, c0:c0 + FFN_CHUNK].astype(jnp.bfloat16))
        up = _dot(u, wu_ref[:, c0:c0 + FFN_CHUNK].astype(jnp.bfloat16))
        act = (gate * jax.nn.sigmoid(gate) * up).astype(jnp.bfloat16)
        y = y + _dot(act, wd_ref[c0:c0 + FFN_CHUNK, :].astype(jnp.bfloat16))
    out_ref[...] = x + mod_ref[5:6, :] * _rmsnorm(y, g_post_ref[...])


def _ffn_call(h, g_pre, g_post, mod, wg, wu, wd, layer):
    s, d = h.shape
    tm = ROW_TILE
    row = pl.BlockSpec((tm, d), lambda i: (i, 0))
    full = lambda a: pl.BlockSpec(a.shape, lambda i: (0, 0), pipeline_mode=pl.Buffered(1))
    stacked = lambda a: pl.BlockSpec((None,) + a.shape[1:], lambda i: (layer, 0, 0), pipeline_mode=pl.Buffered(1))
    return pl.pallas_call(
        _ffn_kernel,
        grid=(s // tm,),
        in_specs=[row, full(g_pre), full(g_post), full(mod), stacked(wg), stacked(wu), stacked(wd)],
        out_specs=row,
        out_shape=jax.ShapeDtypeStruct((s, d), jnp.float32),
        input_output_aliases={0: 0},
        compiler_params=_params(("arbitrary",)),
        name="swiglu_ffn",
    )(h, g_pre, g_post, mod, wg, wu, wd)


def _permute_in_proj(w_in):
    hd = HEAD_DIM
    o_bq = 3 * A_HEADS * hd
    o_bkv = o_bq + B_HEADS * hd
    o_c = o_bkv + 2 * B_KV_HEADS * hd
    o_g = o_c + 3 * C_HEADS * hd
    cw = C_HEADS_PER_GROUP * hd
    sl = lambda a, b: w_in[:, :, a:b]
    parts = [sl(o_g, o_g + 3 * D_MODEL), sl(0, A_HEADS * hd) * (QK_SCALE * LOG2E), sl(A_HEADS * hd, o_bq)]
    parts += [sl(o_bq + hh * hd, o_bq + (hh + 1) * hd) * QK_SCALE for hh in B_HEAD_ORDER]
    parts += [sl(o_bkv, o_c)]
    for g in range(len(C_GROUPS)):
        parts += [sl(o_c + g * 3 * cw, o_c + g * 3 * cw + cw) * QK_SCALE,
                  sl(o_c + g * 3 * cw + cw, o_c + (g + 1) * 3 * cw)]
    out = jnp.concatenate([part.astype(jnp.bfloat16) for part in parts], axis=2)
    assert out.shape[2] == IN_WIDTH
    return out


def kernel(x, c, w_ada, b_ada, g_pre_mix, g_post_mix, w_in, sinks, w_br_a, w_br_b, w_br_c,
           w_out, g_pre_ffn, g_post_ffn, w_gate, w_up, w_down):
    bn, s, d = x.shape
    assert bn == 1 and d == D_MODEL and s % SEQ_MULTIPLE == 0
    hd = HEAD_DIM
    w_in_p = _permute_in_proj(w_in)
    w_br_b_p = jnp.concatenate([w_br_b[:, hh * hd:(hh + 1) * hd] for hh in B_HEAD_ORDER], axis=1)

    mod_all = _ada_call(c, w_ada, b_ada).reshape(DEPTH, 6, d)
    h = x.reshape(s, d)
    for l in range(DEPTH):
        mod = mod_all[l]
        z, zc1, zc2 = _in_proj_call(h, g_pre_mix[l].reshape(1, d), mod, w_in_p, l)
        oa = _moba_call(z)
        ob = _swa_call(z, sinks[l])
        oc_lse = (_dilated_call(z.reshape(1, s, Z_WIDTH), 0, Z_C0)
                  + _dilated_call(zc1, 1, 0) + _dilated_call(zc2, 2, 0))
        h = _merge_call(h, z, oa, ob, oc_lse, w_br_a, w_br_b_p, w_br_c, w_out,
                        g_post_mix[l].reshape(1, d), mod, l)
        h = _ffn_call(h, g_pre_ffn[l].reshape(1, d), g_post_ffn[l].reshape(1, d), mod,
                      w_gate, w_up, w_down, l)
    return h.reshape(bn, s, d)
```

```python
import numpy as np
import jax
import jax.numpy as jnp
from jax import lax
from jax.experimental import pallas as pl
from jax.experimental.pallas import tpu as pltpu

D_MODEL = 1024
DEPTH = 4
HEAD_DIM = 64
A_HEADS = 8
MOBA_BLOCK = 256
MOBA_TOPK = 3
B_HEADS = 8
B_KV_HEADS = 2
B_WINDOW = 128
C_GROUPS = ((128, 1), (512, 4), (2048, 16))
C_HEADS_PER_GROUP = 4
C_HEADS = len(C_GROUPS) * C_HEADS_PER_GROUP
BAND = 128
D_FF = 2816
N_ALIBI_HEADS = A_HEADS + B_HEADS + C_HEADS
SEQ_MULTIPLE = 2048
RMS_EPS = 1e-6

LANES_V7X = 128
BF16_SUBLANES_V7X = 16
VMEM_LIMIT_BYTES_V7X = 56 * 1024 * 1024

ROW_TILE = 512
C_WIDTH = 3 * C_HEADS_PER_GROUP * HEAD_DIM

Z_GATES = 0
Z_A = 3 * D_MODEL
Z_BQ = Z_A + 3 * A_HEADS * HEAD_DIM
Z_BK = Z_BQ + B_HEADS * HEAD_DIM
Z_BV = Z_BK + B_KV_HEADS * HEAD_DIM
Z_C0 = Z_BV + B_KV_HEADS * HEAD_DIM
Z_WIDTH = Z_C0 + C_WIDTH
IN_WIDTH = Z_WIDTH + (len(C_GROUPS) - 1) * C_WIDTH
B_HEAD_ORDER = (0, 4, 1, 5, 2, 6, 3, 7)

NEG_BIG = -1e30
QK_SCALE = HEAD_DIM ** -0.5
LOG2E = 1.4426950408889634


def _alibi_slopes():
    n = N_ALIBI_HEADS
    return [float(2.0 ** (-8.0 * (i + 1) / n)) for i in range(n)]


_SLOPES = _alibi_slopes()
SLOPES_B = _SLOPES[:B_HEADS]
SLOPES_C = _SLOPES[B_HEADS:B_HEADS + C_HEADS]
SLOPES_A = _SLOPES[B_HEADS + C_HEADS:]


def _dot(a, b):
    return jnp.dot(a, b, preferred_element_type=jnp.float32)


def _dot_nt(a, b):
    return lax.dot_general(a, b, (((1,), (1,)), ((), ())), preferred_element_type=jnp.float32)


def _params(semantics):
    return pltpu.CompilerParams(dimension_semantics=semantics, vmem_limit_bytes=VMEM_LIMIT_BYTES_V7X)


def _rmsnorm(x, g):
    return x * lax.rsqrt(jnp.mean(x * x, axis=-1, keepdims=True) + RMS_EPS) * g


def _ada_kernel(c_ref, w_ref, b_ref, o_ref):
    c = c_ref[...]
    sc = c * jax.nn.sigmoid(c)
    o_ref[...] = jnp.sum(w_ref[...] * sc, axis=0, keepdims=True) + b_ref[...]


def _ada_call(c, w_ada, b_ada):
    depth, d, n = w_ada.shape
    tn = 1536
    return pl.pallas_call(
        _ada_kernel,
        grid=(depth, n // tn),
        in_specs=[
            pl.BlockSpec((d, 1), lambda l, j: (0, 0)),
            pl.BlockSpec((None, d, tn), lambda l, j: (l, 0, j)),
            pl.BlockSpec((None, 1, tn), lambda l, j: (l, 0, j)),
        ],
        out_specs=pl.BlockSpec((None, 1, tn), lambda l, j: (l, 0, j)),
        out_shape=jax.ShapeDtypeStruct((depth, 1, n), jnp.float32),
        compiler_params=_params(("arbitrary", "arbitrary")),
        name="adaln_mod",
    )(c.reshape(d, 1), w_ada, b_ada.reshape(depth, 1, n))


IN_PROJ_CHUNK = 512


def _in_proj_kernel(h_ref, g_ref, mod_ref, w_ref, z_ref, zc1_ref, zc2_ref, tmp_ref):
    x = h_ref[...]
    u = _rmsnorm(x, g_ref[...]) * (1.0 + mod_ref[1:2, :]) + mod_ref[0:1, :]
    u = u.astype(jnp.bfloat16)
    for c0 in range(0, Z_WIDTH, IN_PROJ_CHUNK):
        z_ref[:, c0:c0 + IN_PROJ_CHUNK] = _dot(u, w_ref[:, c0:c0 + IN_PROJ_CHUNK]).astype(z_ref.dtype)
    for g, out_ref in ((1, zc1_ref), (2, zc2_ref)):
        c0 = Z_WIDTH + (g - 1) * C_WIDTH
        res = _dot(u, w_ref[:, c0:c0 + C_WIDTH])
        dil = C_GROUPS[g][1]
        rows = x.shape[0] // dil
        ln = LANES_V7X
        for c in range(C_WIDTH // ln):
            tmp_ref[c] = res[:, c * ln:(c + 1) * ln]
        for r in range(dil):
            for c in range(C_WIDTH // ln):
                out_ref[r, :, c * ln:(c + 1) * ln] = tmp_ref[c, pl.ds(r, rows, stride=dil), :].astype(out_ref.dtype)


def _in_proj_call(h, g, mod, w, layer):
    s, d = h.shape
    tm = ROW_TILE
    d1, d2 = C_GROUPS[1][1], C_GROUPS[2][1]
    assert tm % (d2 * BF16_SUBLANES_V7X) == 0
    return pl.pallas_call(
        _in_proj_kernel,
        grid=(s // tm,),
        in_specs=[
            pl.BlockSpec((tm, d), lambda i: (i, 0)),
            pl.BlockSpec((1, d), lambda i: (0, 0)),
            pl.BlockSpec((6, d), lambda i: (0, 0)),
            pl.BlockSpec((None, d, IN_WIDTH), lambda i: (layer, 0, 0), pipeline_mode=pl.Buffered(1)),
        ],
        out_specs=[
            pl.BlockSpec((tm, Z_WIDTH), lambda i: (i, 0)),
            pl.BlockSpec((d1, tm // d1, C_WIDTH), lambda i: (0, i, 0)),
            pl.BlockSpec((d2, tm // d2, C_WIDTH), lambda i: (0, i, 0)),
        ],
        out_shape=[
            jax.ShapeDtypeStruct((s, Z_WIDTH), jnp.bfloat16),
            jax.ShapeDtypeStruct((d1, s // d1, C_WIDTH), jnp.bfloat16),
            jax.ShapeDtypeStruct((d2, s // d2, C_WIDTH), jnp.bfloat16),
        ],
        scratch_shapes=[pltpu.VMEM((C_WIDTH // LANES_V7X, tm, LANES_V7X), jnp.float32)],
        compiler_params=_params(("arbitrary",)),
        name="in_proj",
    )(h, g, mod, w)


SUBLANES_V7X = 8
MOBA_V_ROWS = HEAD_DIM + BF16_SUBLANES_V7X
MOBA_GROUPS = MOBA_BLOCK // SUBLANES_V7X
MOBA_UNROLL_LOG2 = 1
MOBA_UNROLL = 1 << MOBA_UNROLL_LOG2
MOBA_TILES = 4


def _all_sublanes_max(x):
    for shift in (4, 2, 1):
        x = jnp.maximum(x, pltpu.roll(x, shift, axis=0))
    return x


def _moba_lanes():
    lane = lax.broadcasted_iota(jnp.int32, (MOBA_BLOCK, LANES_V7X), 1)
    own = [lane < HEAD_DIM, lane >= HEAD_DIM]
    spare = [HEAD_DIM, 0]
    bias_lanes = [(lane == spare[h]) | (lane == spare[h] + 1) for h in (0, 1)]
    return lane, own, spare, bias_lanes


def _moba_kernel(slopes_ref, q_ref, k_ref, v_ref, o_ref, vt_ref, km_ref, ka_ref, sel_ref, s_ref, e_ref, st_ref, acc_ref):
    p = pl.program_id(0)
    nblk = vt_ref.shape[0]
    blk = MOBA_BLOCK
    half = HEAD_DIM
    _, own, _, bias_lanes = _moba_lanes()
    ones = jnp.ones((BF16_SUBLANES_V7X, blk), jnp.bfloat16)

    def prepare(j, carry):
        rows = pl.ds(pl.multiple_of(j * blk, blk), blk)
        vt = v_ref[rows, :].astype(jnp.float32).T.astype(jnp.bfloat16)
        k2 = k_ref[rows, :]
        km_ref[pl.ds(j, 1), :] = jnp.mean(k2.astype(jnp.float32), axis=0, keepdims=True)
        k2f = k2.astype(jnp.float32)
        pos = lax.broadcasted_iota(jnp.int32, k2.shape, 0).astype(jnp.float32)
        for h in (0, 1):
            vt_ref[j, h, 0:half, :] = vt[h * half:(h + 1) * half, :]
            vt_ref[j, h, half:, :] = ones
            ka = jnp.where(own[h], k2f, jnp.where(bias_lanes[h], pos, 0.0))
            ka_ref[h, rows, :] = ka.astype(jnp.bfloat16)
        return carry

    lax.fori_loop(0, nblk, prepare, 0)
    for w in range(MOBA_TILES):
        _moba_select(w, w, q_ref, km_ref, sel_ref)

    def tiles(t, carry):
        _moba_tiles(t, p, slopes_ref, q_ref, o_ref, vt_ref, km_ref, ka_ref, sel_ref, s_ref, e_ref, st_ref, acc_ref)
        return carry

    lax.fori_loop(0, nblk // MOBA_TILES, tiles, 0)


def _moba_select(i, w, q_ref, km_ref, sel_ref):
    nblk, blk = sel_ref.shape[1], MOBA_BLOCK
    _, own, _, _ = _moba_lanes()
    q2 = q_ref[pl.ds(pl.multiple_of(i * blk, blk), blk), :]
    blk_id = lax.broadcasted_iota(jnp.int32, (nblk, blk), 0)
    km = km_ref[...]
    km_hi = km.astype(jnp.bfloat16)
    km_lo = (km - km_hi.astype(jnp.float32)).astype(jnp.bfloat16)
    for h in (0, 1):
        qz = jnp.where(own[h], q2, jnp.zeros_like(q2))
        gate = _dot_nt(km_hi, qz) + _dot_nt(km_lo, qz)
        gate = jnp.where(blk_id < i, gate, -jnp.inf)
        sel = jnp.full((nblk, blk), NEG_BIG, jnp.float32)
        for _ in range(MOBA_TOPK):
            mx = jnp.max(gate, axis=0, keepdims=True)
            cand = (gate == mx) & (mx > -jnp.inf)
            idx = jnp.min(jnp.where(cand, blk_id, nblk), axis=0, keepdims=True)
            chosen = blk_id == idx
            sel = jnp.where(chosen, 0.0, sel)
            gate = jnp.where(chosen, -jnp.inf, gate)
        sel_ref[2 * w + h] = sel


def _moba_tiles(t, p, slopes_ref, q_ref, o_ref, vt_ref, km_ref, ka_ref, sel_ref, s_ref, e_ref, st_ref, acc_ref):
    nblk = vt_ref.shape[0]
    blk = MOBA_BLOCK
    half = HEAD_DIM
    sub = SUBLANES_V7X
    tile3 = (MOBA_GROUPS, sub, blk)
    key_pos = lax.broadcasted_iota(jnp.int32, tile3, 0) * sub + lax.broadcasted_iota(jnp.int32, tile3, 1)
    qry_pos = lax.broadcasted_iota(jnp.int32, tile3, 2)
    lane, own, spare, _ = _moba_lanes()
    streams = range(2 * MOBA_TILES)
    tile_of = [t * MOBA_TILES + st // 2 for st in streams]
    head_of = [st % 2 for st in streams]
    last_tile = tile_of[-1]

    qh = []
    for st in streams:
        h = head_of[st]
        q2f = q_ref[pl.ds(pl.multiple_of(tile_of[st] * blk, blk), blk), :].astype(jnp.float32)
        hi = slopes_ref[A_HEADS + 2 * p + h]
        lo = slopes_ref[2 * A_HEADS + 2 * p + h]
        extra = jnp.where(lane == spare[h], hi, jnp.where(lane == spare[h] + 1, lo, 0.0))
        qh.append(jnp.where(own[h], q2f, extra).astype(jnp.bfloat16))

    unroll = MOBA_UNROLL
    acc_groups = MOBA_V_ROWS // sub

    def past_block(n):
        return jnp.clip(n - 1, 0, nblk - 1)

    def item_block(n, st, own_first):
        return tile_of[st] if own_first else past_block(n)

    def issue_scores(items, own_first):
        out = []
        for u, n in enumerate(items):
            per_stream = []
            for st in streams:
                b = item_block(n, st, own_first and u == 0)
                rows = pl.ds(pl.multiple_of(b * blk, blk), blk)
                per_stream.append(_dot_nt(ka_ref[head_of[st], rows, :], qh[st]).reshape(tile3))
            out.append(per_stream)
        return out

    def stage_scores(raws, slot, items, own_first):
        for u, n in enumerate(items):
            for st in streams:
                s = raws[u][st]
                if own_first and u == 0:
                    s = jnp.where(qry_pos >= key_pos, s, NEG_BIG)
                    row = jnp.zeros((sub, blk), jnp.float32)
                else:
                    b = past_block(n)
                    gap = ((tile_of[st] - b) * blk).astype(jnp.float32)
                    row = sel_ref[st, pl.ds(b, 1), :] - slopes_ref[2 * p + head_of[st]] * gap
                    row = jnp.broadcast_to(jnp.where(n <= tile_of[st], row, NEG_BIG), (sub, blk))
                s_ref[slot, u, st] = s
                st_ref[slot, u, st, 0] = _all_sublanes_max(jnp.max(s, axis=0)) + row
                st_ref[slot, u, st, 1] = row

    def exponentiate(group_slot, ms):
        new_ms, alphas = [], []
        for st in streams:
            m_new = ms[st]
            for u in range(unroll):
                m_new = jnp.maximum(m_new, st_ref[group_slot, u, st, 0])
            alphas.append(jnp.exp2(ms[st] - m_new))
            new_ms.append(m_new)
            for u in range(unroll):
                e = jnp.exp2(s_ref[group_slot, u, st] - (m_new - st_ref[group_slot, u, st, 1])[None])
                e_ref[group_slot, u, st] = e.reshape(blk, blk).astype(jnp.bfloat16)
        return new_ms, alphas

    def accumulate(group, slot, alphas):
        for st in streams:
            acc = alphas[st][None] * acc_ref[st]
            for u in range(unroll):
                n = group * unroll + u
                vblock = jnp.where(n == 0, tile_of[st], past_block(n))
                acc = acc + _dot(vt_ref[vblock, head_of[st]], e_ref[slot, u, st]).reshape(acc_groups, sub, blk)
            acc_ref[st] = acc

    def score_group(group, slot, own_first):
        items = [group * unroll + u for u in range(unroll)]
        raws = issue_scores(items, own_first)
        return lambda: stage_scores(raws, slot, items, own_first)

    def tick(g, slot, ms, alphas):
        items = [g * unroll + u for u in range(unroll)]
        raws = [[None] * len(streams) for _ in items]
        for st in streams:
            for u, n in enumerate(items):
                rows = pl.ds(pl.multiple_of(past_block(n) * blk, blk), blk)
                raws[u][st] = _dot_nt(ka_ref[head_of[st], rows, :], qh[st]).reshape(tile3)
            acc = alphas[st][None] * acc_ref[st]
            for u in range(unroll):
                n = (g - 2) * unroll + u
                vblock = jnp.where(n == 0, tile_of[st], past_block(n))
                acc = acc + _dot(vt_ref[vblock, head_of[st]], e_ref[slot, u, st]).reshape(acc_groups, sub, blk)
            acc_ref[st] = acc
        ms, alphas = exponentiate(1 - slot, ms)
        stage_scores(raws, slot, items, False)
        return ms, alphas

    n_st = len(streams)
    pairs = jnp.maximum(lax.shift_right_logical(last_tile + 2 * unroll, MOBA_UNROLL_LOG2 + 1), 1)
    ms = [jnp.full((sub, blk), -jnp.inf, jnp.float32) for _ in streams]
    acc_ref[...] = jnp.zeros(acc_ref.shape, jnp.float32)
    finish_scores0 = score_group(0, 0, True)
    finish_scores1 = score_group(1, 1, False)
    finish_scores0()
    ms, alphas = exponentiate(0, ms)
    finish_scores1()

    def two_ticks(k, carry):
        ms, alphas = tick(2 * k, 0, list(carry[:n_st]), list(carry[n_st:]))
        ms, alphas = tick(2 * k + 1, 1, ms, alphas)
        return tuple(ms + alphas)

    carry = lax.fori_loop(1, pairs, two_ticks, tuple(ms + alphas))
    ms, alphas = list(carry[:n_st]), list(carry[n_st:])
    accumulate(2 * pairs - 2, 0, alphas)
    ms, alphas = exponentiate(1, ms)
    for w in range(MOBA_TILES):
        _moba_select(jnp.minimum(tile_of[2 * w] + MOBA_TILES, nblk - 1), w, q_ref, km_ref, sel_ref)
    accumulate(2 * pairs - 1, 1, alphas)
    for w in range(MOBA_TILES):
        accs = [acc_ref[2 * w + h] for h in (0, 1)]
        o_t = jnp.concatenate([(acc[0:half // sub] / acc[half // sub][None]).reshape(half, blk) for acc in accs],
                              axis=0)
        o_ref[pl.ds(pl.multiple_of(tile_of[2 * w] * blk, blk), blk), :] = o_t.T.astype(o_ref.dtype)


def _moba_slopes():
    full = np.asarray([sl * LOG2E for sl in SLOPES_A], np.float32)
    hi = full.astype(jnp.bfloat16).astype(np.float32)
    lo = (full - hi).astype(jnp.bfloat16).astype(np.float32)
    return jnp.asarray(np.concatenate([full, hi, lo]))


def _moba_call(z):
    s = z.shape[0]
    nblk = s // MOBA_BLOCK
    pairs = A_HEADS // 2
    ln = LANES_V7X
    qb, kb, vb = Z_A // ln, Z_A // ln + pairs, Z_A // ln + 2 * pairs
    return pl.pallas_call(
        _moba_kernel,
        grid=(pairs,),
        in_specs=[
            pl.BlockSpec(memory_space=pltpu.SMEM),
            pl.BlockSpec((s, ln), lambda p: (0, qb + p), pipeline_mode=pl.Buffered(1)),
            pl.BlockSpec((s, ln), lambda p: (0, kb + p), pipeline_mode=pl.Buffered(1)),
            pl.BlockSpec((s, ln), lambda p: (0, vb + p), pipeline_mode=pl.Buffered(1)),
        ],
        out_specs=pl.BlockSpec((s, ln), lambda p: (0, p)),
        out_shape=jax.ShapeDtypeStruct((s, A_HEADS * HEAD_DIM), jnp.bfloat16),
        scratch_shapes=[
            pltpu.VMEM((nblk, 2, MOBA_V_ROWS, MOBA_BLOCK), jnp.bfloat16),
            pltpu.VMEM((nblk, ln), jnp.float32),
            pltpu.VMEM((2, s, ln), jnp.bfloat16),
            pltpu.VMEM((2 * MOBA_TILES, nblk, MOBA_BLOCK), jnp.float32),
            pltpu.VMEM((2, MOBA_UNROLL, 2 * MOBA_TILES, MOBA_GROUPS, SUBLANES_V7X, MOBA_BLOCK), jnp.float32),
            pltpu.VMEM((2, MOBA_UNROLL, 2 * MOBA_TILES, MOBA_BLOCK, MOBA_BLOCK), jnp.bfloat16),
            pltpu.VMEM((2, MOBA_UNROLL, 2 * MOBA_TILES, 2, SUBLANES_V7X, MOBA_BLOCK), jnp.float32),
            pltpu.VMEM((2 * MOBA_TILES, MOBA_V_ROWS // SUBLANES_V7X, SUBLANES_V7X, MOBA_BLOCK), jnp.float32),
        ],
        compiler_params=_params(("arbitrary",)),
        name="moba_attn",
    )(_moba_slopes(), z, z, z)


def _blocks_per_step(nb):
    return max(c for c in (4, 2, 1) if nb % c == 0)


def _band_penalty_table(slope_dils, max_steps):
    steps = np.arange(BAND)[:, None] + BAND - np.arange(2 * BAND)[None, :]
    inside = (steps >= 0) & (steps <= max_steps)
    table = [np.where(inside, np.float32(sd) * steps.astype(np.float32), np.float32(-NEG_BIG)) for sd in slope_dils]
    return jnp.asarray(np.stack(table).astype(np.float32))


def _band_no_prev(first_step):
    kj = lax.broadcasted_iota(jnp.int32, (BAND, 2 * BAND), 1)
    return jnp.logical_and(first_step, kj < BAND)


def _band_units(units, lookahead):
    raws = {j: _dot_nt(units[j][0], units[j][1]) for j in range(min(lookahead, len(units)))}
    outs, lses = [], []
    for j, (_, _, vv, penalty, drop, sink) in enumerate(units):
        ahead = j + lookahead
        if ahead < len(units):
            raws[ahead] = _dot_nt(units[ahead][0], units[ahead][1])
        s = raws.pop(j) - penalty
        if drop is not None:
            s = jnp.where(drop, NEG_BIG, s)
        m = jnp.max(s, axis=1, keepdims=True)
        if sink is not None:
            m = jnp.maximum(m, sink)
        e = jnp.exp(s - m)
        denom = jnp.sum(e, axis=1, keepdims=True)
        if sink is not None:
            denom = denom + jnp.exp(sink - m)
        outs.append(_dot(e.astype(jnp.bfloat16), vv) / denom)
        lses.append(m + jnp.log(denom))
    return outs, lses


def _swa_kernel(sinks_ref, penalty_ref, q_ref, kp_ref, ko_ref, vp_ref, vo_ref, o_ref):
    n = pl.program_id(0)
    nbq = q_ref.shape[0] // BAND
    no_prev = _band_no_prev(n == 0)
    k_all = jnp.concatenate([kp_ref[...], ko_ref[...]], axis=0)
    v_all = jnp.concatenate([vp_ref[...], vo_ref[...]], axis=0)
    ln = LANES_V7X
    lane = lax.broadcasted_iota(jnp.int32, (BAND, ln), 1)
    low = lane < HEAD_DIM
    units = []
    for qb in range(nbq):
        kk = k_all[qb * BAND:(qb + 2) * BAND]
        vv = v_all[qb * BAND:(qb + 2) * BAND]
        drop = no_prev if qb == 0 else None
        for b in range(B_HEADS // 2):
            q2 = q_ref[qb * BAND:(qb + 1) * BAND, b * ln:(b + 1) * ln]
            for c in range(2):
                head = B_HEAD_ORDER[2 * b + c]
                qm = jnp.where(low if c == 0 else jnp.logical_not(low), q2, jnp.zeros_like(q2))
                units.append((qm, kk, vv, penalty_ref[2 * b + c], drop, sinks_ref[head]))
    outs, _ = _band_units(units, B_HEADS // 2)
    for qb in range(nbq):
        for b in range(B_HEADS // 2):
            j = qb * B_HEADS + 2 * b
            o_ref[qb * BAND:(qb + 1) * BAND, b * ln:(b + 1) * ln] = (
                jnp.where(low, outs[j], outs[j + 1]).astype(o_ref.dtype))


def _swa_call(z, sinks):
    s = z.shape[0]
    nbq = _blocks_per_step(s // BAND)
    rows = nbq * BAND
    ln = LANES_V7X
    qw = B_HEADS * HEAD_DIM
    prev = lambda n: jnp.maximum(n * nbq - 1, 0)
    penalty = _band_penalty_table([SLOPES_B[head] for head in B_HEAD_ORDER], B_WINDOW - 1)
    return pl.pallas_call(
        _swa_kernel,
        grid=(s // rows,),
        in_specs=[
            pl.BlockSpec(memory_space=pltpu.SMEM),
            pl.BlockSpec(penalty.shape,lambda n: (0, 0, 0), pipeline_mode=pl.Buffered(1)),
            pl.BlockSpec((rows, qw), lambda n: (n, Z_BQ // qw)),
            pl.BlockSpec((BAND, ln), lambda n: (prev(n), Z_BK // ln)),
            pl.BlockSpec((rows, ln), lambda n: (n, Z_BK // ln)),
            pl.BlockSpec((BAND, ln), lambda n: (prev(n), Z_BV // ln)),
            pl.BlockSpec((rows, ln), lambda n: (n, Z_BV // ln)),
        ],
        out_specs=pl.BlockSpec((rows, qw), lambda n: (n, 0)),
        out_shape=jax.ShapeDtypeStruct((s, qw), jnp.bfloat16),
        compiler_params=_params(("arbitrary",)),
        name="swa_attn",
    )(sinks, penalty, z, z, z, z, z)


def _dilated_kernel(penalty_ref, q_ref, kp_ref, ko_ref, vp_ref, vo_ref, o_ref, lse_ref):
    n = pl.program_id(1)
    nbq = q_ref.shape[0] // BAND
    no_prev = _band_no_prev(n == 0)
    k_all = jnp.concatenate([kp_ref[...], ko_ref[...]], axis=0)
    v_all = jnp.concatenate([vp_ref[...], vo_ref[...]], axis=0)
    cw = q_ref.shape[1]
    lane = lax.broadcasted_iota(jnp.int32, (BAND, cw), 1)
    heads = range(C_HEADS_PER_GROUP)
    mine = [(lane >= h * HEAD_DIM) & (lane < (h + 1) * HEAD_DIM) for h in heads]
    penalties = [penalty_ref[h] for h in heads]
    units = []
    for qb in range(nbq):
        q4 = q_ref[qb * BAND:(qb + 1) * BAND, :]
        kk = k_all[qb * BAND:(qb + 2) * BAND]
        vv = v_all[qb * BAND:(qb + 2) * BAND]
        drop = no_prev if qb == 0 else None
        for h in heads:
            units.append((jnp.where(mine[h], q4, jnp.zeros_like(q4)), kk, vv, penalties[h], drop, None))
    outs, lses = _band_units(units, 4 * C_HEADS_PER_GROUP)
    for qb in range(nbq):
        o_all = jnp.zeros((BAND, cw), jnp.float32)
        lse_all = jnp.zeros((BAND, cw), jnp.float32)
        for h in heads:
            j = qb * C_HEADS_PER_GROUP + h
            o_all = jnp.where(mine[h], outs[j], o_all)
            lse_all = jnp.where(mine[h], lses[j], lse_all)
        o_ref[qb * BAND:(qb + 1) * BAND, :] = o_all
        lse_ref[qb * BAND:(qb + 1) * BAND, :] = lse_all


def _dilated_call(zc, group, col0):
    window, dil = C_GROUPS[group]
    assert zc.shape[0] == dil
    length = zc.shape[1]
    nbq = _blocks_per_step(length // BAND)
    rows = nbq * BAND
    cw = C_HEADS_PER_GROUP * HEAD_DIM
    base = col0 // cw
    slopes = [SLOPES_C[group * C_HEADS_PER_GROUP + h] * dil for h in range(C_HEADS_PER_GROUP)]
    penalty = _band_penalty_table(slopes, window // dil)
    own = lambda col: pl.BlockSpec((None, rows, cw), lambda r, n: (r, n, base + col))
    prev = lambda col: pl.BlockSpec((None, BAND, cw), lambda r, n: (r, jnp.maximum(n * nbq - 1, 0), base + col))
    out_blk = pl.BlockSpec((None, rows, cw), lambda r, n: (r, n, 0))
    return pl.pallas_call(
        _dilated_kernel,
        grid=(dil, length // rows),
        in_specs=[pl.BlockSpec(penalty.shape,lambda r, n: (0, 0, 0), pipeline_mode=pl.Buffered(1)),
                  own(0), prev(1), own(1), prev(2), own(2)],
        out_specs=[out_blk, out_blk],
        out_shape=[jax.ShapeDtypeStruct((dil, length, cw), jnp.float32)] * 2,
        compiler_params=_params(("arbitrary", "arbitrary")),
        name=f"dilated_attn_g{group}",
    )(penalty, zc, zc, zc, zc, zc)


def _merge_kernel(h_ref, zg_ref, oa_ref, ob_ref, o0_ref, l0_ref, o1_ref, l1_ref, o2_ref, l2_ref,
                  wa_ref, wb_ref, wc_ref, wo_ref, g_ref, mod_ref, out_ref, nat_ref):
    tm = h_ref.shape[0]

    def natural(ref, slot):
        dil = ref.shape[0]
        if dil == 1:
            return ref[0]
        ln = LANES_V7X
        chunks = ref.shape[2] // ln
        for r in range(dil):
            for c in range(chunks):
                nat_ref[slot * chunks + c, pl.ds(r, tm // dil, stride=dil), :] = ref[r, :, c * ln:(c + 1) * ln]
        return jnp.concatenate([nat_ref[slot * chunks + c] for c in range(chunks)], axis=1)

    l0, l1, l2 = natural(l0_ref, 0), natural(l1_ref, 0), natural(l2_ref, 1)
    o0, o1, o2 = natural(o0_ref, 0), natural(o1_ref, 2), natural(o2_ref, 3)
    mx = jnp.maximum(jnp.maximum(l0, l1), l2)
    e0, e1, e2 = jnp.exp(l0 - mx), jnp.exp(l1 - mx), jnp.exp(l2 - mx)
    oc = ((e0 * o0 + e1 * o1 + e2 * o2) / (e0 + e1 + e2)).astype(jnp.bfloat16)
    d = D_MODEL
    subs = range(0, tm, MERGE_SUB_ROWS)
    wa, wb, wc, wo = [ref[...].astype(jnp.bfloat16) for ref in (wa_ref, wb_ref, wc_ref, wo_ref)]
    branches = [(_dot(oa_ref[pl.ds(r0, MERGE_SUB_ROWS), :], wa),
                 _dot(ob_ref[pl.ds(r0, MERGE_SUB_ROWS), :], wb),
                 _dot(oc[r0:r0 + MERGE_SUB_ROWS], wc)) for r0 in subs]
    ys = []
    for r0, (ya, yb, yc) in zip(subs, branches):
        gates = jax.nn.sigmoid(zg_ref[pl.ds(r0, MERGE_SUB_ROWS), :].astype(jnp.float32))
        merged = gates[:, 0:d] * ya + gates[:, d:2 * d] * yb + gates[:, 2 * d:3 * d] * yc
        ys.append(_dot(merged.astype(jnp.bfloat16), wo))
    for r0, y in zip(subs, ys):
        rs = pl.ds(r0, MERGE_SUB_ROWS)
        out_ref[rs, :] = h_ref[rs, :] + mod_ref[2:3, :] * _rmsnorm(y, g_ref[...])


MERGE_SUB_ROWS = 256


def _merge_call(h, z, oa, ob, oc_lse, wa, wb, wc, wo, g, mod, layer):
    s, d = h.shape
    tm = 2 * MERGE_SUB_ROWS
    cw = C_HEADS_PER_GROUP * HEAD_DIM
    row = lambda w: pl.BlockSpec((tm, w), lambda i: (i, 0))
    res = lambda a: pl.BlockSpec((a.shape[0], tm // a.shape[0], cw), lambda i: (0, i, 0))
    full = lambda a: pl.BlockSpec(a.shape, lambda i: (0, 0), pipeline_mode=pl.Buffered(1))
    stacked = lambda a: pl.BlockSpec((None,) + a.shape[1:], lambda i: (layer, 0, 0), pipeline_mode=pl.Buffered(1))
    return pl.pallas_call(
        _merge_kernel,
        grid=(s // tm,),
        in_specs=[row(d), row(3 * d), row(oa.shape[1]), row(ob.shape[1])] + [res(a) for a in oc_lse]
        + [stacked(wa), stacked(wb), stacked(wc), stacked(wo), full(g), full(mod)],
        out_specs=row(d),
        out_shape=jax.ShapeDtypeStruct((s, d), jnp.float32),
        scratch_shapes=[pltpu.VMEM((4 * cw // LANES_V7X, tm, LANES_V7X), jnp.float32)],
        input_output_aliases={0: 0} if layer > 0 else {},
        compiler_params=_params(("arbitrary",)),
        name="merge_out_proj",
    )(h, z, oa, ob, *oc_lse, wa, wb, wc, wo, g, mod)


FFN_CHUNK = 1408


def _ffn_kernel(h_ref, g_pre_ref, g_post_ref, mod_ref, wg_ref, wu_ref, wd_ref, out_ref):
    x = h_ref[...]
    u = _rmsnorm(x, g_pre_ref[...]) * (1.0 + mod_ref[4:5, :]) + mod_ref[3:4, :]
    u = u.astype(jnp.bfloat16)
    y = jnp.zeros(x.shape, jnp.float32)
    for c0 in range(0, D_FF, FFN_CHUNK):
        gate = _dot(u, wg_ref[:, c0:c0 + FFN_CHUNK].astype(jnp.bfloat16))
        up = _dot(u, wu_ref[:, c0:c0 + FFN_CHUNK].astype(jnp.bfloat16))
        act = (gate * jax.nn.sigmoid(gate) * up).astype(jnp.bfloat16)
        y = y + _dot(act, wd_ref[c0:c0 + FFN_CHUNK, :].astype(jnp.bfloat16))
    out_ref[...] = x + mod_ref[5:6, :] * _rmsnorm(y, g_post_ref[...])


def _ffn_call(h, g_pre, g_post, mod, wg, wu, wd, layer):
    s, d = h.shape
    tm = ROW_TILE
    row = pl.BlockSpec((tm, d), lambda i: (i, 0))
    full = lambda a: pl.BlockSpec(a.shape, lambda i: (0, 0), pipeline_mode=pl.Buffered(1))
    stacked = lambda a: pl.BlockSpec((None,) + a.shape[1:], lambda i: (layer, 0, 0), pipeline_mode=pl.Buffered(1))
    return pl.pallas_call(
        _ffn_kernel,
        grid=(s // tm,),
        in_specs=[row, full(g_pre), full(g_post), full(mod), stacked(wg), stacked(wu), stacked(wd)],
        out_specs=row,
        out_shape=jax.ShapeDtypeStruct((s, d), jnp.float32),
        input_output_aliases={0: 0},
        compiler_params=_params(("arbitrary",)),
        name="swiglu_ffn",
    )(h, g_pre, g_post, mod, wg, wu, wd)


def _permute_in_proj(w_in):
    hd = HEAD_DIM
    o_bq = 3 * A_HEADS * hd
    o_bkv = o_bq + B_HEADS * hd
    o_c = o_bkv + 2 * B_KV_HEADS * hd
    o_g = o_c + 3 * C_HEADS * hd
    cw = C_HEADS_PER_GROUP * hd
    sl = lambda a, b: w_in[:, :, a:b]
    parts = [sl(o_g, o_g + 3 * D_MODEL), sl(0, A_HEADS * hd) * (QK_SCALE * LOG2E), sl(A_HEADS * hd, o_bq)]
    parts += [sl(o_bq + hh * hd, o_bq + (hh + 1) * hd) * QK_SCALE for hh in B_HEAD_ORDER]
    parts += [sl(o_bkv, o_c)]
    for g in range(len(C_GROUPS)):
        parts += [sl(o_c + g * 3 * cw, o_c + g * 3 * cw + cw) * QK_SCALE,
                  sl(o_c + g * 3 * cw + cw, o_c + (g + 1) * 3 * cw)]
    out = jnp.concatenate([part.astype(jnp.bfloat16) for part in parts], axis=2)
    assert out.shape[2] == IN_WIDTH
    return out


def kernel(x, c, w_ada, b_ada, g_pre_mix, g_post_mix, w_in, sinks, w_br_a, w_br_b, w_br_c,
           w_out, g_pre_ffn, g_post_ffn, w_gate, w_up, w_down):
    bn, s, d = x.shape
    assert bn == 1 and d == D_MODEL and s % SEQ_MULTIPLE == 0
    hd = HEAD_DIM
    w_in_p = _permute_in_proj(w_in)
    w_br_b_p = jnp.concatenate([w_br_b[:, hh * hd:(hh + 1) * hd] for hh in B_HEAD_ORDER], axis=1)

    mod_all = _ada_call(c, w_ada, b_ada).reshape(DEPTH, 6, d)
    h = x.reshape(s, d)
    for l in range(DEPTH):
        mod = mod_all[l]
        z, zc1, zc2 = _in_proj_call(h, g_pre_mix[l].reshape(1, d), mod, w_in_p, l)
        oa = _moba_call(z)
        ob = _swa_call(z, sinks[l])
        oc_lse = (_dilated_call(z.reshape(1, s, Z_WIDTH), 0, Z_C0)
                  + _dilated_call(zc1, 1, 0) + _dilated_call(zc2, 2, 0))
        h = _merge_call(h, z, oa, ob, oc_lse, w_br_a, w_br_b_p, w_br_c, w_out,
                        g_post_mix[l].reshape(1, d), mod, l)
        h = _ffn_call(h, g_pre_ffn[l].reshape(1, d), g_post_ffn[l].reshape(1, d), mod,
                      w_gate, w_up, w_down, l)
    return h.reshape(bn, s, d)
```

```python
import numpy as np
import jax
import jax.numpy as jnp
from jax import lax
from jax.experimental import pallas as pl
from jax.experimental.pallas import tpu as pltpu

D_MODEL = 1024
DEPTH = 4
HEAD_DIM = 64
A_HEADS = 8
MOBA_BLOCK = 256
MOBA_TOPK = 3
B_HEADS = 8
B_KV_HEADS = 2
B_WINDOW = 128
C_GROUPS = ((128, 1), (512, 4), (2048, 16))
C_HEADS_PER_GROUP = 4
C_HEADS = len(C_GROUPS) * C_HEADS_PER_GROUP
BAND = 128
D_FF = 2816
N_ALIBI_HEADS = A_HEADS + B_HEADS + C_HEADS
SEQ_MULTIPLE = 2048
RMS_EPS = 1e-6

LANES_V7X = 128
BF16_SUBLANES_V7X = 16
VMEM_LIMIT_BYTES_V7X = 56 * 1024 * 1024

ROW_TILE = 512
C_WIDTH = 3 * C_HEADS_PER_GROUP * HEAD_DIM

Z_GATES = 0
Z_A = 3 * D_MODEL
Z_BQ = Z_A + 3 * A_HEADS * HEAD_DIM
Z_BK = Z_BQ + B_HEADS * HEAD_DIM
Z_BV = Z_BK + B_KV_HEADS * HEAD_DIM
Z_C0 = Z_BV + B_KV_HEADS * HEAD_DIM
Z_WIDTH = Z_C0 + C_WIDTH
IN_WIDTH = Z_WIDTH + (len(C_GROUPS) - 1) * C_WIDTH
B_HEAD_ORDER = (0, 4, 1, 5, 2, 6, 3, 7)

NEG_BIG = -1e30
QK_SCALE = HEAD_DIM ** -0.5
LOG2E = 1.4426950408889634


def _alibi_slopes():
    n = N_ALIBI_HEADS
    return [float(2.0 ** (-8.0 * (i + 1) / n)) for i in range(n)]


_SLOPES = _alibi_slopes()
SLOPES_B = _SLOPES[:B_HEADS]
SLOPES_C = _SLOPES[B_HEADS:B_HEADS + C_HEADS]
SLOPES_A = _SLOPES[B_HEADS + C_HEADS:]


def _dot(a, b):
    return jnp.dot(a, b, preferred_element_type=jnp.float32)


def _dot_nt(a, b):
    return lax.dot_general(a, b, (((1,), (1,)), ((), ())), preferred_element_type=jnp.float32)


def _params(semantics):
    return pltpu.CompilerParams(dimension_semantics=semantics, vmem_limit_bytes=VMEM_LIMIT_BYTES_V7X)


def _rmsnorm(x, g):
    return x * lax.rsqrt(jnp.mean(x * x, axis=-1, keepdims=True) + RMS_EPS) * g


def _ada_kernel(c_ref, w_ref, b_ref, o_ref):
    c = c_ref[...]
    sc = c * jax.nn.sigmoid(c)
    o_ref[...] = jnp.sum(w_ref[...] * sc, axis=0, keepdims=True) + b_ref[...]


def _ada_call(c, w_ada, b_ada):
    depth, d, n = w_ada.shape
    tn = 1536
    return pl.pallas_call(
        _ada_kernel,
        grid=(depth, n // tn),
        in_specs=[
            pl.BlockSpec((d, 1), lambda l, j: (0, 0)),
            pl.BlockSpec((None, d, tn), lambda l, j: (l, 0, j)),
            pl.BlockSpec((None, 1, tn), lambda l, j: (l, 0, j)),
        ],
        out_specs=pl.BlockSpec((None, 1, tn), lambda l, j: (l, 0, j)),
        out_shape=jax.ShapeDtypeStruct((depth, 1, n), jnp.float32),
        compiler_params=_params(("arbitrary", "arbitrary")),
        name="adaln_mod",
    )(c.reshape(d, 1), w_ada, b_ada.reshape(depth, 1, n))


IN_PROJ_CHUNK = 512


def _in_proj_kernel(h_ref, g_ref, mod_ref, w_ref, z_ref, zc1_ref, zc2_ref, tmp_ref):
    x = h_ref[...]
    u = _rmsnorm(x, g_ref[...]) * (1.0 + mod_ref[1:2, :]) + mod_ref[0:1, :]
    u = u.astype(jnp.bfloat16)
    for c0 in range(0, Z_WIDTH, IN_PROJ_CHUNK):
        z_ref[:, c0:c0 + IN_PROJ_CHUNK] = _dot(u, w_ref[:, c0:c0 + IN_PROJ_CHUNK]).astype(z_ref.dtype)
    for g, out_ref in ((1, zc1_ref), (2, zc2_ref)):
        c0 = Z_WIDTH + (g - 1) * C_WIDTH
        res = _dot(u, w_ref[:, c0:c0 + C_WIDTH])
        dil = C_GROUPS[g][1]
        rows = x.shape[0] // dil
        ln = LANES_V7X
        for c in range(C_WIDTH // ln):
            tmp_ref[c] = res[:, c * ln:(c + 1) * ln]
        for r in range(dil):
            for c in range(C_WIDTH // ln):
                out_ref[r, :, c * ln:(c + 1) * ln] = tmp_ref[c, pl.ds(r, rows, stride=dil), :].astype(out_ref.dtype)


def _in_proj_call(h, g, mod, w, layer):
    s, d = h.shape
    tm = ROW_TILE
    d1, d2 = C_GROUPS[1][1], C_GROUPS[2][1]
    assert tm % (d2 * BF16_SUBLANES_V7X) == 0
    return pl.pallas_call(
        _in_proj_kernel,
        grid=(s // tm,),
        in_specs=[
            pl.BlockSpec((tm, d), lambda i: (i, 0)),
            pl.BlockSpec((1, d), lambda i: (0, 0)),
            pl.BlockSpec((6, d), lambda i: (0, 0)),
            pl.BlockSpec((None, d, IN_WIDTH), lambda i: (layer, 0, 0), pipeline_mode=pl.Buffered(1)),
        ],
        out_specs=[
            pl.BlockSpec((tm, Z_WIDTH), lambda i: (i, 0)),
            pl.BlockSpec((d1, tm // d1, C_WIDTH), lambda i: (0, i, 0)),
            pl.BlockSpec((d2, tm // d2, C_WIDTH), lambda i: (0, i, 0)),
        ],
        out_shape=[
            jax.ShapeDtypeStruct((s, Z_WIDTH), jnp.bfloat16),
            jax.ShapeDtypeStruct((d1, s // d1, C_WIDTH), jnp.bfloat16),
            jax.ShapeDtypeStruct((d2, s // d2, C_WIDTH), jnp.bfloat16),
        ],
        scratch_shapes=[pltpu.VMEM((C_WIDTH // LANES_V7X, tm, LANES_V7X), jnp.float32)],
        compiler_params=_params(("arbitrary",)),
        name="in_proj",
    )(h, g, mod, w)


SUBLANES_V7X = 8
MOBA_V_ROWS = HEAD_DIM + BF16_SUBLANES_V7X
MOBA_GROUPS = MOBA_BLOCK // SUBLANES_V7X
MOBA_UNROLL_LOG2 = 0
MOBA_UNROLL = 1 << MOBA_UNROLL_LOG2
MOBA_TILES = 8


def _all_sublanes_max(x):
    for shift in (4, 2, 1):
        x = jnp.maximum(x, pltpu.roll(x, shift, axis=0))
    return x


def _moba_lanes():
    lane = lax.broadcasted_iota(jnp.int32, (MOBA_BLOCK, LANES_V7X), 1)
    own = [lane < HEAD_DIM, lane >= HEAD_DIM]
    spare = [HEAD_DIM, 0]
    bias_lanes = [(lane == spare[h]) | (lane == spare[h] + 1) for h in (0, 1)]
    return lane, own, spare, bias_lanes


def _moba_kernel(slopes_ref, q_ref, k_ref, v_ref, o_ref, vt_ref, km_ref, ka_ref, sel_ref, s_ref, e_ref, st_ref, acc_ref):
    p = pl.program_id(0)
    nblk = vt_ref.shape[0]
    blk = MOBA_BLOCK
    half = HEAD_DIM
    _, own, _, bias_lanes = _moba_lanes()
    ones = jnp.ones((BF16_SUBLANES_V7X, blk), jnp.bfloat16)

    def prepare(j, carry):
        rows = pl.ds(pl.multiple_of(j * blk, blk), blk)
        vt = v_ref[rows, :].astype(jnp.float32).T.astype(jnp.bfloat16)
        k2 = k_ref[rows, :]
        km_ref[pl.ds(j, 1), :] = jnp.mean(k2.astype(jnp.float32), axis=0, keepdims=True)
        k2f = k2.astype(jnp.float32)
        pos = lax.broadcasted_iota(jnp.int32, k2.shape, 0).astype(jnp.float32)
        for h in (0, 1):
            vt_ref[j, h, 0:half, :] = vt[h * half:(h + 1) * half, :]
            vt_ref[j, h, half:, :] = ones
            ka = jnp.where(own[h], k2f, jnp.where(bias_lanes[h], pos, 0.0))
            ka_ref[h, rows, :] = ka.astype(jnp.bfloat16)
        return carry

    lax.fori_loop(0, nblk, prepare, 0)
    for w in range(MOBA_TILES):
        _moba_select(w, w, q_ref, km_ref, sel_ref)

    def tiles(t, carry):
        _moba_tiles(t, p, slopes_ref, q_ref, o_ref, vt_ref, km_ref, ka_ref, sel_ref, s_ref, e_ref, st_ref, acc_ref)
        return carry

    lax.fori_loop(0, nblk // MOBA_TILES, tiles, 0)


def _moba_select(i, w, q_ref, km_ref, sel_ref):
    nblk, blk = sel_ref.shape[1], MOBA_BLOCK
    _, own, _, _ = _moba_lanes()
    q2 = q_ref[pl.ds(pl.multiple_of(i * blk, blk), blk), :]
    blk_id = lax.broadcasted_iota(jnp.int32, (nblk, blk), 0)
    km = km_ref[...]
    km_hi = km.astype(jnp.bfloat16)
    km_lo = (km - km_hi.astype(jnp.float32)).astype(jnp.bfloat16)
    for h in (0, 1):
        qz = jnp.where(own[h], q2, jnp.zeros_like(q2))
        gate = _dot_nt(km_hi, qz) + _dot_nt(km_lo, qz)
        gate = jnp.where(blk_id < i, gate, -jnp.inf)
        sel = jnp.full((nblk, blk), NEG_BIG, jnp.float32)
        for _ in range(MOBA_TOPK):
            mx = jnp.max(gate, axis=0, keepdims=True)
            cand = (gate == mx) & (mx > -jnp.inf)
            idx = jnp.min(jnp.where(cand, blk_id, nblk), axis=0, keepdims=True)
            chosen = blk_id == idx
            sel = jnp.where(chosen, 0.0, sel)
            gate = jnp.where(chosen, -jnp.inf, gate)
        sel_ref[2 * w + h] = sel


def _moba_tiles(t, p, slopes_ref, q_ref, o_ref, vt_ref, km_ref, ka_ref, sel_ref, s_ref, e_ref, st_ref, acc_ref):
    nblk = vt_ref.shape[0]
    blk = MOBA_BLOCK
    half = HEAD_DIM
    sub = SUBLANES_V7X
    tile3 = (MOBA_GROUPS, sub, blk)
    key_pos = lax.broadcasted_iota(jnp.int32, tile3, 0) * sub + lax.broadcasted_iota(jnp.int32, tile3, 1)
    qry_pos = lax.broadcasted_iota(jnp.int32, tile3, 2)
    lane, own, spare, _ = _moba_lanes()
    streams = range(2 * MOBA_TILES)
    tile_of = [t * MOBA_TILES + st // 2 for st in streams]
    head_of = [st % 2 for st in streams]
    last_tile = tile_of[-1]

    qh = []
    for st in streams:
        h = head_of[st]
        q2f = q_ref[pl.ds(pl.multiple_of(tile_of[st] * blk, blk), blk), :].astype(jnp.float32)
        hi = slopes_ref[A_HEADS + 2 * p + h]
        lo = slopes_ref[2 * A_HEADS + 2 * p + h]
        extra = jnp.where(lane == spare[h], hi, jnp.where(lane == spare[h] + 1, lo, 0.0))
        qh.append(jnp.where(own[h], q2f, extra).astype(jnp.bfloat16))

    unroll = MOBA_UNROLL
    acc_groups = MOBA_V_ROWS // sub

    def past_block(n):
        return jnp.clip(n - 1, 0, nblk - 1)

    def item_block(n, st, own_first):
        return tile_of[st] if own_first else past_block(n)

    def issue_scores(items, own_first):
        out = []
        for u, n in enumerate(items):
            per_stream = []
            for st in streams:
                b = item_block(n, st, own_first and u == 0)
                rows = pl.ds(pl.multiple_of(b * blk, blk), blk)
                per_stream.append(_dot_nt(ka_ref[head_of[st], rows, :], qh[st]).reshape(tile3))
            out.append(per_stream)
        return out

    def stage_scores(raws, slot, items, own_first):
        for u, n in enumerate(items):
            for st in streams:
                s = raws[u][st]
                if own_first and u == 0:
                    s = jnp.where(qry_pos >= key_pos, s, NEG_BIG)
                    row = jnp.zeros((sub, blk), jnp.float32)
                else:
                    b = past_block(n)
                    gap = ((tile_of[st] - b) * blk).astype(jnp.float32)
                    row = sel_ref[st, pl.ds(b, 1), :] - slopes_ref[2 * p + head_of[st]] * gap
                    row = jnp.broadcast_to(jnp.where(n <= tile_of[st], row, NEG_BIG), (sub, blk))
                s_ref[slot, u, st] = s
                st_ref[slot, u, st, 0] = _all_sublanes_max(jnp.max(s, axis=0)) + row
                st_ref[slot, u, st, 1] = row

    def exponentiate(group_slot, ms):
        new_ms, alphas = [], []
        for st in streams:
            m_new = ms[st]
            for u in range(unroll):
                m_new = jnp.maximum(m_new, st_ref[group_slot, u, st, 0])
            alphas.append(jnp.exp2(ms[st] - m_new))
            new_ms.append(m_new)
            for u in range(unroll):
                e = jnp.exp2(s_ref[group_slot, u, st] - (m_new - st_ref[group_slot, u, st, 1])[None])
                e_ref[group_slot, u, st] = e.reshape(blk, blk).astype(jnp.bfloat16)
        return new_ms, alphas

    def accumulate(group, slot, alphas):
        for st in streams:
            acc = alphas[st][None] * acc_ref[st]
            for u in range(unroll):
                n = group * unroll + u
                vblock = jnp.where(n == 0, tile_of[st], past_block(n))
                acc = acc + _dot(vt_ref[vblock, head_of[st]], e_ref[slot, u, st]).reshape(acc_groups, sub, blk)
            acc_ref[st] = acc

    def score_group(group, slot, own_first):
        items = [group * unroll + u for u in range(unroll)]
        raws = issue_scores(items, own_first)
        return lambda: stage_scores(raws, slot, items, own_first)

    def tick(g, slot, ms, alphas):
        items = [g * unroll + u for u in range(unroll)]
        raws = [[None] * len(streams) for _ in items]
        for st in streams:
            for u, n in enumerate(items):
                rows = pl.ds(pl.multiple_of(past_block(n) * blk, blk), blk)
                raws[u][st] = _dot_nt(ka_ref[head_of[st], rows, :], qh[st]).reshape(tile3)
            acc = alphas[st][None] * acc_ref[st]
            for u in range(unroll):
                n = (g - 2) * unroll + u
                vblock = jnp.where(n == 0, tile_of[st], past_block(n))
                acc = acc + _dot(vt_ref[vblock, head_of[st]], e_ref[slot, u, st]).reshape(acc_groups, sub, blk)
            acc_ref[st] = acc
        ms, alphas = exponentiate(1 - slot, ms)
        stage_scores(raws, slot, items, False)
        return ms, alphas

    n_st = len(streams)
    pairs = jnp.maximum(lax.shift_right_logical(last_tile + 2 * unroll, MOBA_UNROLL_LOG2 + 1), 1)
    ms = [jnp.full((sub, blk), -jnp.inf, jnp.float32) for _ in streams]
    acc_ref[...] = jnp.zeros(acc_ref.shape, jnp.float32)
    finish_scores0 = score_group(0, 0, True)
    finish_scores1 = score_group(1, 1, False)
    finish_scores0()
    ms, alphas = exponentiate(0, ms)
    finish_scores1()

    def two_ticks(k, carry):
        ms, alphas = tick(2 * k, 0, list(carry[:n_st]), list(carry[n_st:]))
        ms, alphas = tick(2 * k + 1, 1, ms, alphas)
        return tuple(ms + alphas)

    carry = lax.fori_loop(1, pairs, two_ticks, tuple(ms + alphas))
    ms, alphas = list(carry[:n_st]), list(carry[n_st:])
    accumulate(2 * pairs - 2, 0, alphas)
    ms, alphas = exponentiate(1, ms)
    for w in range(MOBA_TILES):
        _moba_select(jnp.minimum(tile_of[2 * w] + MOBA_TILES, nblk - 1), w, q_ref, km_ref, sel_ref)
    accumulate(2 * pairs - 1, 1, alphas)
    for w in range(MOBA_TILES):
        accs = [acc_ref[2 * w + h] for h in (0, 1)]
        o_t = jnp.concatenate([(acc[0:half // sub] / acc[half // sub][None]).reshape(half, blk) for acc in accs],
                              axis=0)
        o_ref[pl.ds(pl.multiple_of(tile_of[2 * w] * blk, blk), blk), :] = o_t.T.astype(o_ref.dtype)


def _moba_slopes():
    full = np.asarray([sl * LOG2E for sl in SLOPES_A], np.float32)
    hi = full.astype(jnp.bfloat16).astype(np.float32)
    lo = (full - hi).astype(jnp.bfloat16).astype(np.float32)
    return jnp.asarray(np.concatenate([full, hi, lo]))


def _moba_call(z):
    s = z.shape[0]
    nblk = s // MOBA_BLOCK
    pairs = A_HEADS // 2
    ln = LANES_V7X
    qb, kb, vb = Z_A // ln, Z_A // ln + pairs, Z_A // ln + 2 * pairs
    return pl.pallas_call(
        _moba_kernel,
        grid=(pairs,),
        in_specs=[
            pl.BlockSpec(memory_space=pltpu.SMEM),
            pl.BlockSpec((s, ln), lambda p: (0, qb + p), pipeline_mode=pl.Buffered(1)),
            pl.BlockSpec((s, ln), lambda p: (0, kb + p), pipeline_mode=pl.Buffered(1)),
            pl.BlockSpec((s, ln), lambda p: (0, vb + p), pipeline_mode=pl.Buffered(1)),
        ],
        out_specs=pl.BlockSpec((s, ln), lambda p: (0, p)),
        out_shape=jax.ShapeDtypeStruct((s, A_HEADS * HEAD_DIM), jnp.bfloat16),
        scratch_shapes=[
            pltpu.VMEM((nblk, 2, MOBA_V_ROWS, MOBA_BLOCK), jnp.bfloat16),
            pltpu.VMEM((nblk, ln), jnp.float32),
            pltpu.VMEM((2, s, ln), jnp.bfloat16),
            pltpu.VMEM((2 * MOBA_TILES, nblk, MOBA_BLOCK), jnp.float32),
            pltpu.VMEM((2, MOBA_UNROLL, 2 * MOBA_TILES, MOBA_GROUPS, SUBLANES_V7X, MOBA_BLOCK), jnp.float32),
            pltpu.VMEM((2, MOBA_UNROLL, 2 * MOBA_TILES, MOBA_BLOCK, MOBA_BLOCK), jnp.bfloat16),
            pltpu.VMEM((2, MOBA_UNROLL, 2 * MOBA_TILES, 2, SUBLANES_V7X, MOBA_BLOCK), jnp.float32),
            pltpu.VMEM((2 * MOBA_TILES, MOBA_V_ROWS // SUBLANES_V7X, SUBLANES_V7X, MOBA_BLOCK), jnp.float32),
        ],
        compiler_params=_params(("arbitrary",)),
        name="moba_attn",
    )(_moba_slopes(), z, z, z)


def _blocks_per_step(nb):
    return max(c for c in (4, 2, 1) if nb % c == 0)


def _band_penalty_table(slope_dils, max_steps):
    steps = np.arange(BAND)[:, None] + BAND - np.arange(2 * BAND)[None, :]
    inside = (steps >= 0) & (steps <= max_steps)
    table = [np.where(inside, np.float32(sd) * steps.astype(np.float32), np.float32(-NEG_BIG)) for sd in slope_dils]
    return jnp.asarray(np.stack(table).astype(np.float32))


def _band_no_prev(first_step):
    kj = lax.broadcasted_iota(jnp.int32, (BAND, 2 * BAND), 1)
    return jnp.logical_and(first_step, kj < BAND)


def _band_units(units, lookahead):
    raws = {j: _dot_nt(units[j][0], units[j][1]) for j in range(min(lookahead, len(units)))}
    outs, lses = [], []
    for j, (_, _, vv, penalty, drop, sink) in enumerate(units):
        ahead = j + lookahead
        if ahead < len(units):
            raws[ahead] = _dot_nt(units[ahead][0], units[ahead][1])
        s = raws.pop(j) - penalty
        if drop is not None:
            s = jnp.where(drop, NEG_BIG, s)
        m = jnp.max(s, axis=1, keepdims=True)
        if sink is not None:
            m = jnp.maximum(m, sink)
        e = jnp.exp(s - m)
        denom = jnp.sum(e, axis=1, keepdims=True)
        if sink is not None:
            denom = denom + jnp.exp(sink - m)
        outs.append(_dot(e.astype(jnp.bfloat16), vv) / denom)
        lses.append(m + jnp.log(denom))
    return outs, lses


def _swa_kernel(sinks_ref, penalty_ref, q_ref, kp_ref, ko_ref, vp_ref, vo_ref, o_ref):
    n = pl.program_id(0)
    nbq = q_ref.shape[0] // BAND
    no_prev = _band_no_prev(n == 0)
    k_all = jnp.concatenate([kp_ref[...], ko_ref[...]], axis=0)
    v_all = jnp.concatenate([vp_ref[...], vo_ref[...]], axis=0)
    ln = LANES_V7X
    lane = lax.broadcasted_iota(jnp.int32, (BAND, ln), 1)
    low = lane < HEAD_DIM
    units = []
    for qb in range(nbq):
        kk = k_all[qb * BAND:(qb + 2) * BAND]
        vv = v_all[qb * BAND:(qb + 2) * BAND]
        drop = no_prev if qb == 0 else None
        for b in range(B_HEADS // 2):
            q2 = q_ref[qb * BAND:(qb + 1) * BAND, b * ln:(b + 1) * ln]
            for c in range(2):
                head = B_HEAD_ORDER[2 * b + c]
                qm = jnp.where(low if c == 0 else jnp.logical_not(low), q2, jnp.zeros_like(q2))
                units.append((qm, kk, vv, penalty_ref[2 * b + c], drop, sinks_ref[head]))
    outs, _ = _band_units(units, B_HEADS // 2)
    for qb in range(nbq):
        for b in range(B_HEADS // 2):
            j = qb * B_HEADS + 2 * b
            o_ref[qb * BAND:(qb + 1) * BAND, b * ln:(b + 1) * ln] = (
                jnp.where(low, outs[j], outs[j + 1]).astype(o_ref.dtype))


def _swa_call(z, sinks):
    s = z.shape[0]
    nbq = _blocks_per_step(s // BAND)
    rows = nbq * BAND
    ln = LANES_V7X
    qw = B_HEADS * HEAD_DIM
    prev = lambda n: jnp.maximum(n * nbq - 1, 0)
    penalty = _band_penalty_table([SLOPES_B[head] for head in B_HEAD_ORDER], B_WINDOW - 1)
    return pl.pallas_call(
        _swa_kernel,
        grid=(s // rows,),
        in_specs=[
            pl.BlockSpec(memory_space=pltpu.SMEM),
            pl.BlockSpec(penalty.shape,lambda n: (0, 0, 0), pipeline_mode=pl.Buffered(1)),
            pl.BlockSpec((rows, qw), lambda n: (n, Z_BQ // qw)),
            pl.BlockSpec((BAND, ln), lambda n: (prev(n), Z_BK // ln)),
            pl.BlockSpec((rows, ln), lambda n: (n, Z_BK // ln)),
            pl.BlockSpec((BAND, ln), lambda n: (prev(n), Z_BV // ln)),
            pl.BlockSpec((rows, ln), lambda n: (n, Z_BV // ln)),
        ],
        out_specs=pl.BlockSpec((rows, qw), lambda n: (n, 0)),
        out_shape=jax.ShapeDtypeStruct((s, qw), jnp.bfloat16),
        compiler_params=_params(("arbitrary",)),
        name="swa_attn",
    )(sinks, penalty, z, z, z, z, z)


def _dilated_kernel(penalty_ref, q_ref, kp_ref, ko_ref, vp_ref, vo_ref, o_ref, lse_ref):
    n = pl.program_id(1)
    nbq = q_ref.shape[0] // BAND
    no_prev = _band_no_prev(n == 0)
    k_all = jnp.concatenate([kp_ref[...], ko_ref[...]], axis=0)
    v_all = jnp.concatenate([vp_ref[...], vo_ref[...]], axis=0)
    cw = q_ref.shape[1]
    lane = lax.broadcasted_iota(jnp.int32, (BAND, cw), 1)
    heads = range(C_HEADS_PER_GROUP)
    mine = [(lane >= h * HEAD_DIM) & (lane < (h + 1) * HEAD_DIM) for h in heads]
    penalties = [penalty_ref[h] for h in heads]
    units = []
    for qb in range(nbq):
        q4 = q_ref[qb * BAND:(qb + 1) * BAND, :]
        kk = k_all[qb * BAND:(qb + 2) * BAND]
        vv = v_all[qb * BAND:(qb + 2) * BAND]
        drop = no_prev if qb == 0 else None
        for h in heads:
            units.append((jnp.where(mine[h], q4, jnp.zeros_like(q4)), kk, vv, penalties[h], drop, None))
    outs, lses = _band_units(units, 4 * C_HEADS_PER_GROUP)
    for qb in range(nbq):
        o_all = jnp.zeros((BAND, cw), jnp.float32)
        lse_all = jnp.zeros((BAND, cw), jnp.float32)
        for h in heads:
            j = qb * C_HEADS_PER_GROUP + h
            o_all = jnp.where(mine[h], outs[j], o_all)
            lse_all = jnp.where(mine[h], lses[j], lse_all)
        o_ref[qb * BAND:(qb + 1) * BAND, :] = o_all
        lse_ref[qb * BAND:(qb + 1) * BAND, :] = lse_all


def _dilated_call(zc, group, col0):
    window, dil = C_GROUPS[group]
    assert zc.shape[0] == dil
    length = zc.shape[1]
    nbq = _blocks_per_step(length // BAND)
    rows = nbq * BAND
    cw = C_HEADS_PER_GROUP * HEAD_DIM
    base = col0 // cw
    slopes = [SLOPES_C[group * C_HEADS_PER_GROUP + h] * dil for h in range(C_HEADS_PER_GROUP)]
    penalty = _band_penalty_table(slopes, window // dil)
    own = lambda col: pl.BlockSpec((None, rows, cw), lambda r, n: (r, n, base + col))
    prev = lambda col: pl.BlockSpec((None, BAND, cw), lambda r, n: (r, jnp.maximum(n * nbq - 1, 0), base + col))
    out_blk = pl.BlockSpec((None, rows, cw), lambda r, n: (r, n, 0))
    return pl.pallas_call(
        _dilated_kernel,
        grid=(dil, length // rows),
        in_specs=[pl.BlockSpec(penalty.shape,lambda r, n: (0, 0, 0), pipeline_mode=pl.Buffered(1)),
                  own(0), prev(1), own(1), prev(2), own(2)],
        out_specs=[out_blk, out_blk],
        out_shape=[jax.ShapeDtypeStruct((dil, length, cw), jnp.float32)] * 2,
        compiler_params=_params(("arbitrary", "arbitrary")),
        name=f"dilated_attn_g{group}",
    )(penalty, zc, zc, zc, zc, zc)


def _merge_kernel(h_ref, zg_ref, oa_ref, ob_ref, o0_ref, l0_ref, o1_ref, l1_ref, o2_ref, l2_ref,
                  wa_ref, wb_ref, wc_ref, wo_ref, g_ref, mod_ref, out_ref, nat_ref):
    tm = h_ref.shape[0]

    def natural(ref, slot):
        dil = ref.shape[0]
        if dil == 1:
            return ref[0]
        ln = LANES_V7X
        chunks = ref.shape[2] // ln
        for r in range(dil):
            for c in range(chunks):
                nat_ref[slot * chunks + c, pl.ds(r, tm // dil, stride=dil), :] = ref[r, :, c * ln:(c + 1) * ln]
        return jnp.concatenate([nat_ref[slot * chunks + c] for c in range(chunks)], axis=1)

    l0, l1, l2 = natural(l0_ref, 0), natural(l1_ref, 0), natural(l2_ref, 1)
    o0, o1, o2 = natural(o0_ref, 0), natural(o1_ref, 2), natural(o2_ref, 3)
    mx = jnp.maximum(jnp.maximum(l0, l1), l2)
    e0, e1, e2 = jnp.exp(l0 - mx), jnp.exp(l1 - mx), jnp.exp(l2 - mx)
    oc = ((e0 * o0 + e1 * o1 + e2 * o2) / (e0 + e1 + e2)).astype(jnp.bfloat16)
    d = D_MODEL
    subs = range(0, tm, MERGE_SUB_ROWS)
    wa, wb, wc, wo = [ref[...].astype(jnp.bfloat16) for ref in (wa_ref, wb_ref, wc_ref, wo_ref)]
    branches = [(_dot(oa_ref[pl.ds(r0, MERGE_SUB_ROWS), :], wa),
                 _dot(ob_ref[pl.ds(r0, MERGE_SUB_ROWS), :], wb),
                 _dot(oc[r0:r0 + MERGE_SUB_ROWS], wc)) for r0 in subs]
    ys = []
    for r0, (ya, yb, yc) in zip(subs, branches):
        gates = jax.nn.sigmoid(zg_ref[pl.ds(r0, MERGE_SUB_ROWS), :].astype(jnp.float32))
        merged = gates[:, 0:d] * ya + gates[:, d:2 * d] * yb + gates[:, 2 * d:3 * d] * yc
        ys.append(_dot(merged.astype(jnp.bfloat16), wo))
    for r0, y in zip(subs, ys):
        rs = pl.ds(r0, MERGE_SUB_ROWS)
        out_ref[rs, :] = h_ref[rs, :] + mod_ref[2:3, :] * _rmsnorm(y, g_ref[...])


MERGE_SUB_ROWS = 256


def _merge_call(h, z, oa, ob, oc_lse, wa, wb, wc, wo, g, mod, layer):
    s, d = h.shape
    tm = 2 * MERGE_SUB_ROWS
    cw = C_HEADS_PER_GROUP * HEAD_DIM
    row = lambda w: pl.BlockSpec((tm, w), lambda i: (i, 0))
    res = lambda a: pl.BlockSpec((a.shape[0], tm // a.shape[0], cw), lambda i: (0, i, 0))
    full = lambda a: pl.BlockSpec(a.shape, lambda i: (0, 0), pipeline_mode=pl.Buffered(1))
    stacked = lambda a: pl.BlockSpec((None,) + a.shape[1:], lambda i: (layer, 0, 0), pipeline_mode=pl.Buffered(1))
    return pl.pallas_call(
        _merge_kernel,
        grid=(s // tm,),
        in_specs=[row(d), row(3 * d), row(oa.shape[1]), row(ob.shape[1])] + [res(a) for a in oc_lse]
        + [stacked(wa), stacked(wb), stacked(wc), stacked(wo), full(g), full(mod)],
        out_specs=row(d),
        out_shape=jax.ShapeDtypeStruct((s, d), jnp.float32),
        scratch_shapes=[pltpu.VMEM((4 * cw // LANES_V7X, tm, LANES_V7X), jnp.float32)],
        input_output_aliases={0: 0} if layer > 0 else {},
        compiler_params=_params(("arbitrary",)),
        name="merge_out_proj",
    )(h, z, oa, ob, *oc_lse, wa, wb, wc, wo, g, mod)


FFN_CHUNK = 1408


def _ffn_kernel(h_ref, g_pre_ref, g_post_ref, mod_ref, wg_ref, wu_ref, wd_ref, out_ref):
    x = h_ref[...]
    u = _rmsnorm(x, g_pre_ref[...]) * (1.0 + mod_ref[4:5, :]) + mod_ref[3:4, :]
    u = u.astype(jnp.bfloat16)
    y = jnp.zeros(x.shape, jnp.float32)
    for c0 in range(0, D_FF, FFN_CHUNK):
        gate = _dot(u, wg_ref[:, c0:c0 + FFN_CHUNK].astype(jnp.bfloat16))
        up = _dot(u, wu_ref[:, c0:c0 + FFN_CHUNK].astype(jnp.bfloat16))
        act = (gate * jax.nn.sigmoid(gate) * up).astype(jnp.bfloat16)
        y = y + _dot(act, wd_ref[c0:c0 + FFN_CHUNK, :].astype(jnp.bfloat16))
    out_ref[...] = x + mod_ref[5:6, :] * _rmsnorm(y, g_post_ref[...])


def _ffn_call(h, g_pre, g_post, mod, wg, wu, wd, layer):
    s, d = h.shape
    tm = ROW_TILE
    row = pl.BlockSpec((tm, d), lambda i: (i, 0))
    full = lambda a: pl.BlockSpec(a.shape, lambda i: (0, 0), pipeline_mode=pl.Buffered(1))
    stacked = lambda a: pl.BlockSpec((None,) + a.shape[1:], lambda i: (layer, 0, 0), pipeline_mode=pl.Buffered(1))
    return pl.pallas_call(
        _ffn_kernel,
        grid=(s // tm,),
        in_specs=[row, full(g_pre), full(g_post), full(mod), stacked(wg), stacked(wu), stacked(wd)],
        out_specs=row,
        out_shape=jax.ShapeDtypeStruct((s, d), jnp.float32),
        input_output_aliases={0: 0},
        compiler_params=_params(("arbitrary",)),
        name="swiglu_ffn",
    )(h, g_pre, g_post, mod, wg, wu, wd)


def _permute_in_proj(w_in):
    hd = HEAD_DIM
    o_bq = 3 * A_HEADS * hd
    o_bkv = o_bq + B_HEADS * hd
    o_c = o_bkv + 2 * B_KV_HEADS * hd
    o_g = o_c + 3 * C_HEADS * hd
    cw = C_HEADS_PER_GROUP * hd
    sl = lambda a, b: w_in[:, :, a:b]
    parts = [sl(o_g, o_g + 3 * D_MODEL), sl(0, A_HEADS * hd) * (QK_SCALE * LOG2E), sl(A_HEADS * hd, o_bq)]
    parts += [sl(o_bq + hh * hd, o_bq + (hh + 1) * hd) * QK_SCALE for hh in B_HEAD_ORDER]
    parts += [sl(o_bkv, o_c)]
    for g in range(len(C_GROUPS)):
        parts += [sl(o_c + g * 3 * cw, o_c + g * 3 * cw + cw) * QK_SCALE,
                  sl(o_c + g * 3 * cw + cw, o_c + (g + 1) * 3 * cw)]
    out = jnp.concatenate([part.astype(jnp.bfloat16) for part in parts], axis=2)
    assert out.shape[2] == IN_WIDTH
    return out


def kernel(x, c, w_ada, b_ada, g_pre_mix, g_post_mix, w_in, sinks, w_br_a, w_br_b, w_br_c,
           w_out, g_pre_ffn, g_post_ffn, w_gate, w_up, w_down):
    bn, s, d = x.shape
    assert bn == 1 and d == D_MODEL and s % SEQ_MULTIPLE == 0
    hd = HEAD_DIM
    w_in_p = _permute_in_proj(w_in)
    w_br_b_p = jnp.concatenate([w_br_b[:, hh * hd:(hh + 1) * hd] for hh in B_HEAD_ORDER], axis=1)

    mod_all = _ada_call(c, w_ada, b_ada).reshape(DEPTH, 6, d)
    h = x.reshape(s, d)
    for l in range(DEPTH):
        mod = mod_all[l]
        z, zc1, zc2 = _in_proj_call(h, g_pre_mix[l].reshape(1, d), mod, w_in_p, l)
        oa = _moba_call(z)
        ob = _swa_call(z, sinks[l])
        oc_lse = (_dilated_call(z.reshape(1, s, Z_WIDTH), 0, Z_C0)
                  + _dilated_call(zc1, 1, 0) + _dilated_call(zc2, 2, 0))
        h = _merge_call(h, z, oa, ob, oc_lse, w_br_a, w_br_b_p, w_br_c, w_out,
                        g_post_mix[l].reshape(1, d), mod, l)
        h = _ffn_call(h, g_pre_ffn[l].reshape(1, d), g_post_ffn[l].reshape(1, d), mod,
                      w_gate, w_up, w_down, l)
    return h.reshape(bn, s, d)
```

```python
import numpy as np
import jax
import jax.numpy as jnp
from jax import lax
from jax.experimental import pallas as pl
from jax.experimental.pallas import tpu as pltpu

D_MODEL = 1024
DEPTH = 4
HEAD_DIM = 64
A_HEADS = 8
MOBA_BLOCK = 256
MOBA_TOPK = 3
B_HEADS = 8
B_KV_HEADS = 2
B_WINDOW = 128
C_GROUPS = ((128, 1), (512, 4), (2048, 16))
C_HEADS_PER_GROUP = 4
C_HEADS = len(C_GROUPS) * C_HEADS_PER_GROUP
BAND = 128
D_FF = 2816
N_ALIBI_HEADS = A_HEADS + B_HEADS + C_HEADS
SEQ_MULTIPLE = 2048
RMS_EPS = 1e-6

LANES_V7X = 128
BF16_SUBLANES_V7X = 16
VMEM_LIMIT_BYTES_V7X = 56 * 1024 * 1024

ROW_TILE = 512
C_WIDTH = 3 * C_HEADS_PER_GROUP * HEAD_DIM

Z_GATES = 0
Z_A = 3 * D_MODEL
Z_BQ = Z_A + 3 * A_HEADS * HEAD_DIM
Z_BK = Z_BQ + B_HEADS * HEAD_DIM
Z_BV = Z_BK + B_KV_HEADS * HEAD_DIM
Z_C0 = Z_BV + B_KV_HEADS * HEAD_DIM
Z_WIDTH = Z_C0 + C_WIDTH
IN_WIDTH = Z_WIDTH + (len(C_GROUPS) - 1) * C_WIDTH
B_HEAD_ORDER = (0, 4, 1, 5, 2, 6, 3, 7)

NEG_BIG = -1e30
QK_SCALE = HEAD_DIM ** -0.5
LOG2E = 1.4426950408889634


def _alibi_slopes():
    n = N_ALIBI_HEADS
    return [float(2.0 ** (-8.0 * (i + 1) / n)) for i in range(n)]


_SLOPES = _alibi_slopes()
SLOPES_B = _SLOPES[:B_HEADS]
SLOPES_C = _SLOPES[B_HEADS:B_HEADS + C_HEADS]
SLOPES_A = _SLOPES[B_HEADS + C_HEADS:]


def _dot(a, b):
    return jnp.dot(a, b, preferred_element_type=jnp.float32)


def _dot_nt(a, b):
    return lax.dot_general(a, b, (((1,), (1,)), ((), ())), preferred_element_type=jnp.float32)


def _params(semantics):
    return pltpu.CompilerParams(dimension_semantics=semantics, vmem_limit_bytes=VMEM_LIMIT_BYTES_V7X)


def _rmsnorm(x, g):
    return x * lax.rsqrt(jnp.mean(x * x, axis=-1, keepdims=True) + RMS_EPS) * g


def _ada_kernel(c_ref, w_ref, b_ref, o_ref):
    c = c_ref[...]
    sc = c * jax.nn.sigmoid(c)
    o_ref[...] = jnp.sum(w_ref[...] * sc, axis=0, keepdims=True) + b_ref[...]


def _ada_call(c, w_ada, b_ada):
    depth, d, n = w_ada.shape
    tn = 1536
    return pl.pallas_call(
        _ada_kernel,
        grid=(depth, n // tn),
        in_specs=[
            pl.BlockSpec((d, 1), lambda l, j: (0, 0)),
            pl.BlockSpec((None, d, tn), lambda l, j: (l, 0, j)),
            pl.BlockSpec((None, 1, tn), lambda l, j: (l, 0, j)),
        ],
        out_specs=pl.BlockSpec((None, 1, tn), lambda l, j: (l, 0, j)),
        out_shape=jax.ShapeDtypeStruct((depth, 1, n), jnp.float32),
        compiler_params=_params(("arbitrary", "arbitrary")),
        name="adaln_mod",
    )(c.reshape(d, 1), w_ada, b_ada.reshape(depth, 1, n))


IN_PROJ_CHUNK = 512


def _in_proj_kernel(h_ref, g_ref, mod_ref, w_ref, z_ref, zc1_ref, zc2_ref, tmp_ref):
    x = h_ref[...]
    u = _rmsnorm(x, g_ref[...]) * (1.0 + mod_ref[1:2, :]) + mod_ref[0:1, :]
    u = u.astype(jnp.bfloat16)
    for c0 in range(0, Z_WIDTH, IN_PROJ_CHUNK):
        z_ref[:, c0:c0 + IN_PROJ_CHUNK] = _dot(u, w_ref[:, c0:c0 + IN_PROJ_CHUNK]).astype(z_ref.dtype)
    for g, out_ref in ((1, zc1_ref), (2, zc2_ref)):
        c0 = Z_WIDTH + (g - 1) * C_WIDTH
        res = _dot(u, w_ref[:, c0:c0 + C_WIDTH])
        dil = C_GROUPS[g][1]
        rows = x.shape[0] // dil
        ln = LANES_V7X
        for c in range(C_WIDTH // ln):
            tmp_ref[c] = res[:, c * ln:(c + 1) * ln]
        for r in range(dil):
            for c in range(C_WIDTH // ln):
                out_ref[r, :, c * ln:(c + 1) * ln] = tmp_ref[c, pl.ds(r, rows, stride=dil), :].astype(out_ref.dtype)


def _in_proj_call(h, g, mod, w, layer):
    s, d = h.shape
    tm = ROW_TILE
    d1, d2 = C_GROUPS[1][1], C_GROUPS[2][1]
    assert tm % (d2 * BF16_SUBLANES_V7X) == 0
    return pl.pallas_call(
        _in_proj_kernel,
        grid=(s // tm,),
        in_specs=[
            pl.BlockSpec((tm, d), lambda i: (i, 0)),
            pl.BlockSpec((1, d), lambda i: (0, 0)),
            pl.BlockSpec((6, d), lambda i: (0, 0)),
            pl.BlockSpec((None, d, IN_WIDTH), lambda i: (layer, 0, 0), pipeline_mode=pl.Buffered(1)),
        ],
        out_specs=[
            pl.BlockSpec((tm, Z_WIDTH), lambda i: (i, 0)),
            pl.BlockSpec((d1, tm // d1, C_WIDTH), lambda i: (0, i, 0)),
            pl.BlockSpec((d2, tm // d2, C_WIDTH), lambda i: (0, i, 0)),
        ],
        out_shape=[
            jax.ShapeDtypeStruct((s, Z_WIDTH), jnp.bfloat16),
            jax.ShapeDtypeStruct((d1, s // d1, C_WIDTH), jnp.bfloat16),
            jax.ShapeDtypeStruct((d2, s // d2, C_WIDTH), jnp.bfloat16),
        ],
        scratch_shapes=[pltpu.VMEM((C_WIDTH // LANES_V7X, tm, LANES_V7X), jnp.float32)],
        compiler_params=_params(("arbitrary",)),
        name="in_proj",
    )(h, g, mod, w)


SUBLANES_V7X = 8
MOBA_V_ROWS = HEAD_DIM + BF16_SUBLANES_V7X
MOBA_GROUPS = MOBA_BLOCK // SUBLANES_V7X
MOBA_UNROLL_LOG2 = 1
MOBA_UNROLL = 1 << MOBA_UNROLL_LOG2
MOBA_TILES = 4


def _all_sublanes_max(x):
    for shift in (4, 2, 1):
        x = jnp.maximum(x, pltpu.roll(x, shift, axis=0))
    return x


def _moba_lanes():
    lane = lax.broadcasted_iota(jnp.int32, (MOBA_BLOCK, LANES_V7X), 1)
    own = [lane < HEAD_DIM, lane >= HEAD_DIM]
    spare = [HEAD_DIM, 0]
    bias_lanes = [(lane == spare[h]) | (lane == spare[h] + 1) for h in (0, 1)]
    return lane, own, spare, bias_lanes


def _moba_kernel(slopes_ref, q_ref, k_ref, v_ref, o_ref, vt_ref, km_ref, ka_ref, sel_ref, s_ref, e_ref, st_ref, acc_ref):
    p = pl.program_id(0)
    nblk = vt_ref.shape[0]
    blk = MOBA_BLOCK
    half = HEAD_DIM
    _, own, _, bias_lanes = _moba_lanes()
    ones = jnp.ones((BF16_SUBLANES_V7X, blk), jnp.bfloat16)

    def prepare(j, carry):
        rows = pl.ds(pl.multiple_of(j * blk, blk), blk)
        vt = v_ref[rows, :].astype(jnp.float32).T.astype(jnp.bfloat16)
        k2 = k_ref[rows, :]
        km_ref[pl.ds(j, 1), :] = jnp.mean(k2.astype(jnp.float32), axis=0, keepdims=True)
        k2f = k2.astype(jnp.float32)
        pos = lax.broadcasted_iota(jnp.int32, k2.shape, 0).astype(jnp.float32)
        for h in (0, 1):
            vt_ref[j, h, 0:half, :] = vt[h * half:(h + 1) * half, :]
            vt_ref[j, h, half:, :] = ones
            ka = jnp.where(own[h], k2f, jnp.where(bias_lanes[h], pos, 0.0))
            ka_ref[h, rows, :] = ka.astype(jnp.bfloat16)
        return carry

    lax.fori_loop(0, nblk, prepare, 0)
    for w in range(MOBA_TILES):
        _moba_select(w, w, q_ref, km_ref, sel_ref)

    def tiles(t, carry):
        _moba_tiles(t, p, slopes_ref, q_ref, o_ref, vt_ref, km_ref, ka_ref, sel_ref, s_ref, e_ref, st_ref, acc_ref)
        return carry

    lax.fori_loop(0, nblk // MOBA_TILES, tiles, 0)


def _moba_select(i, w, q_ref, km_ref, sel_ref):
    nblk, blk = sel_ref.shape[1], MOBA_BLOCK
    _, own, _, _ = _moba_lanes()
    q2 = q_ref[pl.ds(pl.multiple_of(i * blk, blk), blk), :]
    blk_id = lax.broadcasted_iota(jnp.int32, (nblk, blk), 0)
    km = km_ref[...]
    km_hi = km.astype(jnp.bfloat16)
    km_lo = (km - km_hi.astype(jnp.float32)).astype(jnp.bfloat16)
    for h in (0, 1):
        qz = jnp.where(own[h], q2, jnp.zeros_like(q2))
        gate = _dot_nt(km_hi, qz) + _dot_nt(km_lo, qz)
        gate = jnp.where(blk_id < i, gate, -jnp.inf)
        sel = jnp.full((nblk, blk), NEG_BIG, jnp.float32)
        for _ in range(MOBA_TOPK):
            mx = jnp.max(gate, axis=0, keepdims=True)
            cand = (gate == mx) & (mx > -jnp.inf)
            idx = jnp.min(jnp.where(cand, blk_id, nblk), axis=0, keepdims=True)
            chosen = blk_id == idx
            sel = jnp.where(chosen, 0.0, sel)
            gate = jnp.where(chosen, -jnp.inf, gate)
        sel_ref[2 * w + h] = sel


def _moba_tiles(t, p, slopes_ref, q_ref, o_ref, vt_ref, km_ref, ka_ref, sel_ref, s_ref, e_ref, st_ref, acc_ref):
    nblk = vt_ref.shape[0]
    blk = MOBA_BLOCK
    half = HEAD_DIM
    sub = SUBLANES_V7X
    tile3 = (MOBA_GROUPS, sub, blk)
    key_pos = lax.broadcasted_iota(jnp.int32, tile3, 0) * sub + lax.broadcasted_iota(jnp.int32, tile3, 1)
    qry_pos = lax.broadcasted_iota(jnp.int32, tile3, 2)
    lane, own, spare, _ = _moba_lanes()
    streams = range(2 * MOBA_TILES)
    tile_of = [t * MOBA_TILES + st // 2 for st in streams]
    head_of = [st % 2 for st in streams]
    last_tile = tile_of[-1]

    qh = []
    for st in streams:
        h = head_of[st]
        q2f = q_ref[pl.ds(pl.multiple_of(tile_of[st] * blk, blk), blk), :].astype(jnp.float32)
        hi = slopes_ref[A_HEADS + 2 * p + h]
        lo = slopes_ref[2 * A_HEADS + 2 * p + h]
        extra = jnp.where(lane == spare[h], hi, jnp.where(lane == spare[h] + 1, lo, 0.0))
        qh.append(jnp.where(own[h], q2f, extra).astype(jnp.bfloat16))

    unroll = MOBA_UNROLL
    acc_groups = MOBA_V_ROWS // sub

    def past_block(n):
        return jnp.clip(n - 1, 0, nblk - 1)

    def item_block(n, st, own_first):
        return tile_of[st] if own_first else past_block(n)

    def issue_scores(items, own_first):
        out = []
        for u, n in enumerate(items):
            per_stream = []
            for st in streams:
                b = item_block(n, st, own_first and u == 0)
                rows = pl.ds(pl.multiple_of(b * blk, blk), blk)
                per_stream.append(_dot_nt(ka_ref[head_of[st], rows, :], qh[st]).reshape(tile3))
            out.append(per_stream)
        return out

    def stage_scores(raws, slot, items, own_first):
        for u, n in enumerate(items):
            for st in streams:
                s = raws[u][st]
                if own_first and u == 0:
                    s = jnp.where(qry_pos >= key_pos, s, NEG_BIG)
                    row = jnp.zeros((sub, blk), jnp.float32)
                else:
                    b = past_block(n)
                    gap = ((tile_of[st] - b) * blk).astype(jnp.float32)
                    row = sel_ref[st, pl.ds(b, 1), :] - slopes_ref[2 * p + head_of[st]] * gap
                    row = jnp.broadcast_to(jnp.where(n <= tile_of[st], row, NEG_BIG), (sub, blk))
                s_ref[slot, u, st] = s
                st_ref[slot, u, st, 0] = _all_sublanes_max(jnp.max(s, axis=0)) + row
                st_ref[slot, u, st, 1] = row

    def exponentiate(group_slot, ms):
        new_ms, alphas = [], []
        for st in streams:
            m_new = ms[st]
            for u in range(unroll):
                m_new = jnp.maximum(m_new, st_ref[group_slot, u, st, 0])
            alphas.append(jnp.exp2(ms[st] - m_new))
            new_ms.append(m_new)
            for u in range(unroll):
                e = jnp.exp2(s_ref[group_slot, u, st] - (m_new - st_ref[group_slot, u, st, 1])[None])
                e_ref[group_slot, u, st] = e.reshape(blk, blk).astype(jnp.bfloat16)
        return new_ms, alphas

    def accumulate(group, slot, alphas):
        for st in streams:
            acc = alphas[st][None] * acc_ref[st]
            for u in range(unroll):
                n = group * unroll + u
                vblock = jnp.where(n == 0, tile_of[st], past_block(n))
                acc = acc + _dot(vt_ref[vblock, head_of[st]], e_ref[slot, u, st]).reshape(acc_groups, sub, blk)
            acc_ref[st] = acc

    def score_group(group, slot, own_first):
        items = [group * unroll + u for u in range(unroll)]
        raws = issue_scores(items, own_first)
        return lambda: stage_scores(raws, slot, items, own_first)

    def tick(g, slot, ms, alphas):
        items = [g * unroll + u for u in range(unroll)]
        raws = [[None] * len(streams) for _ in items]
        for st in streams:
            for u, n in enumerate(items):
                rows = pl.ds(pl.multiple_of(past_block(n) * blk, blk), blk)
                raws[u][st] = _dot_nt(ka_ref[head_of[st], rows, :], qh[st]).reshape(tile3)
            acc = alphas[st][None] * acc_ref[st]
            for u in range(unroll):
                n = (g - 2) * unroll + u
                vblock = jnp.where(n == 0, tile_of[st], past_block(n))
                acc = acc + _dot(vt_ref[vblock, head_of[st]], e_ref[slot, u, st]).reshape(acc_groups, sub, blk)
            acc_ref[st] = acc
        ms, alphas = exponentiate(1 - slot, ms)
        stage_scores(raws, slot, items, False)
        return ms, alphas

    n_st = len(streams)
    pairs = jnp.maximum(lax.shift_right_logical(last_tile + 2 * unroll, MOBA_UNROLL_LOG2 + 1), 1)
    ms = [jnp.full((sub, blk), -jnp.inf, jnp.float32) for _ in streams]
    acc_ref[...] = jnp.zeros(acc_ref.shape, jnp.float32)
    finish_scores0 = score_group(0, 0, True)
    finish_scores1 = score_group(1, 1, False)
    finish_scores0()
    ms, alphas = exponentiate(0, ms)
    finish_scores1()

    def two_ticks(k, carry):
        ms, alphas = tick(2 * k, 0, list(carry[:n_st]), list(carry[n_st:]))
        ms, alphas = tick(2 * k + 1, 1, ms, alphas)
        return tuple(ms + alphas)

    carry = lax.fori_loop(1, pairs, two_ticks, tuple(ms + alphas))
    ms, alphas = list(carry[:n_st]), list(carry[n_st:])
    accumulate(2 * pairs - 2, 0, alphas)
    ms, alphas = exponentiate(1, ms)
    for w in range(MOBA_TILES):
        _moba_select(jnp.minimum(tile_of[2 * w] + MOBA_TILES, nblk - 1), w, q_ref, km_ref, sel_ref)
    accumulate(2 * pairs - 1, 1, alphas)
    for w in range(MOBA_TILES):
        accs = [acc_ref[2 * w + h] for h in (0, 1)]
        o_t = jnp.concatenate([(acc[0:half // sub] / acc[half // sub][None]).reshape(half, blk) for acc in accs],
                              axis=0)
        o_ref[pl.ds(pl.multiple_of(tile_of[2 * w] * blk, blk), blk), :] = o_t.T.astype(o_ref.dtype)


def _moba_slopes():
    full = np.asarray([sl * LOG2E for sl in SLOPES_A], np.float32)
    hi = full.astype(jnp.bfloat16).astype(np.float32)
    lo = (full - hi).astype(jnp.bfloat16).astype(np.float32)
    return jnp.asarray(np.concatenate([full, hi, lo]))


def _moba_call(z):
    s = z.shape[0]
    nblk = s // MOBA_BLOCK
    pairs = A_HEADS // 2
    ln = LANES_V7X
    qb, kb, vb = Z_A // ln, Z_A // ln + pairs, Z_A // ln + 2 * pairs
    return pl.pallas_call(
        _moba_kernel,
        grid=(pairs,),
        in_specs=[
            pl.BlockSpec(memory_space=pltpu.SMEM),
            pl.BlockSpec((s, ln), lambda p: (0, qb + p), pipeline_mode=pl.Buffered(1)),
            pl.BlockSpec((s, ln), lambda p: (0, kb + p), pipeline_mode=pl.Buffered(1)),
            pl.BlockSpec((s, ln), lambda p: (0, vb + p), pipeline_mode=pl.Buffered(1)),
        ],
        out_specs=pl.BlockSpec((s, ln), lambda p: (0, p)),
        out_shape=jax.ShapeDtypeStruct((s, A_HEADS * HEAD_DIM), jnp.bfloat16),
        scratch_shapes=[
            pltpu.VMEM((nblk, 2, MOBA_V_ROWS, MOBA_BLOCK), jnp.bfloat16),
            pltpu.VMEM((nblk, ln), jnp.float32),
            pltpu.VMEM((2, s, ln), jnp.bfloat16),
            pltpu.VMEM((2 * MOBA_TILES, nblk, MOBA_BLOCK), jnp.float32),
            pltpu.VMEM((2, MOBA_UNROLL, 2 * MOBA_TILES, MOBA_GROUPS, SUBLANES_V7X, MOBA_BLOCK), jnp.float32),
            pltpu.VMEM((2, MOBA_UNROLL, 2 * MOBA_TILES, MOBA_BLOCK, MOBA_BLOCK), jnp.bfloat16),
            pltpu.VMEM((2, MOBA_UNROLL, 2 * MOBA_TILES, 2, SUBLANES_V7X, MOBA_BLOCK), jnp.float32),
            pltpu.VMEM((2 * MOBA_TILES, MOBA_V_ROWS // SUBLANES_V7X, SUBLANES_V7X, MOBA_BLOCK), jnp.float32),
        ],
        compiler_params=_params(("arbitrary",)),
        name="moba_attn",
    )(_moba_slopes(), z, z, z)


def _blocks_per_step(nb, most):
    return max(c for c in (8, 4, 2, 1) if c <= most and nb % c == 0)


def _band_penalty_table(slope_dils, max_steps):
    steps = np.arange(BAND)[:, None] + BAND - np.arange(2 * BAND)[None, :]
    inside = (steps >= 0) & (steps <= max_steps)
    table = [np.where(inside, np.float32(sd) * steps.astype(np.float32), np.float32(-NEG_BIG)) for sd in slope_dils]
    return jnp.asarray(np.stack(table).astype(np.float32))


def _band_no_prev(first_step):
    kj = lax.broadcasted_iota(jnp.int32, (BAND, 2 * BAND), 1)
    return jnp.logical_and(first_step, kj < BAND)


def _band_units(units, lookahead):
    raws = {j: _dot_nt(units[j][0], units[j][1]) for j in range(min(lookahead, len(units)))}
    outs, lses = [], []
    for j, (_, _, vv, penalty, drop, sink) in enumerate(units):
        ahead = j + lookahead
        if ahead < len(units):
            raws[ahead] = _dot_nt(units[ahead][0], units[ahead][1])
        s = raws.pop(j) - penalty
        if drop is not None:
            s = jnp.where(drop, NEG_BIG, s)
        m = jnp.max(s, axis=1, keepdims=True)
        if sink is not None:
            m = jnp.maximum(m, sink)
        e = jnp.exp(s - m)
        denom = jnp.sum(e, axis=1, keepdims=True)
        if sink is not None:
            denom = denom + jnp.exp(sink - m)
        outs.append(_dot(e.astype(jnp.bfloat16), vv) / denom)
        lses.append(m + jnp.log(denom))
    return outs, lses


def _swa_kernel(sinks_ref, penalty_ref, q_ref, kp_ref, ko_ref, vp_ref, vo_ref, o_ref):
    n = pl.program_id(0)
    nbq = q_ref.shape[0] // BAND
    no_prev = _band_no_prev(n == 0)
    k_all = jnp.concatenate([kp_ref[...], ko_ref[...]], axis=0)
    v_all = jnp.concatenate([vp_ref[...], vo_ref[...]], axis=0)
    ln = LANES_V7X
    lane = lax.broadcasted_iota(jnp.int32, (BAND, ln), 1)
    low = lane < HEAD_DIM
    units = []
    for qb in range(nbq):
        kk = k_all[qb * BAND:(qb + 2) * BAND]
        vv = v_all[qb * BAND:(qb + 2) * BAND]
        drop = no_prev if qb == 0 else None
        for b in range(B_HEADS // 2):
            q2 = q_ref[qb * BAND:(qb + 1) * BAND, b * ln:(b + 1) * ln]
            for c in range(2):
                head = B_HEAD_ORDER[2 * b + c]
                qm = jnp.where(low if c == 0 else jnp.logical_not(low), q2, jnp.zeros_like(q2))
                units.append((qm, kk, vv, penalty_ref[2 * b + c], drop, sinks_ref[head]))
    outs, _ = _band_units(units, B_HEADS // 2)
    for qb in range(nbq):
        for b in range(B_HEADS // 2):
            j = qb * B_HEADS + 2 * b
            o_ref[qb * BAND:(qb + 1) * BAND, b * ln:(b + 1) * ln] = (
                jnp.where(low, outs[j], outs[j + 1]).astype(o_ref.dtype))


def _swa_call(z, sinks):
    s = z.shape[0]
    nbq = _blocks_per_step(s // BAND, 4)
    rows = nbq * BAND
    ln = LANES_V7X
    qw = B_HEADS * HEAD_DIM
    prev = lambda n: jnp.maximum(n * nbq - 1, 0)
    penalty = _band_penalty_table([SLOPES_B[head] for head in B_HEAD_ORDER], B_WINDOW - 1)
    return pl.pallas_call(
        _swa_kernel,
        grid=(s // rows,),
        in_specs=[
            pl.BlockSpec(memory_space=pltpu.SMEM),
            pl.BlockSpec(penalty.shape,lambda n: (0, 0, 0), pipeline_mode=pl.Buffered(1)),
            pl.BlockSpec((rows, qw), lambda n: (n, Z_BQ // qw)),
            pl.BlockSpec((BAND, ln), lambda n: (prev(n), Z_BK // ln)),
            pl.BlockSpec((rows, ln), lambda n: (n, Z_BK // ln)),
            pl.BlockSpec((BAND, ln), lambda n: (prev(n), Z_BV // ln)),
            pl.BlockSpec((rows, ln), lambda n: (n, Z_BV // ln)),
        ],
        out_specs=pl.BlockSpec((rows, qw), lambda n: (n, 0)),
        out_shape=jax.ShapeDtypeStruct((s, qw), jnp.bfloat16),
        compiler_params=_params(("arbitrary",)),
        name="swa_attn",
    )(sinks, penalty, z, z, z, z, z)


def _dilated_kernel(penalty_ref, q_ref, kp_ref, ko_ref, vp_ref, vo_ref, o_ref, lse_ref):
    n = pl.program_id(1)
    nbq = q_ref.shape[0] // BAND
    no_prev = _band_no_prev(n == 0)
    k_all = jnp.concatenate([kp_ref[...], ko_ref[...]], axis=0)
    v_all = jnp.concatenate([vp_ref[...], vo_ref[...]], axis=0)
    cw = q_ref.shape[1]
    lane = lax.broadcasted_iota(jnp.int32, (BAND, cw), 1)
    heads = range(C_HEADS_PER_GROUP)
    mine = [(lane >= h * HEAD_DIM) & (lane < (h + 1) * HEAD_DIM) for h in heads]
    penalties = [penalty_ref[h] for h in heads]
    units = []
    for qb in range(nbq):
        q4 = q_ref[qb * BAND:(qb + 1) * BAND, :]
        kk = k_all[qb * BAND:(qb + 2) * BAND]
        vv = v_all[qb * BAND:(qb + 2) * BAND]
        drop = no_prev if qb == 0 else None
        for h in heads:
            units.append((jnp.where(mine[h], q4, jnp.zeros_like(q4)), kk, vv, penalties[h], drop, None))
    outs, lses = _band_units(units, len(units))
    for qb in range(nbq):
        o_all = jnp.zeros((BAND, cw), jnp.float32)
        lse_all = jnp.zeros((BAND, cw), jnp.float32)
        for h in heads:
            j = qb * C_HEADS_PER_GROUP + h
            o_all = jnp.where(mine[h], outs[j], o_all)
            lse_all = jnp.where(mine[h], lses[j], lse_all)
        o_ref[qb * BAND:(qb + 1) * BAND, :] = o_all
        lse_ref[qb * BAND:(qb + 1) * BAND, :] = lse_all


def _dilated_call(zc, group, col0):
    window, dil = C_GROUPS[group]
    assert zc.shape[0] == dil
    length = zc.shape[1]
    nbq = _blocks_per_step(length // BAND, 8)
    rows = nbq * BAND
    cw = C_HEADS_PER_GROUP * HEAD_DIM
    base = col0 // cw
    slopes = [SLOPES_C[group * C_HEADS_PER_GROUP + h] * dil for h in range(C_HEADS_PER_GROUP)]
    penalty = _band_penalty_table(slopes, window // dil)
    own = lambda col: pl.BlockSpec((None, rows, cw), lambda r, n: (r, n, base + col))
    prev = lambda col: pl.BlockSpec((None, BAND, cw), lambda r, n: (r, jnp.maximum(n * nbq - 1, 0), base + col))
    out_blk = pl.BlockSpec((None, rows, cw), lambda r, n: (r, n, 0))
    return pl.pallas_call(
        _dilated_kernel,
        grid=(dil, length // rows),
        in_specs=[pl.BlockSpec(penalty.shape,lambda r, n: (0, 0, 0), pipeline_mode=pl.Buffered(1)),
                  own(0), prev(1), own(1), prev(2), own(2)],
        out_specs=[out_blk, out_blk],
        out_shape=[jax.ShapeDtypeStruct((dil, length, cw), jnp.float32)] * 2,
        compiler_params=_params(("arbitrary", "arbitrary")),
        name=f"dilated_attn_g{group}",
    )(penalty, zc, zc, zc, zc, zc)


def _merge_kernel(h_ref, zg_ref, oa_ref, ob_ref, o0_ref, l0_ref, o1_ref, l1_ref, o2_ref, l2_ref,
                  wa_ref, wb_ref, wc_ref, wo_ref, g_ref, mod_ref, out_ref, nat_ref):
    tm = h_ref.shape[0]

    def natural(ref, slot):
        dil = ref.shape[0]
        if dil == 1:
            return ref[0]
        ln = LANES_V7X
        chunks = ref.shape[2] // ln
        for r in range(dil):
            for c in range(chunks):
                nat_ref[slot * chunks + c, pl.ds(r, tm // dil, stride=dil), :] = ref[r, :, c * ln:(c + 1) * ln]
        return jnp.concatenate([nat_ref[slot * chunks + c] for c in range(chunks)], axis=1)

    l0, l1, l2 = natural(l0_ref, 0), natural(l1_ref, 0), natural(l2_ref, 1)
    o0, o1, o2 = natural(o0_ref, 0), natural(o1_ref, 2), natural(o2_ref, 3)
    mx = jnp.maximum(jnp.maximum(l0, l1), l2)
    e0, e1, e2 = jnp.exp(l0 - mx), jnp.exp(l1 - mx), jnp.exp(l2 - mx)
    oc = ((e0 * o0 + e1 * o1 + e2 * o2) / (e0 + e1 + e2)).astype(jnp.bfloat16)
    d = D_MODEL
    subs = range(0, tm, MERGE_SUB_ROWS)
    wa, wb, wc, wo = [ref[...].astype(jnp.bfloat16) for ref in (wa_ref, wb_ref, wc_ref, wo_ref)]
    branches = [(_dot(oa_ref[pl.ds(r0, MERGE_SUB_ROWS), :], wa),
                 _dot(ob_ref[pl.ds(r0, MERGE_SUB_ROWS), :], wb),
                 _dot(oc[r0:r0 + MERGE_SUB_ROWS], wc)) for r0 in subs]
    ys = []
    for r0, (ya, yb, yc) in zip(subs, branches):
        gates = jax.nn.sigmoid(zg_ref[pl.ds(r0, MERGE_SUB_ROWS), :].astype(jnp.float32))
        merged = gates[:, 0:d] * ya + gates[:, d:2 * d] * yb + gates[:, 2 * d:3 * d] * yc
        ys.append(_dot(merged.astype(jnp.bfloat16), wo))
    for r0, y in zip(subs, ys):
        rs = pl.ds(r0, MERGE_SUB_ROWS)
        out_ref[rs, :] = h_ref[rs, :] + mod_ref[2:3, :] * _rmsnorm(y, g_ref[...])


MERGE_SUB_ROWS = 256


def _merge_call(h, z, oa, ob, oc_lse, wa, wb, wc, wo, g, mod, layer):
    s, d = h.shape
    tm = 2 * MERGE_SUB_ROWS
    cw = C_HEADS_PER_GROUP * HEAD_DIM
    row = lambda w: pl.BlockSpec((tm, w), lambda i: (i, 0))
    res = lambda a: pl.BlockSpec((a.shape[0], tm // a.shape[0], cw), lambda i: (0, i, 0))
    full = lambda a: pl.BlockSpec(a.shape, lambda i: (0, 0), pipeline_mode=pl.Buffered(1))
    stacked = lambda a: pl.BlockSpec((None,) + a.shape[1:], lambda i: (layer, 0, 0), pipeline_mode=pl.Buffered(1))
    return pl.pallas_call(
        _merge_kernel,
        grid=(s // tm,),
        in_specs=[row(d), row(3 * d), row(oa.shape[1]), row(ob.shape[1])] + [res(a) for a in oc_lse]
        + [stacked(wa), stacked(wb), stacked(wc), stacked(wo), full(g), full(mod)],
        out_specs=row(d),
        out_shape=jax.ShapeDtypeStruct((s, d), jnp.float32),
        scratch_shapes=[pltpu.VMEM((4 * cw // LANES_V7X, tm, LANES_V7X), jnp.float32)],
        input_output_aliases={0: 0} if layer > 0 else {},
        compiler_params=_params(("arbitrary",)),
        name="merge_out_proj",
    )(h, z, oa, ob, *oc_lse, wa, wb, wc, wo, g, mod)


FFN_CHUNK = 1408


def _ffn_kernel(h_ref, g_pre_ref, g_post_ref, mod_ref, wg_ref, wu_ref, wd_ref, out_ref):
    x = h_ref[...]
    u = _rmsnorm(x, g_pre_ref[...]) * (1.0 + mod_ref[4:5, :]) + mod_ref[3:4, :]
    u = u.astype(jnp.bfloat16)
    y = jnp.zeros(x.shape, jnp.float32)
    for c0 in range(0, D_FF, FFN_CHUNK):
        gate = _dot(u, wg_ref[:, c0:c0 + FFN_CHUNK].astype(jnp.bfloat16))
        up = _dot(u, wu_ref[:, c0:c0 + FFN_CHUNK].astype(jnp.bfloat16))
        act = (gate * jax.nn.sigmoid(gate) * up).astype(jnp.bfloat16)
        y = y + _dot(act, wd_ref[c0:c0 + FFN_CHUNK, :].astype(jnp.bfloat16))
    out_ref[...] = x + mod_ref[5:6, :] * _rmsnorm(y, g_post_ref[...])


def _ffn_call(h, g_pre, g_post, mod, wg, wu, wd, layer):
    s, d = h.shape
    tm = ROW_TILE
    row = pl.BlockSpec((tm, d), lambda i: (i, 0))
    full = lambda a: pl.BlockSpec(a.shape, lambda i: (0, 0), pipeline_mode=pl.Buffered(1))
    stacked = lambda a: pl.BlockSpec((None,) + a.shape[1:], lambda i: (layer, 0, 0), pipeline_mode=pl.Buffered(1))
    return pl.pallas_call(
        _ffn_kernel,
        grid=(s // tm,),
        in_specs=[row, full(g_pre), full(g_post), full(mod), stacked(wg), stacked(wu), stacked(wd)],
        out_specs=row,
        out_shape=jax.ShapeDtypeStruct((s, d), jnp.float32),
        input_output_aliases={0: 0},
        compiler_params=_params(("arbitrary",)),
        name="swiglu_ffn",
    )(h, g_pre, g_post, mod, wg, wu, wd)


def _permute_in_proj(w_in):
    hd = HEAD_DIM
    o_bq = 3 * A_HEADS * hd
    o_bkv = o_bq + B_HEADS * hd
    o_c = o_bkv + 2 * B_KV_HEADS * hd
    o_g = o_c + 3 * C_HEADS * hd
    cw = C_HEADS_PER_GROUP * hd
    sl = lambda a, b: w_in[:, :, a:b]
    parts = [sl(o_g, o_g + 3 * D_MODEL), sl(0, A_HEADS * hd) * (QK_SCALE * LOG2E), sl(A_HEADS * hd, o_bq)]
    parts += [sl(o_bq + hh * hd, o_bq + (hh + 1) * hd) * QK_SCALE for hh in B_HEAD_ORDER]
    parts += [sl(o_bkv, o_c)]
    for g in range(len(C_GROUPS)):
        parts += [sl(o_c + g * 3 * cw, o_c + g * 3 * cw + cw) * QK_SCALE,
                  sl(o_c + g * 3 * cw + cw, o_c + (g + 1) * 3 * cw)]
    out = jnp.concatenate([part.astype(jnp.bfloat16) for part in parts], axis=2)
    assert out.shape[2] == IN_WIDTH
    return out


def kernel(x, c, w_ada, b_ada, g_pre_mix, g_post_mix, w_in, sinks, w_br_a, w_br_b, w_br_c,
           w_out, g_pre_ffn, g_post_ffn, w_gate, w_up, w_down):
    bn, s, d = x.shape
    assert bn == 1 and d == D_MODEL and s % SEQ_MULTIPLE == 0
    hd = HEAD_DIM
    w_in_p = _permute_in_proj(w_in)
    w_br_b_p = jnp.concatenate([w_br_b[:, hh * hd:(hh + 1) * hd] for hh in B_HEAD_ORDER], axis=1)

    mod_all = _ada_call(c, w_ada, b_ada).reshape(DEPTH, 6, d)
    h = x.reshape(s, d)
    for l in range(DEPTH):
        mod = mod_all[l]
        z, zc1, zc2 = _in_proj_call(h, g_pre_mix[l].reshape(1, d), mod, w_in_p, l)
        oa = _moba_call(z)
        ob = _swa_call(z, sinks[l])
        oc_lse = (_dilated_call(z.reshape(1, s, Z_WIDTH), 0, Z_C0)
                  + _dilated_call(zc1, 1, 0) + _dilated_call(zc2, 2, 0))
        h = _merge_call(h, z, oa, ob, oc_lse, w_br_a, w_br_b_p, w_br_c, w_out,
                        g_post_mix[l].reshape(1, d), mod, l)
        h = _ffn_call(h, g_pre_ffn[l].reshape(1, d), g_post_ffn[l].reshape(1, d), mod,
                      w_gate, w_up, w_down, l)
    return h.reshape(bn, s, d)
```

```python
import numpy as np
import jax
import jax.numpy as jnp
from jax import lax
from jax.experimental import pallas as pl
from jax.experimental.pallas import tpu as pltpu

D_MODEL = 1024
DEPTH = 4
HEAD_DIM = 64
A_HEADS = 8
MOBA_BLOCK = 256
MOBA_TOPK = 3
B_HEADS = 8
B_KV_HEADS = 2
B_WINDOW = 128
C_GROUPS = ((128, 1), (512, 4), (2048, 16))
C_HEADS_PER_GROUP = 4
C_HEADS = len(C_GROUPS) * C_HEADS_PER_GROUP
BAND = 128
D_FF = 2816
N_ALIBI_HEADS = A_HEADS + B_HEADS + C_HEADS
SEQ_MULTIPLE = 2048
RMS_EPS = 1e-6

LANES_V7X = 128
BF16_SUBLANES_V7X = 16
VMEM_LIMIT_BYTES_V7X = 56 * 1024 * 1024

ROW_TILE = 512
C_WIDTH = 3 * C_HEADS_PER_GROUP * HEAD_DIM

Z_GATES = 0
Z_A = 3 * D_MODEL
Z_BQ = Z_A + 3 * A_HEADS * HEAD_DIM
Z_BK = Z_BQ + B_HEADS * HEAD_DIM
Z_BV = Z_BK + B_KV_HEADS * HEAD_DIM
Z_C0 = Z_BV + B_KV_HEADS * HEAD_DIM
Z_WIDTH = Z_C0 + C_WIDTH
IN_WIDTH = Z_WIDTH + (len(C_GROUPS) - 1) * C_WIDTH
B_HEAD_ORDER = (0, 4, 1, 5, 2, 6, 3, 7)

NEG_BIG = -1e30
QK_SCALE = HEAD_DIM ** -0.5
LOG2E = 1.4426950408889634


def _alibi_slopes():
    n = N_ALIBI_HEADS
    return [float(2.0 ** (-8.0 * (i + 1) / n)) for i in range(n)]


_SLOPES = _alibi_slopes()
SLOPES_B = _SLOPES[:B_HEADS]
SLOPES_C = _SLOPES[B_HEADS:B_HEADS + C_HEADS]
SLOPES_A = _SLOPES[B_HEADS + C_HEADS:]


def _dot(a, b):
    return jnp.dot(a, b, preferred_element_type=jnp.float32)


def _dot_nt(a, b):
    return lax.dot_general(a, b, (((1,), (1,)), ((), ())), preferred_element_type=jnp.float32)


def _params(semantics):
    return pltpu.CompilerParams(dimension_semantics=semantics, vmem_limit_bytes=VMEM_LIMIT_BYTES_V7X)


def _rmsnorm(x, g):
    return x * lax.rsqrt(jnp.mean(x * x, axis=-1, keepdims=True) + RMS_EPS) * g


def _ada_kernel(c_ref, w_ref, b_ref, o_ref):
    c = c_ref[...]
    sc = c * jax.nn.sigmoid(c)
    o_ref[...] = jnp.sum(w_ref[...] * sc, axis=0, keepdims=True) + b_ref[...]


def _ada_call(c, w_ada, b_ada):
    depth, d, n = w_ada.shape
    tn = 1536
    return pl.pallas_call(
        _ada_kernel,
        grid=(depth, n // tn),
        in_specs=[
            pl.BlockSpec((d, 1), lambda l, j: (0, 0)),
            pl.BlockSpec((None, d, tn), lambda l, j: (l, 0, j)),
            pl.BlockSpec((None, 1, tn), lambda l, j: (l, 0, j)),
        ],
        out_specs=pl.BlockSpec((None, 1, tn), lambda l, j: (l, 0, j)),
        out_shape=jax.ShapeDtypeStruct((depth, 1, n), jnp.float32),
        compiler_params=_params(("arbitrary", "arbitrary")),
        name="adaln_mod",
    )(c.reshape(d, 1), w_ada, b_ada.reshape(depth, 1, n))


IN_PROJ_CHUNK = 512


def _in_proj_kernel(h_ref, g_ref, mod_ref, w_ref, z_ref, zc1_ref, zc2_ref, tmp_ref):
    x = h_ref[...]
    u = _rmsnorm(x, g_ref[...]) * (1.0 + mod_ref[1:2, :]) + mod_ref[0:1, :]
    u = u.astype(jnp.bfloat16)
    for c0 in range(0, Z_WIDTH, IN_PROJ_CHUNK):
        z_ref[:, c0:c0 + IN_PROJ_CHUNK] = _dot(u, w_ref[:, c0:c0 + IN_PROJ_CHUNK]).astype(z_ref.dtype)
    for g, out_ref in ((1, zc1_ref), (2, zc2_ref)):
        c0 = Z_WIDTH + (g - 1) * C_WIDTH
        res = _dot(u, w_ref[:, c0:c0 + C_WIDTH])
        dil = C_GROUPS[g][1]
        rows = x.shape[0] // dil
        ln = LANES_V7X
        for c in range(C_WIDTH // ln):
            tmp_ref[c] = res[:, c * ln:(c + 1) * ln]
        for r in range(dil):
            for c in range(C_WIDTH // ln):
                out_ref[r, :, c * ln:(c + 1) * ln] = tmp_ref[c, pl.ds(r, rows, stride=dil), :].astype(out_ref.dtype)


def _in_proj_call(h, g, mod, w, layer):
    s, d = h.shape
    tm = ROW_TILE
    d1, d2 = C_GROUPS[1][1], C_GROUPS[2][1]
    assert tm % (d2 * BF16_SUBLANES_V7X) == 0
    return pl.pallas_call(
        _in_proj_kernel,
        grid=(s // tm,),
        in_specs=[
            pl.BlockSpec((tm, d), lambda i: (i, 0)),
            pl.BlockSpec((1, d), lambda i: (0, 0)),
            pl.BlockSpec((6, d), lambda i: (0, 0)),
            pl.BlockSpec((None, d, IN_WIDTH), lambda i: (layer, 0, 0), pipeline_mode=pl.Buffered(1)),
        ],
        out_specs=[
            pl.BlockSpec((tm, Z_WIDTH), lambda i: (i, 0)),
            pl.BlockSpec((d1, tm // d1, C_WIDTH), lambda i: (0, i, 0)),
            pl.BlockSpec((d2, tm // d2, C_WIDTH), lambda i: (0, i, 0)),
        ],
        out_shape=[
            jax.ShapeDtypeStruct((s, Z_WIDTH), jnp.bfloat16),
            jax.ShapeDtypeStruct((d1, s // d1, C_WIDTH), jnp.bfloat16),
            jax.ShapeDtypeStruct((d2, s // d2, C_WIDTH), jnp.bfloat16),
        ],
        scratch_shapes=[pltpu.VMEM((C_WIDTH // LANES_V7X, tm, LANES_V7X), jnp.float32)],
        compiler_params=_params(("arbitrary",)),
        name="in_proj",
    )(h, g, mod, w)


SUBLANES_V7X = 8
MOBA_V_ROWS = HEAD_DIM + BF16_SUBLANES_V7X
MOBA_GROUPS = MOBA_BLOCK // SUBLANES_V7X
MOBA_UNROLL_LOG2 = 1
MOBA_UNROLL = 1 << MOBA_UNROLL_LOG2
MOBA_TILES = 4


def _all_sublanes_max(x):
    for shift in (4, 2, 1):
        x = jnp.maximum(x, pltpu.roll(x, shift, axis=0))
    return x


def _moba_lanes():
    lane = lax.broadcasted_iota(jnp.int32, (MOBA_BLOCK, LANES_V7X), 1)
    own = [lane < HEAD_DIM, lane >= HEAD_DIM]
    spare = [HEAD_DIM, 0]
    bias_lanes = [(lane == spare[h]) | (lane == spare[h] + 1) for h in (0, 1)]
    return lane, own, spare, bias_lanes


def _moba_kernel(slopes_ref, q_ref, k_ref, v_ref, o_ref, vt_ref, km_ref, ka_ref, sel_ref, s_ref, e_ref, st_ref, acc_ref):
    p = pl.program_id(0)
    nblk = vt_ref.shape[0]
    blk = MOBA_BLOCK
    half = HEAD_DIM
    _, own, _, bias_lanes = _moba_lanes()
    ones = jnp.ones((BF16_SUBLANES_V7X, blk), jnp.bfloat16)

    def prepare(j, carry):
        rows = pl.ds(pl.multiple_of(j * blk, blk), blk)
        vt = v_ref[rows, :].astype(jnp.float32).T.astype(jnp.bfloat16)
        k2 = k_ref[rows, :]
        km_ref[pl.ds(j, 1), :] = jnp.mean(k2.astype(jnp.float32), axis=0, keepdims=True)
        k2f = k2.astype(jnp.float32)
        pos = lax.broadcasted_iota(jnp.int32, k2.shape, 0).astype(jnp.float32)
        for h in (0, 1):
            vt_ref[j, h, 0:half, :] = vt[h * half:(h + 1) * half, :]
            vt_ref[j, h, half:, :] = ones
            ka = jnp.where(own[h], k2f, jnp.where(bias_lanes[h], pos, 0.0))
            ka_ref[h, rows, :] = ka.astype(jnp.bfloat16)
        return carry

    lax.fori_loop(0, nblk, prepare, 0)
    for w in range(MOBA_TILES):
        _moba_select(w, w, q_ref, km_ref, sel_ref)

    def tiles(t, carry):
        _moba_tiles(t, p, slopes_ref, q_ref, o_ref, vt_ref, km_ref, ka_ref, sel_ref, s_ref, e_ref, st_ref, acc_ref)
        return carry

    lax.fori_loop(0, nblk // MOBA_TILES, tiles, 0)


def _moba_select(i, w, q_ref, km_ref, sel_ref):
    nblk, blk = sel_ref.shape[1], MOBA_BLOCK
    _, own, _, _ = _moba_lanes()
    q2 = q_ref[pl.ds(pl.multiple_of(i * blk, blk), blk), :]
    blk_id = lax.broadcasted_iota(jnp.int32, (nblk, blk), 0)
    km = km_ref[...]
    km_hi = km.astype(jnp.bfloat16)
    km_lo = (km - km_hi.astype(jnp.float32)).astype(jnp.bfloat16)
    for h in (0, 1):
        qz = jnp.where(own[h], q2, jnp.zeros_like(q2))
        gate = _dot_nt(km_hi, qz) + _dot_nt(km_lo, qz)
        gate = jnp.where(blk_id < i, gate, -jnp.inf)
        sel = jnp.full((nblk, blk), NEG_BIG, jnp.float32)
        for _ in range(MOBA_TOPK):
            mx = jnp.max(gate, axis=0, keepdims=True)
            cand = (gate == mx) & (mx > -jnp.inf)
            idx = jnp.min(jnp.where(cand, blk_id, nblk), axis=0, keepdims=True)
            chosen = blk_id == idx
            sel = jnp.where(chosen, 0.0, sel)
            gate = jnp.where(chosen, -jnp.inf, gate)
        sel_ref[2 * w + h] = sel


def _moba_tiles(t, p, slopes_ref, q_ref, o_ref, vt_ref, km_ref, ka_ref, sel_ref, s_ref, e_ref, st_ref, acc_ref):
    nblk = vt_ref.shape[0]
    blk = MOBA_BLOCK
    half = HEAD_DIM
    sub = SUBLANES_V7X
    tile3 = (MOBA_GROUPS, sub, blk)
    key_pos = lax.broadcasted_iota(jnp.int32, tile3, 0) * sub + lax.broadcasted_iota(jnp.int32, tile3, 1)
    qry_pos = lax.broadcasted_iota(jnp.int32, tile3, 2)
    lane, own, spare, _ = _moba_lanes()
    streams = range(2 * MOBA_TILES)
    tile_of = [t * MOBA_TILES + st // 2 for st in streams]
    head_of = [st % 2 for st in streams]
    last_tile = tile_of[-1]

    qh = []
    for st in streams:
        h = head_of[st]
        q2f = q_ref[pl.ds(pl.multiple_of(tile_of[st] * blk, blk), blk), :].astype(jnp.float32)
        hi = slopes_ref[A_HEADS + 2 * p + h]
        lo = slopes_ref[2 * A_HEADS + 2 * p + h]
        extra = jnp.where(lane == spare[h], hi, jnp.where(lane == spare[h] + 1, lo, 0.0))
        qh.append(jnp.where(own[h], q2f, extra).astype(jnp.bfloat16))

    unroll = MOBA_UNROLL
    acc_groups = MOBA_V_ROWS // sub

    def past_block(n):
        return jnp.clip(n - 1, 0, nblk - 1)

    def item_block(n, st, own_first):
        return tile_of[st] if own_first else past_block(n)

    def issue_scores(items, own_first):
        out = []
        for u, n in enumerate(items):
            per_stream = []
            for st in streams:
                b = item_block(n, st, own_first and u == 0)
                rows = pl.ds(pl.multiple_of(b * blk, blk), blk)
                per_stream.append(_dot_nt(ka_ref[head_of[st], rows, :], qh[st]).reshape(tile3))
            out.append(per_stream)
        return out

    def stage_scores(raws, slot, items, own_first):
        for u, n in enumerate(items):
            for st in streams:
                s = raws[u][st]
                if own_first and u == 0:
                    s = jnp.where(qry_pos >= key_pos, s, NEG_BIG)
                    row = jnp.zeros((sub, blk), jnp.float32)
                else:
                    b = past_block(n)
                    gap = ((tile_of[st] - b) * blk).astype(jnp.float32)
                    row = sel_ref[st, pl.ds(b, 1), :] - slopes_ref[2 * p + head_of[st]] * gap
                    row = jnp.broadcast_to(jnp.where(n <= tile_of[st], row, NEG_BIG), (sub, blk))
                s_ref[slot, u, st] = s
                st_ref[slot, u, st, 0] = _all_sublanes_max(jnp.max(s, axis=0)) + row
                st_ref[slot, u, st, 1] = row

    def exponentiate(group_slot, ms):
        new_ms, alphas = [], []
        for st in streams:
            m_new = ms[st]
            for u in range(unroll):
                m_new = jnp.maximum(m_new, st_ref[group_slot, u, st, 0])
            alphas.append(jnp.exp2(ms[st] - m_new))
            new_ms.append(m_new)
            for u in range(unroll):
                e = jnp.exp2(s_ref[group_slot, u, st] - (m_new - st_ref[group_slot, u, st, 1])[None])
                e_ref[group_slot, u, st] = e.reshape(blk, blk).astype(jnp.bfloat16)
        return new_ms, alphas

    def accumulate(group, slot, alphas):
        for st in streams:
            acc = alphas[st][None] * acc_ref[st]
            for u in range(unroll):
                n = group * unroll + u
                vblock = jnp.where(n == 0, tile_of[st], past_block(n))
                acc = acc + _dot(vt_ref[vblock, head_of[st]], e_ref[slot, u, st]).reshape(acc_groups, sub, blk)
            acc_ref[st] = acc

    def score_group(group, slot, own_first):
        items = [group * unroll + u for u in range(unroll)]
        raws = issue_scores(items, own_first)
        return lambda: stage_scores(raws, slot, items, own_first)

    def tick(g, slot, ms, alphas):
        items = [g * unroll + u for u in range(unroll)]
        raws = [[None] * len(streams) for _ in items]
        for st in streams:
            for u, n in enumerate(items):
                rows = pl.ds(pl.multiple_of(past_block(n) * blk, blk), blk)
                raws[u][st] = _dot_nt(ka_ref[head_of[st], rows, :], qh[st]).reshape(tile3)
            acc = alphas[st][None] * acc_ref[st]
            for u in range(unroll):
                n = (g - 2) * unroll + u
                vblock = jnp.where(n == 0, tile_of[st], past_block(n))
                acc = acc + _dot(vt_ref[vblock, head_of[st]], e_ref[slot, u, st]).reshape(acc_groups, sub, blk)
            acc_ref[st] = acc
        ms, alphas = exponentiate(1 - slot, ms)
        stage_scores(raws, slot, items, False)
        return ms, alphas

    n_st = len(streams)
    pairs = jnp.maximum(lax.shift_right_logical(last_tile + 2 * unroll, MOBA_UNROLL_LOG2 + 1), 1)
    ms = [jnp.full((sub, blk), -jnp.inf, jnp.float32) for _ in streams]
    acc_ref[...] = jnp.zeros(acc_ref.shape, jnp.float32)
    finish_scores0 = score_group(0, 0, True)
    finish_scores1 = score_group(1, 1, False)
    finish_scores0()
    ms, alphas = exponentiate(0, ms)
    finish_scores1()

    def two_ticks(k, carry):
        ms, alphas = tick(2 * k, 0, list(carry[:n_st]), list(carry[n_st:]))
        ms, alphas = tick(2 * k + 1, 1, ms, alphas)
        return tuple(ms + alphas)

    carry = lax.fori_loop(1, pairs, two_ticks, tuple(ms + alphas))
    ms, alphas = list(carry[:n_st]), list(carry[n_st:])
    accumulate(2 * pairs - 2, 0, alphas)
    ms, alphas = exponentiate(1, ms)
    for w in range(MOBA_TILES):
        _moba_select(jnp.minimum(tile_of[2 * w] + MOBA_TILES, nblk - 1), w, q_ref, km_ref, sel_ref)
    accumulate(2 * pairs - 1, 1, alphas)
    for w in range(MOBA_TILES):
        accs = [acc_ref[2 * w + h] for h in (0, 1)]
        o_t = jnp.concatenate([(acc[0:half // sub] / acc[half // sub][None]).reshape(half, blk) for acc in accs],
                              axis=0)
        o_ref[pl.ds(pl.multiple_of(tile_of[2 * w] * blk, blk), blk), :] = o_t.T.astype(o_ref.dtype)


def _moba_slopes():
    full = np.asarray([sl * LOG2E for sl in SLOPES_A], np.float32)
    hi = full.astype(jnp.bfloat16).astype(np.float32)
    lo = (full - hi).astype(jnp.bfloat16).astype(np.float32)
    return jnp.asarray(np.concatenate([full, hi, lo]))


def _moba_call(z):
    s = z.shape[0]
    nblk = s // MOBA_BLOCK
    pairs = A_HEADS // 2
    ln = LANES_V7X
    qb, kb, vb = Z_A // ln, Z_A // ln + pairs, Z_A // ln + 2 * pairs
    return pl.pallas_call(
        _moba_kernel,
        grid=(pairs,),
        in_specs=[
            pl.BlockSpec(memory_space=pltpu.SMEM),
            pl.BlockSpec((s, ln), lambda p: (0, qb + p), pipeline_mode=pl.Buffered(1)),
            pl.BlockSpec((s, ln), lambda p: (0, kb + p), pipeline_mode=pl.Buffered(1)),
            pl.BlockSpec((s, ln), lambda p: (0, vb + p), pipeline_mode=pl.Buffered(1)),
        ],
        out_specs=pl.BlockSpec((s, ln), lambda p: (0, p)),
        out_shape=jax.ShapeDtypeStruct((s, A_HEADS * HEAD_DIM), jnp.bfloat16),
        scratch_shapes=[
            pltpu.VMEM((nblk, 2, MOBA_V_ROWS, MOBA_BLOCK), jnp.bfloat16),
            pltpu.VMEM((nblk, ln), jnp.float32),
            pltpu.VMEM((2, s, ln), jnp.bfloat16),
            pltpu.VMEM((2 * MOBA_TILES, nblk, MOBA_BLOCK), jnp.float32),
            pltpu.VMEM((2, MOBA_UNROLL, 2 * MOBA_TILES, MOBA_GROUPS, SUBLANES_V7X, MOBA_BLOCK), jnp.float32),
            pltpu.VMEM((2, MOBA_UNROLL, 2 * MOBA_TILES, MOBA_BLOCK, MOBA_BLOCK), jnp.bfloat16),
            pltpu.VMEM((2, MOBA_UNROLL, 2 * MOBA_TILES, 2, SUBLANES_V7X, MOBA_BLOCK), jnp.float32),
            pltpu.VMEM((2 * MOBA_TILES, MOBA_V_ROWS // SUBLANES_V7X, SUBLANES_V7X, MOBA_BLOCK), jnp.float32),
        ],
        compiler_params=_params(("arbitrary",)),
        name="moba_attn",
    )(_moba_slopes(), z, z, z)


def _blocks_per_step(nb, most):
    return max(c for c in (8, 4, 2, 1) if c <= most and nb % c == 0)


def _band_penalty_table(slope_dils, max_steps):
    steps = np.arange(BAND)[:, None] + BAND - np.arange(2 * BAND)[None, :]
    inside = (steps >= 0) & (steps <= max_steps)
    table = [np.where(inside, np.float32(sd) * steps.astype(np.float32), np.float32(-NEG_BIG)) for sd in slope_dils]
    return jnp.asarray(np.stack(table).astype(np.float32))


def _band_no_prev(first_step):
    kj = lax.broadcasted_iota(jnp.int32, (BAND, 2 * BAND), 1)
    return jnp.logical_and(first_step, kj < BAND)


def _band_units(units, lookahead):
    raws = {j: _dot_nt(units[j][0], units[j][1]) for j in range(min(lookahead, len(units)))}
    outs, lses = [], []
    for j, (_, _, vv, penalty, drop, sink) in enumerate(units):
        ahead = j + lookahead
        if ahead < len(units):
            raws[ahead] = _dot_nt(units[ahead][0], units[ahead][1])
        s = raws.pop(j) - penalty
        if drop is not None:
            s = jnp.where(drop, NEG_BIG, s)
        m = jnp.max(s, axis=1, keepdims=True)
        if sink is not None:
            m = jnp.maximum(m, sink)
        e = jnp.exp(s - m)
        denom = jnp.sum(e, axis=1, keepdims=True)
        if sink is not None:
            denom = denom + jnp.exp(sink - m)
        outs.append(_dot(e.astype(jnp.bfloat16), vv) / denom)
        lses.append(m + jnp.log(denom))
    return outs, lses


def _swa_kernel(sinks_ref, penalty_ref, q_ref, kp_ref, ko_ref, vp_ref, vo_ref, o_ref):
    n = pl.program_id(0)
    nbq = q_ref.shape[0] // BAND
    no_prev = _band_no_prev(n == 0)
    k_all = jnp.concatenate([kp_ref[...], ko_ref[...]], axis=0)
    v_all = jnp.concatenate([vp_ref[...], vo_ref[...]], axis=0)
    ln = LANES_V7X
    lane = lax.broadcasted_iota(jnp.int32, (BAND, ln), 1)
    low = lane < HEAD_DIM
    units = []
    for qb in range(nbq):
        kk = k_all[qb * BAND:(qb + 2) * BAND]
        vv = v_all[qb * BAND:(qb + 2) * BAND]
        drop = no_prev if qb == 0 else None
        for b in range(B_HEADS // 2):
            q2 = q_ref[qb * BAND:(qb + 1) * BAND, b * ln:(b + 1) * ln]
            for c in range(2):
                head = B_HEAD_ORDER[2 * b + c]
                qm = jnp.where(low if c == 0 else jnp.logical_not(low), q2, jnp.zeros_like(q2))
                units.append((qm, kk, vv, penalty_ref[2 * b + c], drop, sinks_ref[head]))
    outs, _ = _band_units(units, B_HEADS // 2)
    for qb in range(nbq):
        for b in range(B_HEADS // 2):
            j = qb * B_HEADS + 2 * b
            o_ref[qb * BAND:(qb + 1) * BAND, b * ln:(b + 1) * ln] = (
                jnp.where(low, outs[j], outs[j + 1]).astype(o_ref.dtype))


def _swa_call(z, sinks):
    s = z.shape[0]
    nbq = _blocks_per_step(s // BAND, 8)
    rows = nbq * BAND
    ln = LANES_V7X
    qw = B_HEADS * HEAD_DIM
    prev = lambda n: jnp.maximum(n * nbq - 1, 0)
    penalty = _band_penalty_table([SLOPES_B[head] for head in B_HEAD_ORDER], B_WINDOW - 1)
    return pl.pallas_call(
        _swa_kernel,
        grid=(s // rows,),
        in_specs=[
            pl.BlockSpec(memory_space=pltpu.SMEM),
            pl.BlockSpec(penalty.shape,lambda n: (0, 0, 0), pipeline_mode=pl.Buffered(1)),
            pl.BlockSpec((rows, qw), lambda n: (n, Z_BQ // qw)),
            pl.BlockSpec((BAND, ln), lambda n: (prev(n), Z_BK // ln)),
            pl.BlockSpec((rows, ln), lambda n: (n, Z_BK // ln)),
            pl.BlockSpec((BAND, ln), lambda n: (prev(n), Z_BV // ln)),
            pl.BlockSpec((rows, ln), lambda n: (n, Z_BV // ln)),
        ],
        out_specs=pl.BlockSpec((rows, qw), lambda n: (n, 0)),
        out_shape=jax.ShapeDtypeStruct((s, qw), jnp.bfloat16),
        compiler_params=_params(("arbitrary",)),
        name="swa_attn",
    )(sinks, penalty, z, z, z, z, z)


def _dilated_kernel(penalty_ref, q_ref, kp_ref, ko_ref, vp_ref, vo_ref, o_ref, lse_ref):
    n = pl.program_id(1)
    nbq = q_ref.shape[0] // BAND
    no_prev = _band_no_prev(n == 0)
    k_all = jnp.concatenate([kp_ref[...], ko_ref[...]], axis=0)
    v_all = jnp.concatenate([vp_ref[...], vo_ref[...]], axis=0)
    cw = q_ref.shape[1]
    lane = lax.broadcasted_iota(jnp.int32, (BAND, cw), 1)
    heads = range(C_HEADS_PER_GROUP)
    mine = [(lane >= h * HEAD_DIM) & (lane < (h + 1) * HEAD_DIM) for h in heads]
    penalties = [penalty_ref[h] for h in heads]
    units = []
    for qb in range(nbq):
        q4 = q_ref[qb * BAND:(qb + 1) * BAND, :]
        kk = k_all[qb * BAND:(qb + 2) * BAND]
        vv = v_all[qb * BAND:(qb + 2) * BAND]
        drop = no_prev if qb == 0 else None
        for h in heads:
            units.append((jnp.where(mine[h], q4, jnp.zeros_like(q4)), kk, vv, penalties[h], drop, None))
    outs, lses = _band_units(units, len(units))
    for qb in range(nbq):
        o_all = jnp.zeros((BAND, cw), jnp.float32)
        lse_all = jnp.zeros((BAND, cw), jnp.float32)
        for h in heads:
            j = qb * C_HEADS_PER_GROUP + h
            o_all = jnp.where(mine[h], outs[j], o_all)
            lse_all = jnp.where(mine[h], lses[j], lse_all)
        o_ref[qb * BAND:(qb + 1) * BAND, :] = o_all
        lse_ref[qb * BAND:(qb + 1) * BAND, :] = lse_all


def _dilated_call(zc, group, col0):
    window, dil = C_GROUPS[group]
    assert zc.shape[0] == dil
    length = zc.shape[1]
    nbq = _blocks_per_step(length // BAND, 8)
    rows = nbq * BAND
    cw = C_HEADS_PER_GROUP * HEAD_DIM
    base = col0 // cw
    slopes = [SLOPES_C[group * C_HEADS_PER_GROUP + h] * dil for h in range(C_HEADS_PER_GROUP)]
    penalty = _band_penalty_table(slopes, window // dil)
    own = lambda col: pl.BlockSpec((None, rows, cw), lambda r, n: (r, n, base + col))
    prev = lambda col: pl.BlockSpec((None, BAND, cw), lambda r, n: (r, jnp.maximum(n * nbq - 1, 0), base + col))
    out_blk = pl.BlockSpec((None, rows, cw), lambda r, n: (r, n, 0))
    return pl.pallas_call(
        _dilated_kernel,
        grid=(dil, length // rows),
        in_specs=[pl.BlockSpec(penalty.shape,lambda r, n: (0, 0, 0), pipeline_mode=pl.Buffered(1)),
                  own(0), prev(1), own(1), prev(2), own(2)],
        out_specs=[out_blk, out_blk],
        out_shape=[jax.ShapeDtypeStruct((dil, length, cw), jnp.float32)] * 2,
        compiler_params=_params(("arbitrary", "arbitrary")),
        name=f"dilated_attn_g{group}",
    )(penalty, zc, zc, zc, zc, zc)


def _merge_kernel(h_ref, zg_ref, oa_ref, ob_ref, o0_ref, l0_ref, o1_ref, l1_ref, o2_ref, l2_ref,
                  wa_ref, wb_ref, wc_ref, wo_ref, g_ref, mod_ref, out_ref, nat_ref):
    tm = h_ref.shape[0]

    def natural(ref, slot):
        dil = ref.shape[0]
        if dil == 1:
            return ref[0]
        ln = LANES_V7X
        chunks = ref.shape[2] // ln
        for r in range(dil):
            for c in range(chunks):
                nat_ref[slot * chunks + c, pl.ds(r, tm // dil, stride=dil), :] = ref[r, :, c * ln:(c + 1) * ln]
        return jnp.concatenate([nat_ref[slot * chunks + c] for c in range(chunks)], axis=1)

    l0, l1, l2 = natural(l0_ref, 0), natural(l1_ref, 0), natural(l2_ref, 1)
    o0, o1, o2 = natural(o0_ref, 0), natural(o1_ref, 2), natural(o2_ref, 3)
    mx = jnp.maximum(jnp.maximum(l0, l1), l2)
    e0, e1, e2 = jnp.exp(l0 - mx), jnp.exp(l1 - mx), jnp.exp(l2 - mx)
    oc = ((e0 * o0 + e1 * o1 + e2 * o2) / (e0 + e1 + e2)).astype(jnp.bfloat16)
    d = D_MODEL
    subs = range(0, tm, MERGE_SUB_ROWS)
    wa, wb, wc, wo = [ref[...].astype(jnp.bfloat16) for ref in (wa_ref, wb_ref, wc_ref, wo_ref)]
    branches = [(_dot(oa_ref[pl.ds(r0, MERGE_SUB_ROWS), :], wa),
                 _dot(ob_ref[pl.ds(r0, MERGE_SUB_ROWS), :], wb),
                 _dot(oc[r0:r0 + MERGE_SUB_ROWS], wc)) for r0 in subs]
    ys = []
    for r0, (ya, yb, yc) in zip(subs, branches):
        gates = jax.nn.sigmoid(zg_ref[pl.ds(r0, MERGE_SUB_ROWS), :].astype(jnp.float32))
        merged = gates[:, 0:d] * ya + gates[:, d:2 * d] * yb + gates[:, 2 * d:3 * d] * yc
        ys.append(_dot(merged.astype(jnp.bfloat16), wo))
    for r0, y in zip(subs, ys):
        rs = pl.ds(r0, MERGE_SUB_ROWS)
        out_ref[rs, :] = h_ref[rs, :] + mod_ref[2:3, :] * _rmsnorm(y, g_ref[...])


MERGE_SUB_ROWS = 256


def _merge_call(h, z, oa, ob, oc_lse, wa, wb, wc, wo, g, mod, layer):
    s, d = h.shape
    tm = 2 * MERGE_SUB_ROWS
    cw = C_HEADS_PER_GROUP * HEAD_DIM
    row = lambda w: pl.BlockSpec((tm, w), lambda i: (i, 0))
    res = lambda a: pl.BlockSpec((a.shape[0], tm // a.shape[0], cw), lambda i: (0, i, 0))
    full = lambda a: pl.BlockSpec(a.shape, lambda i: (0, 0), pipeline_mode=pl.Buffered(1))
    stacked = lambda a: pl.BlockSpec((None,) + a.shape[1:], lambda i: (layer, 0, 0), pipeline_mode=pl.Buffered(1))
    return pl.pallas_call(
        _merge_kernel,
        grid=(s // tm,),
        in_specs=[row(d), row(3 * d), row(oa.shape[1]), row(ob.shape[1])] + [res(a) for a in oc_lse]
        + [stacked(wa), stacked(wb), stacked(wc), stacked(wo), full(g), full(mod)],
        out_specs=row(d),
        out_shape=jax.ShapeDtypeStruct((s, d), jnp.float32),
        scratch_shapes=[pltpu.VMEM((4 * cw // LANES_V7X, tm, LANES_V7X), jnp.float32)],
        input_output_aliases={0: 0} if layer > 0 else {},
        compiler_params=_params(("arbitrary",)),
        name="merge_out_proj",
    )(h, z, oa, ob, *oc_lse, wa, wb, wc, wo, g, mod)


FFN_CHUNK = 1408


def _ffn_kernel(h_ref, g_pre_ref, g_post_ref, mod_ref, wg_ref, wu_ref, wd_ref, out_ref):
    x = h_ref[...]
    u = _rmsnorm(x, g_pre_ref[...]) * (1.0 + mod_ref[4:5, :]) + mod_ref[3:4, :]
    u = u.astype(jnp.bfloat16)
    y = jnp.zeros(x.shape, jnp.float32)
    for c0 in range(0, D_FF, FFN_CHUNK):
        gate = _dot(u, wg_ref[:, c0:c0 + FFN_CHUNK].astype(jnp.bfloat16))
        up = _dot(u, wu_ref[:, c0:c0 + FFN_CHUNK].astype(jnp.bfloat16))
        act = (gate * jax.nn.sigmoid(gate) * up).astype(jnp.bfloat16)
        y = y + _dot(act, wd_ref[c0:c0 + FFN_CHUNK, :].astype(jnp.bfloat16))
    out_ref[...] = x + mod_ref[5:6, :] * _rmsnorm(y, g_post_ref[...])


def _ffn_call(h, g_pre, g_post, mod, wg, wu, wd, layer):
    s, d = h.shape
    tm = ROW_TILE
    row = pl.BlockSpec((tm, d), lambda i: (i, 0))
    full = lambda a: pl.BlockSpec(a.shape, lambda i: (0, 0), pipeline_mode=pl.Buffered(1))
    stacked = lambda a: pl.BlockSpec((None,) + a.shape[1:], lambda i: (layer, 0, 0), pipeline_mode=pl.Buffered(1))
    return pl.pallas_call(
        _ffn_kernel,
        grid=(s // tm,),
        in_specs=[row, full(g_pre), full(g_post), full(mod), stacked(wg), stacked(wu), stacked(wd)],
        out_specs=row,
        out_shape=jax.ShapeDtypeStruct((s, d), jnp.float32),
        input_output_aliases={0: 0},
        compiler_params=_params(("arbitrary",)),
        name="swiglu_ffn",
    )(h, g_pre, g_post, mod, wg, wu, wd)


def _permute_in_proj(w_in):
    hd = HEAD_DIM
    o_bq = 3 * A_HEADS * hd
    o_bkv = o_bq + B_HEADS * hd
    o_c = o_bkv + 2 * B_KV_HEADS * hd
    o_g = o_c + 3 * C_HEADS * hd
    cw = C_HEADS_PER_GROUP * hd
    sl = lambda a, b: w_in[:, :, a:b]
    parts = [sl(o_g, o_g + 3 * D_MODEL), sl(0, A_HEADS * hd) * (QK_SCALE * LOG2E), sl(A_HEADS * hd, o_bq)]
    parts += [sl(o_bq + hh * hd, o_bq + (hh + 1) * hd) * QK_SCALE for hh in B_HEAD_ORDER]
    parts += [sl(o_bkv, o_c)]
    for g in range(len(C_GROUPS)):
        parts += [sl(o_c + g * 3 * cw, o_c + g * 3 * cw + cw) * QK_SCALE,
                  sl(o_c + g * 3 * cw + cw, o_c + (g + 1) * 3 * cw)]
    out = jnp.concatenate([part.astype(jnp.bfloat16) for part in parts], axis=2)
    assert out.shape[2] == IN_WIDTH
    return out


def kernel(x, c, w_ada, b_ada, g_pre_mix, g_post_mix, w_in, sinks, w_br_a, w_br_b, w_br_c,
           w_out, g_pre_ffn, g_post_ffn, w_gate, w_up, w_down):
    bn, s, d = x.shape
    assert bn == 1 and d == D_MODEL and s % SEQ_MULTIPLE == 0
    hd = HEAD_DIM
    w_in_p = _permute_in_proj(w_in)
    w_br_b_p = jnp.concatenate([w_br_b[:, hh * hd:(hh + 1) * hd] for hh in B_HEAD_ORDER], axis=1)

    mod_all = _ada_call(c, w_ada, b_ada).reshape(DEPTH, 6, d)
    h = x.reshape(s, d)
    for l in range(DEPTH):
        mod = mod_all[l]
        z, zc1, zc2 = _in_proj_call(h, g_pre_mix[l].reshape(1, d), mod, w_in_p, l)
        oa = _moba_call(z)
        ob = _swa_call(z, sinks[l])
        oc_lse = (_dilated_call(z.reshape(1, s, Z_WIDTH), 0, Z_C0)
                  + _dilated_call(zc1, 1, 0) + _dilated_call(zc2, 2, 0))
        h = _merge_call(h, z, oa, ob, oc_lse, w_br_a, w_br_b_p, w_br_c, w_out,
                        g_post_mix[l].reshape(1, d), mod, l)
        h = _ffn_call(h, g_pre_ffn[l].reshape(1, d), g_post_ffn[l].reshape(1, d), mod,
                      w_gate, w_up, w_down, l)
    return h.reshape(bn, s, d)
```
